```python
import jax, jax.numpy as jnp
from jax import lax
import numpy as np

D_MODEL = 2048
BATCH = 16
SEQ = 256
DEPTH = 4
DEC_BATCH = 4
DEC_SEQ = 4096
PAST_LEN = 512

GRID_W = 64
Q_BLOCK = 128
ROPE_BASE = 10000.0
EPS = 1e-6
A_HEADS = 8
A_KV_HEADS = 2
A_GROUP = A_HEADS // A_KV_HEADS
HEAD_DIM = 128
A_SCALE = HEAD_DIM ** -0.5
B_HEADS = 8
B_NOPE = 128
B_ROPE = 64
B_V = 128
KV_RANK = 512
B_SCALE = (B_NOPE + B_ROPE) ** -0.5
A_Q_W = A_HEADS * HEAD_DIM
A_KV_W = A_KV_HEADS * HEAD_DIM
B_Q_W = B_HEADS * (B_NOPE + B_ROPE)
ATTN_SPLITS = (A_Q_W, A_Q_W + A_KV_W, A_Q_W + 2 * A_KV_W, A_Q_W + 2 * A_KV_W + B_Q_W,
               A_Q_W + 2 * A_KV_W + B_Q_W + KV_RANK)
ATTN_IN = A_Q_W + 2 * A_KV_W + B_Q_W + KV_RANK + B_ROPE
ATTN_CAT = A_HEADS * HEAD_DIM + B_HEADS * B_V
CHUNK = 128
C_WIDTH = D_MODEL
C_GROUPS = 8
C_GROUP_W = C_WIDTH // C_GROUPS
FFN_HIDDEN = ((8 * D_MODEL + 3 * 256 - 1) // (3 * 256)) * 256
N_ATTN_LAYERS = (DEPTH + 1) // 2
N_CMLP_LAYERS = DEPTH // 2

kernel_name = "hybrid_diffusion_gqa_mla_chunkmlp_step"

F32 = jnp.float32


def rmsnorm(x, g):
    xf = x.astype(F32)
    y = xf * lax.rsqrt(jnp.mean(xf * xf, axis=-1, keepdims=True) + EPS)
    return (y * g.astype(F32)).astype(x.dtype)


def _rotate(x, ang):
    m = x.shape[-1] // 2
    xf = x.astype(F32)
    x1, x2 = xf[..., :m], xf[..., m:]
    shape = (1, ang.shape[0]) + (1,) * (x.ndim - 3) + (m,)
    cos = jnp.cos(ang).reshape(shape)
    sin = jnp.sin(ang).reshape(shape)
    return jnp.concatenate([x1 * cos - x2 * sin, x1 * sin + x2 * cos], axis=-1)


def axial_rope(x):
    n = x.shape[1]
    rows = n // GRID_W
    t = jnp.arange(rows * GRID_W)
    row = (t // GRID_W).astype(F32)
    col = (t % GRID_W).astype(F32)
    half = x.shape[-1] // 2
    inv = 1.0 / (ROPE_BASE ** (jnp.arange(0, half, 2, dtype=F32) / half))
    out = jnp.concatenate([_rotate(x[..., :half], row[:, None] * inv),
                           _rotate(x[..., half:], col[:, None] * inv)], axis=-1)
    return out.astype(x.dtype)


def blocked_attention(q, k, v, scale):
    b, sq, hk, g, dk = q.shape
    nb = sq // Q_BLOCK
    qb = jnp.moveaxis(q.reshape(b, nb, Q_BLOCK, hk, g, dk), 1, 0)

    def one_block(qi):
        s = jnp.einsum('bqhgd,bkhd->bhgqk', qi, k, preferred_element_type=F32) * scale
        p = jax.nn.softmax(s, axis=-1).astype(v.dtype)
        return jnp.einsum('bhgqk,bkhe->bqhge', p, v)

    o = lax.map(one_block, qb)
    return jnp.moveaxis(o, 0, 1).reshape(b, sq, hk, g, v.shape[-1])


def _attn_project(h, w_in, q_norm, k_norm, kv_norm):
    b, s, _ = h.shape
    z = h @ w_in
    qa, ka, va, qb, ckv, krope = jnp.split(z, list(ATTN_SPLITS), axis=-1)
    qa = rmsnorm(qa.reshape(b, s, A_KV_HEADS, A_GROUP, HEAD_DIM), q_norm)
    ka = rmsnorm(ka.reshape(b, s, A_KV_HEADS, HEAD_DIM), k_norm)
    va = va.reshape(b, s, A_KV_HEADS, HEAD_DIM)
    qb = qb.reshape(b, s, B_HEADS, B_NOPE + B_ROPE)
    ckv = rmsnorm(ckv, kv_norm)
    return qa, ka, va, qb, ckv, krope


def _mla_expand(ckv, krope, w_kv_up):
    b, s, _ = ckv.shape
    kv = (ckv @ w_kv_up).reshape(b, s, B_HEADS, B_NOPE + B_V)
    k_nope, v = kv[..., :B_NOPE], kv[..., B_NOPE:]
    k = jnp.concatenate([k_nope, jnp.broadcast_to(krope[:, :, None, :], (b, s, B_HEADS, B_ROPE))], axis=-1)
    return k, v


def attn_mixer(h, w_in, q_norm, k_norm, kv_norm, w_kv_up, w_out, ctx=None):
    b, s, _ = h.shape
    qa, ka, va, qb, ckv, krope = _attn_project(h, w_in, q_norm, k_norm, kv_norm)
    if ctx is None:
        keys_a, vals_a, ckv_all, krope_all = ka, va, ckv, krope
    else:
        ctx_k, ctx_v, ctx_ckv, ctx_krope = ctx
        qa = axial_rope(qa)
        qb = jnp.concatenate([qb[..., :B_NOPE], axial_rope(qb[..., B_NOPE:])], axis=-1)
        keys_a = jnp.concatenate([axial_rope(ka), ctx_k], axis=1)
        vals_a = jnp.concatenate([va, ctx_v], axis=1)
        ckv_all = jnp.concatenate([ckv, ctx_ckv], axis=1)
        krope_all = jnp.concatenate([axial_rope(krope[:, :, None, :])[:, :, 0, :], ctx_krope], axis=1)
    oa = blocked_attention(qa, keys_a, vals_a, A_SCALE)
    kb, vb = _mla_expand(ckv_all, krope_all, w_kv_up)
    ob = blocked_attention(qb[:, :, :, None, :], kb, vb, B_SCALE)
    o = jnp.concatenate([oa.reshape(b, s, A_HEADS * HEAD_DIM), ob.reshape(b, s, B_HEADS * B_V)], axis=-1)
    return o @ w_out, (ka, va, ckv, krope)


def chunk_mlp(h, w_in, v_norm, w_s, b_s, w_out):
    b, s, _ = h.shape
    z = jax.nn.gelu(h @ w_in)
    u, v = z[..., :C_WIDTH], z[..., C_WIDTH:]
    v = rmsnorm(v, v_norm).reshape(b, s // CHUNK, CHUNK, C_GROUPS, C_GROUP_W)
    sv = jnp.einsum('gpq,bnqgc->bnpgc', w_s, v) + b_s.T[None, None, :, :, None]
    return (u * sv.reshape(b, s, C_WIDTH)) @ w_out


def swiglu(h, w_gate, w_up, w_down):
    return (jax.nn.silu(h @ w_gate) * (h @ w_up)) @ w_down


def ada_mod(cond, w, bias):
    m = jax.nn.silu(cond) @ w + bias
    return jnp.split(m[:, None, :], 6, axis=-1)


def modulate(x, g, shift, scale):
    return rmsnorm(x, g) * (1 + scale) + shift


def setup_inputs(seed: int = 0) -> dict:
    key = jax.random.key(seed)
    ks = jax.random.split(key, 32)

    def nrm(k, shape, scale):
        return jax.random.normal(k, shape, F32) * scale

    def gain(k, shape):
        return 1.0 + 0.02 * jax.random.normal(k, shape, F32)

    D = D_MODEL
    return {
        'x_prompt': nrm(ks[0], (BATCH, SEQ, D), 1.0),
        'x_sample': nrm(ks[1], (DEC_BATCH, DEC_SEQ, D), 1.0),
        'c': nrm(ks[2], (DEC_BATCH, D), 1.0),
        'cache_gqa_k': nrm(ks[3], (DEC_BATCH, N_ATTN_LAYERS, PAST_LEN, A_KV_HEADS, HEAD_DIM), 1.0),
        'cache_gqa_v': nrm(ks[4], (DEC_BATCH, N_ATTN_LAYERS, PAST_LEN, A_KV_HEADS, HEAD_DIM), 1.0),
        'cache_mla_ckv': nrm(ks[5], (DEC_BATCH, N_ATTN_LAYERS, PAST_LEN, KV_RANK), 1.0),
        'cache_mla_krope': nrm(ks[6], (DEC_BATCH, N_ATTN_LAYERS, PAST_LEN, B_ROPE), 1.0),
        'c_ctx': nrm(ks[7], (D,), 1.0),
        'ada_w': nrm(ks[8], (DEPTH, D, 6 * D), 0.5 * D ** -0.5),
        'ada_b': nrm(ks[9], (DEPTH, 6 * D), 0.02),
        'norm_mix': gain(ks[10], (DEPTH, D)),
        'norm_ffn': gain(ks[11], (DEPTH, D)),
        'ffn_gate': nrm(ks[12], (DEPTH, D, FFN_HIDDEN), D ** -0.5),
        'ffn_up': nrm(ks[13], (DEPTH, D, FFN_HIDDEN), D ** -0.5),
        'ffn_down': nrm(ks[14], (DEPTH, FFN_HIDDEN, D), FFN_HIDDEN ** -0.5),
        'attn_w_in': nrm(ks[15], (N_ATTN_LAYERS, D, ATTN_IN), D ** -0.5),
        'attn_q_norm': gain(ks[16], (N_ATTN_LAYERS, HEAD_DIM)),
        'attn_k_norm': gain(ks[17], (N_ATTN_LAYERS, HEAD_DIM)),
        'attn_kv_norm': gain(ks[18], (N_ATTN_LAYERS, KV_RANK)),
        'attn_w_kv_up': nrm(ks[19], (N_ATTN_LAYERS, KV_RANK, B_HEADS * (B_NOPE + B_V)), KV_RANK ** -0.5),
        'attn_w_out': nrm(ks[20], (N_ATTN_LAYERS, ATTN_CAT, D), ATTN_CAT ** -0.5),
        'cmlp_w_in': nrm(ks[21], (N_CMLP_LAYERS, D, 2 * C_WIDTH), D ** -0.5),
        'cmlp_v_norm': gain(ks[22], (N_CMLP_LAYERS, C_WIDTH)),
        'cmlp_w_s': nrm(ks[23], (N_CMLP_LAYERS, C_GROUPS, CHUNK, CHUNK), CHUNK ** -0.5),
        'cmlp_b_s': gain(ks[24], (N_CMLP_LAYERS, C_GROUPS, CHUNK)),
        'cmlp_w_out': nrm(ks[25], (N_CMLP_LAYERS, C_WIDTH, D), C_WIDTH ** -0.5),
        'final_norm': gain(ks[26], (D,)),
    }


def reference(x_prompt, x_sample, c, cache_gqa_k, cache_gqa_v, cache_mla_ckv, cache_mla_krope,
              c_ctx, ada_w, ada_b, norm_mix, norm_ffn, ffn_gate, ffn_up, ffn_down,
              attn_w_in, attn_q_norm, attn_k_norm, attn_kv_norm, attn_w_kv_up, attn_w_out,
              cmlp_w_in, cmlp_v_norm, cmlp_w_s, cmlp_b_s, cmlp_w_out, final_norm):
    xp, xs = x_prompt, x_sample
    cond_p = jnp.broadcast_to(c_ctx[None, :], (xp.shape[0], c_ctx.shape[0]))
    new_k, new_v, new_ckv, new_kr = [], [], [], []
    for i in range(DEPTH):
        j = i // 2
        sh1p, sc1p, g1p, sh2p, sc2p, g2p = ada_mod(cond_p, ada_w[i], ada_b[i])
        sh1s, sc1s, g1s, sh2s, sc2s, g2s = ada_mod(c, ada_w[i], ada_b[i])
        hp = modulate(xp, norm_mix[i], sh1p, sc1p)
        hs = modulate(xs, norm_mix[i], sh1s, sc1s)
        if i % 2 == 0:
            wts = (attn_w_in[j], attn_q_norm[j], attn_k_norm[j], attn_kv_norm[j], attn_w_kv_up[j], attn_w_out[j])
            op, (k_c, v_c, ckv_c, kr_c) = attn_mixer(hp, *wts)
            new_k.append(k_c)
            new_v.append(v_c)
            new_ckv.append(ckv_c)
            new_kr.append(kr_c)
            os_, _ = attn_mixer(hs, *wts, ctx=(cache_gqa_k[:, j], cache_gqa_v[:, j],
                                               cache_mla_ckv[:, j], cache_mla_krope[:, j]))
        else:
            wts = (cmlp_w_in[j], cmlp_v_norm[j], cmlp_w_s[j], cmlp_b_s[j], cmlp_w_out[j])
            op = chunk_mlp(hp, *wts)
            os_ = chunk_mlp(hs, *wts)
        xp = xp + g1p * op
        xs = xs + g1s * os_
        fw = (ffn_gate[i], ffn_up[i], ffn_down[i])
        xp = xp + g2p * swiglu(modulate(xp, norm_ffn[i], sh2p, sc2p), *fw)
        xs = xs + g2s * swiglu(modulate(xs, norm_ffn[i], sh2s, sc2s), *fw)
    y_prompt = rmsnorm(xp, final_norm)
    y_sample = rmsnorm(xs, final_norm)
    state_gqa_k = jnp.stack(new_k, axis=1)
    state_gqa_v = jnp.stack(new_v, axis=1)
    state_mla_ckv = jnp.stack(new_ckv, axis=1)
    state_mla_krope = jnp.stack(new_kr, axis=1)
    return (y_prompt, y_sample, state_gqa_k, state_gqa_v, state_mla_ckv, state_mla_krope)
```

```python
import functools

import numpy as np
import jax
import jax.numpy as jnp
from jax import lax
from jax.experimental import pallas as pl
from jax.experimental.pallas import tpu as pltpu

F32 = jnp.float32
BF16 = jnp.bfloat16

D_MODEL = 2048
DEPTH = 4
GRID_W = 64
ROPE_BASE = 10000.0
EPS = 1e-6
A_HEADS = 8
A_KV_HEADS = 2
HEAD_DIM = 128
A_SCALE = HEAD_DIM ** -0.5
B_HEADS = 8
B_NOPE = 128
B_ROPE = 64
B_V = 128
KV_RANK = 512
B_SCALE = (B_NOPE + B_ROPE) ** -0.5
B_QK_PAD = 256
A_Q_W = A_HEADS * HEAD_DIM
A_KV_W = A_KV_HEADS * HEAD_DIM
B_Q_W = B_HEADS * (B_NOPE + B_ROPE)
CHUNK = 128
C_GROUPS = 8
C_GROUP_W = D_MODEL // C_GROUPS
FFN_HIDDEN = 5632
LOG2E = 1.4426950408889634

Z_QA = 0
Z_KA = Z_QA + A_Q_W
Z_VA = Z_KA + A_KV_W
Z_QB = Z_VA + A_KV_W
Z_CKV = Z_QB + B_HEADS * B_QK_PAD
Z_KR = Z_CKV + KV_RANK
Z_W = Z_KR + 128

MOD_ROWS = 8


def _cparams(sem, vmem_mib):
    return pltpu.CompilerParams(dimension_semantics=sem, vmem_limit_bytes=vmem_mib << 20)


def _resident(shape):
    n = len(shape)
    return pl.BlockSpec(shape, lambda *_: (0,) * n, pipeline_mode=pl.Buffered(1))


def _rms(x):
    return x * lax.rsqrt(jnp.mean(x * x, axis=-1, keepdims=True) + EPS)


def _modulate(x, g, shift, scale):
    return (_rms(x) * g) * (1.0 + scale) + shift


def _swap_halves(v, n):
    lane = lax.broadcasted_iota(jnp.int32, v.shape, 1)
    fwd = pltpu.roll(v, 128 - n, 1)
    bwd = pltpu.roll(v, n, 1)
    return jnp.where((lane & (2 * n - 1)) < n, fwd, bwd)


def _ada_kernel(cond_ref, w_ref, b_ref, o_ref):
    c = cond_ref[...]
    a = (c * jax.nn.sigmoid(c)).astype(BF16)
    o_ref[...] = jnp.dot(a, w_ref[...].astype(BF16), preferred_element_type=F32) + b_ref[...]


def _ada_mod(cond, ada_w, ada_b):
    tn = 1024
    n6 = 6 * D_MODEL
    out = pl.pallas_call(
        _ada_kernel,
        grid=(DEPTH, n6 // tn),
        in_specs=[
            pl.BlockSpec((MOD_ROWS, D_MODEL), lambda l, n: (0, 0)),
            pl.BlockSpec((None, D_MODEL, tn), lambda l, n: (l, 0, n)),
            pl.BlockSpec((None, 1, tn), lambda l, n: (l, 0, n)),
        ],
        out_specs=pl.BlockSpec((None, MOD_ROWS, tn), lambda l, n: (l, 0, n)),
        out_shape=jax.ShapeDtypeStruct((DEPTH, MOD_ROWS, n6), F32),
        compiler_params=_cparams(("arbitrary", "arbitrary"), 40),
        name="ada_mod",
    )(cond, ada_w, ada_b.reshape(DEPTH, 1, n6))
    return out.reshape(DEPTH, MOD_ROWS, 6, D_MODEL)


def _ffn_kernel(*refs, n_chunks, final):
    if final:
        x_ref, mod_ref, g_ref, wg_ref, wu_ref, wd_ref, fn_ref, o_ref, xb_ref = refs
    else:
        x_ref, mod_ref, g_ref, wg_ref, wu_ref, wd_ref, o_ref, xb_ref = refs
    j = pl.program_id(2)

    @pl.when(j == 0)
    def _():
        xm = _modulate(x_ref[...], g_ref[...], mod_ref[3:4, :], mod_ref[4:5, :])
        xb_ref[...] = xm.astype(BF16)
        o_ref[...] = jnp.zeros_like(o_ref)

    xb = xb_ref[...]
    gate = jnp.dot(xb, wg_ref[...], preferred_element_type=F32)
    up = jnp.dot(xb, wu_ref[...], preferred_element_type=F32)
    h = (gate * jax.nn.sigmoid(gate) * up).astype(BF16)
    o_ref[...] += jnp.dot(h, wd_ref[...], preferred_element_type=F32)

    @pl.when(j == n_chunks - 1)
    def _():
        y = x_ref[...] + mod_ref[5:6, :] * o_ref[...]
        if final:
            y = _rms(y) * fn_ref[...]
        o_ref[...] = y


def _ffn(x, mod, mod_row0, g, wg, wu, wd, final_g=None):
    nb, s, d = x.shape
    tm, tc = 512, 512
    n_chunks = FFN_HIDDEN // tc
    final = final_g is not None
    in_specs = [
        pl.BlockSpec((None, tm, d), lambda b, i, j: (b, i, 0)),
        pl.BlockSpec((None, 6, d), lambda b, i, j: (mod_row0 + b, 0, 0)),
        pl.BlockSpec((1, d), lambda b, i, j: (0, 0)),
        pl.BlockSpec((d, tc), lambda b, i, j: (0, j)),
        pl.BlockSpec((d, tc), lambda b, i, j: (0, j)),
        pl.BlockSpec((tc, d), lambda b, i, j: (j, 0)),
    ]
    args = [x, mod, g, wg, wu, wd]
    if final:
        in_specs.append(pl.BlockSpec((1, d), lambda b, i, j: (0, 0)))
        args.append(final_g)
    return pl.pallas_call(
        functools.partial(_ffn_kernel, n_chunks=n_chunks, final=final),
        grid=(nb, s // tm, n_chunks),
        in_specs=in_specs,
        out_specs=pl.BlockSpec((None, tm, d), lambda b, i, j: (b, i, 0)),
        out_shape=jax.ShapeDtypeStruct((nb, s, d), F32),
        scratch_shapes=[pltpu.VMEM((tm, d), BF16)],
        compiler_params=_cparams(("parallel", "parallel", "arbitrary"), 48),
        name="ffn_final" if final else "ffn",
    )(*args)


def _proj_kernel(*refs, prompt):
    if prompt:
        (x_ref, mod_ref, g_ref, w_ref, wkv_ref, qn_ref, kn_ref, kvn_ref,
         qa_o, ka_o, va_o, qb_o, kb_o, vb_o, sk_o, sv_o, sc_o, sr_o) = refs
    else:
        (x_ref, mod_ref, g_ref, w_ref, wkv_ref, qn_ref, kn_ref, kvn_ref,
         c128_ref, s128_ref, c64_ref, s64_ref,
         qa_o, ka_o, va_o, qb_o, kb_o, vb_o) = refs
        c128, s128 = c128_ref[...], s128_ref[...]
        c64, s64 = c64_ref[...], s64_ref[...]

    def rope128(v):
        return v if prompt else v * c128 + _swap_halves(v, 32) * s128

    def rope64(v):
        return v if prompt else v * c64 + _swap_halves(v, 16) * s64

    xm = _modulate(x_ref[...], g_ref[...], mod_ref[0:1, :], mod_ref[1:2, :])
    z = jnp.dot(xm.astype(BF16), w_ref[...], preferred_element_type=F32)

    for h in range(A_HEADS):
        sl = slice(h * HEAD_DIM, (h + 1) * HEAD_DIM)
        q = _rms(z[:, Z_QA + h * HEAD_DIM:Z_QA + (h + 1) * HEAD_DIM]) * qn_ref[...]
        qa_o[:, sl] = rope128(q).astype(BF16)
    for h in range(A_KV_HEADS):
        sl = slice(h * HEAD_DIM, (h + 1) * HEAD_DIM)
        k = _rms(z[:, Z_KA + h * HEAD_DIM:Z_KA + (h + 1) * HEAD_DIM]) * kn_ref[...]
        if prompt:
            sk_o[:, sl] = k
        ka_o[:, sl] = rope128(k).astype(BF16)
    va = z[:, Z_VA:Z_VA + A_KV_W]
    if prompt:
        sv_o[...] = va
    va_o[...] = va.astype(BF16)
    for h in range(B_HEADS):
        base = Z_QB + h * B_QK_PAD
        qb_o[:, h * B_QK_PAD:h * B_QK_PAD + B_NOPE] = z[:, base:base + B_NOPE].astype(BF16)
        qb_o[:, h * B_QK_PAD + B_NOPE:(h + 1) * B_QK_PAD] = (
            rope64(z[:, base + B_NOPE:base + B_QK_PAD]).astype(BF16))

    ckv = _rms(z[:, Z_CKV:Z_CKV + KV_RANK]) * kvn_ref[...]
    kr = z[:, Z_KR:Z_KR + 128]
    if prompt:
        sc_o[...] = ckv
        sr_o[...] = kr[:, :B_ROPE]
    krb = rope64(kr).astype(BF16)
    kvu = jnp.dot(ckv.astype(BF16), wkv_ref[...], preferred_element_type=F32)
    for h in range(B_HEADS):
        kb_o[:, h * B_QK_PAD:h * B_QK_PAD + B_NOPE] = kvu[:, h * B_NOPE:(h + 1) * B_NOPE].astype(BF16)
        kb_o[:, h * B_QK_PAD + B_NOPE:(h + 1) * B_QK_PAD] = krb
    vb_o[...] = kvu[:, B_HEADS * B_NOPE:].astype(BF16)


def _attn_project(x, mod, mod_row0, g, w_main, w_kvup, qn, kn, kvn, rope=None):
    nb, s, d = x.shape
    tm = 256
    prompt = rope is None

    def row(w):
        return pl.BlockSpec((None, tm, w), lambda b, i: (b, i, 0))

    in_specs = [
        row(d),
        pl.BlockSpec((None, 6, d), lambda b, i: (mod_row0 + b, 0, 0)),
        pl.BlockSpec((1, d), lambda b, i: (0, 0)),
        _resident((d, Z_W)),
        _resident((KV_RANK, 2 * B_HEADS * B_NOPE)),
        pl.BlockSpec((1, HEAD_DIM), lambda b, i: (0, 0)),
        pl.BlockSpec((1, HEAD_DIM), lambda b, i: (0, 0)),
        pl.BlockSpec((1, KV_RANK), lambda b, i: (0, 0)),
    ]
    args = [x, mod, g, w_main, w_kvup, qn, kn, kvn]
    widths = [A_Q_W, A_KV_W, A_KV_W, B_HEADS * B_QK_PAD, B_HEADS * B_QK_PAD, B_HEADS * B_V]
    out_specs = [row(w) for w in widths]
    out_shape = [jax.ShapeDtypeStruct((nb, s, w), BF16) for w in widths]
    if prompt:
        for w in (A_KV_W, A_KV_W, KV_RANK, B_ROPE):
            out_specs.append(row(w))
            out_shape.append(jax.ShapeDtypeStruct((nb, s, w), F32))
    else:
        in_specs += [pl.BlockSpec((tm, 128), lambda b, i: (i, 0))] * 4
        args += list(rope)
    return pl.pallas_call(
        functools.partial(_proj_kernel, prompt=prompt),
        grid=(nb, s // tm),
        in_specs=in_specs,
        out_specs=out_specs,
        out_shape=out_shape,
        compiler_params=_cparams(("parallel", "parallel"), 52),
        name="attn_proj_prompt" if prompt else "attn_proj_latent",
    )(*args)


def _ctx_kv_kernel(ckv_ref, kr_ref, wkv_ref, kb_o, vb_o):
    kvu = jnp.dot(ckv_ref[...].astype(BF16), wkv_ref[...], preferred_element_type=F32)
    krb = kr_ref[...].astype(BF16)
    zero = jnp.zeros((krb.shape[0], B_QK_PAD - B_NOPE - B_ROPE), BF16)
    for h in range(B_HEADS):
        base = h * B_QK_PAD
        kb_o[:, base:base + B_NOPE] = kvu[:, h * B_NOPE:(h + 1) * B_NOPE].astype(BF16)
        kb_o[:, base + B_NOPE:base + B_NOPE + B_ROPE] = krb
        kb_o[:, base + B_NOPE + B_ROPE:base + B_QK_PAD] = zero
    vb_o[...] = kvu[:, B_HEADS * B_NOPE:].astype(BF16)


def _ctx_kv_expand(cache_ckv, cache_kr, layer, w_kvup):
    nb, _, p, _ = cache_ckv.shape
    return pl.pallas_call(
        _ctx_kv_kernel,
        grid=(nb,),
        in_specs=[
            pl.BlockSpec((None, None, p, KV_RANK), lambda b: (b, layer, 0, 0)),
            pl.BlockSpec((None, None, p, B_ROPE), lambda b: (b, layer, 0, 0)),
            pl.BlockSpec((KV_RANK, 2 * B_HEADS * B_NOPE), lambda b: (0, 0)),
        ],
        out_specs=[
            pl.BlockSpec((None, p, B_HEADS * B_QK_PAD), lambda b: (b, 0, 0)),
            pl.BlockSpec((None, p, B_HEADS * B_V), lambda b: (b, 0, 0)),
        ],
        out_shape=[
            jax.ShapeDtypeStruct((nb, p, B_HEADS * B_QK_PAD), BF16),
            jax.ShapeDtypeStruct((nb, p, B_HEADS * B_V), BF16),
        ],
        compiler_params=_cparams(("parallel",), 32),
        name="ctx_kv_expand",
    )(cache_ckv, cache_kr, w_kvup)


def _attn_kernel(*refs, nseg, hb, group, dq, dv, tq, exp2_scale):
    q_ref = refs[0]
    kv_refs = refs[1:1 + 2 * nseg]
    o_ref = refs[1 + 2 * nseg]
    n_sub = q_ref.shape[0] // tq

    for h in range(hb):
        kvh = h // group

        def body(t, carry, h=h, kvh=kvh):
            r = pl.multiple_of(t * tq, tq)
            q = q_ref[pl.ds(r, tq), h * dq:(h + 1) * dq]
            scores = []
            for i in range(nseg):
                k = kv_refs[2 * i][:, kvh * dq:(kvh + 1) * dq].astype(BF16)
                scores.append(lax.dot_general(q, k, (((1,), (1,)), ((), ())),
                                              preferred_element_type=F32))
            m = functools.reduce(jnp.maximum, [jnp.max(sc, axis=-1, keepdims=True) for sc in scores])
            denom = None
            acc = None
            for i in range(nseg):
                p = jnp.exp2((scores[i] - m) * exp2_scale)
                v = kv_refs[2 * i + 1][:, kvh * dv:(kvh + 1) * dv].astype(BF16)
                li = jnp.sum(p, axis=-1, keepdims=True)
                oi = jnp.dot(p.astype(BF16), v, preferred_element_type=F32)
                denom = li if denom is None else denom + li
                acc = oi if acc is None else acc + oi
            o_ref[pl.ds(r, tq), h * dv:(h + 1) * dv] = (acc / denom).astype(o_ref.dtype)
            return carry

        lax.fori_loop(0, n_sub, body, 0)


def _attention(q, segs, *, hb, group, dq, scale, tqb, seg_layer=None):
    nb, s, qw = q.shape
    dv = HEAD_DIM
    n_heads = qw // dq
    nkv = hb // group
    tq = min(256, tqb)
    in_specs = [pl.BlockSpec((None, tqb, hb * dq), lambda b, h, t: (b, t, h))]
    args = [q]
    for k, v in segs:
        for a, w in ((k, dq), (v, dv)):
            if a.ndim == 4:
                in_specs.append(pl.BlockSpec((None, None, a.shape[2], nkv * w),
                                             lambda b, h, t: (b, seg_layer, 0, h)))
            else:
                in_specs.append(pl.BlockSpec((None, a.shape[1], nkv * w), lambda b, h, t: (b, 0, h)))
            args.append(a)
    return pl.pallas_call(
        functools.partial(_attn_kernel, nseg=len(segs), hb=hb, group=group, dq=dq, dv=dv, tq=tq,
                          exp2_scale=scale * LOG2E),
        grid=(nb, n_heads // hb, s // tqb),
        in_specs=in_specs,
        out_specs=pl.BlockSpec((None, tqb, hb * dv), lambda b, h, t: (b, t, h)),
        out_shape=jax.ShapeDtypeStruct((nb, s, n_heads * dv), BF16),
        compiler_params=_cparams(("parallel", "parallel", "arbitrary"), 48),
        name="attention",
    )(*args)


def _attn_out_kernel(x_ref, mod_ref, oa_ref, ob_ref, w_ref, o_ref):
    y = jnp.dot(oa_ref[...], w_ref[:A_Q_W, :], preferred_element_type=F32)
    y = y + jnp.dot(ob_ref[...], w_ref[A_Q_W:, :], preferred_element_type=F32)
    o_ref[...] = x_ref[...] + mod_ref[2:3, :] * y


def _attn_out(x, mod, mod_row0, oa, ob, w_out):
    nb, s, d = x.shape
    tm = 512
    return pl.pallas_call(
        _attn_out_kernel,
        grid=(nb, s // tm),
        in_specs=[
            pl.BlockSpec((None, tm, d), lambda b, i: (b, i, 0)),
            pl.BlockSpec((None, 6, d), lambda b, i: (mod_row0 + b, 0, 0)),
            pl.BlockSpec((None, tm, A_Q_W), lambda b, i: (b, i, 0)),
            pl.BlockSpec((None, tm, B_HEADS * B_V), lambda b, i: (b, i, 0)),
            _resident((A_Q_W + B_HEADS * B_V, d)),
        ],
        out_specs=pl.BlockSpec((None, tm, d), lambda b, i: (b, i, 0)),
        out_shape=jax.ShapeDtypeStruct((nb, s, d), F32),
        compiler_params=_cparams(("parallel", "parallel"), 40),
        name="attn_out",
    )(x, mod, oa, ob, w_out)


def _gelu_tanh(x):
    return 0.5 * x * (1.0 + jnp.tanh(0.7978845608028654 * (x + 0.044715 * (x * x * x))))


def _cmlp_kernel(x_ref, mod_ref, g_ref, win_ref, vn_ref, ws_ref, bias_ref, wout_ref, o_ref, t_ref):
    x = x_ref[...]
    xm = _modulate(x, g_ref[...], mod_ref[0:1, :], mod_ref[1:2, :])
    z = _gelu_tanh(jnp.dot(xm.astype(BF16), win_ref[...], preferred_element_type=F32))
    u = z[:, :D_MODEL]
    vb = (_rms(z[:, D_MODEL:]) * vn_ref[...]).astype(BF16)
    for n in range(x.shape[0] // CHUNK):
        rows = slice(n * CHUNK, (n + 1) * CHUNK)
        for g in range(C_GROUPS):
            cols = slice(g * C_GROUP_W, (g + 1) * C_GROUP_W)
            sv = jnp.dot(ws_ref[g], vb[rows, cols], preferred_element_type=F32) + bias_ref[:, cols]
            t_ref[rows, cols] = (u[rows, cols] * sv).astype(BF16)
    y = jnp.dot(t_ref[...], wout_ref[...], preferred_element_type=F32)
    o_ref[...] = x + mod_ref[2:3, :] * y


def _chunk_mlp(x, mod, mod_row0, g, w_in, v_norm, w_s, bias, w_out):
    nb, s, d = x.shape
    tm = 256
    return pl.pallas_call(
        _cmlp_kernel,
        grid=(nb, s // tm),
        in_specs=[
            pl.BlockSpec((None, tm, d), lambda b, i: (b, i, 0)),
            pl.BlockSpec((None, 6, d), lambda b, i: (mod_row0 + b, 0, 0)),
            pl.BlockSpec((1, d), lambda b, i: (0, 0)),
            _resident((d, 2 * d)),
            pl.BlockSpec((1, d), lambda b, i: (0, 0)),
            _resident((C_GROUPS, CHUNK, CHUNK)),
            _resident((CHUNK, d)),
            _resident((d, d)),
        ],
        out_specs=pl.BlockSpec((None, tm, d), lambda b, i: (b, i, 0)),
        out_shape=jax.ShapeDtypeStruct((nb, s, d), F32),
        scratch_shapes=[pltpu.VMEM((tm, d), BF16)],
        compiler_params=_cparams(("parallel", "parallel"), 52),
        name="chunk_mlp",
    )(x, mod, g, w_in, v_norm, w_s, bias, w_out)


def _rope_tables(n_tokens):
    t = np.arange(n_tokens)
    row = (t // GRID_W).astype(np.float32)
    col = (t % GRID_W).astype(np.float32)

    def axis_tables(pos, half):
        inv = (1.0 / (ROPE_BASE ** (np.arange(0, half, 2, dtype=np.float32) / half))).astype(np.float32)
        ang = (pos[:, None] * inv).astype(np.float32).astype(np.float64)
        c, s = np.cos(ang), np.sin(ang)
        return np.concatenate([c, c], axis=1), np.concatenate([-s, s], axis=1)

    def tables(dim):
        cr, sr = axis_tables(row, dim // 2)
        cc, sc = axis_tables(col, dim // 2)
        return np.concatenate([cr, cc], axis=1), np.concatenate([sr, sc], axis=1)

    c128, s128 = tables(HEAD_DIM)
    c64, s64 = tables(B_ROPE)
    pad = np.zeros((n_tokens, 128 - B_ROPE))
    c64 = np.concatenate([c64, pad], axis=1)
    s64 = np.concatenate([s64, pad], axis=1)
    return tuple(jnp.asarray(a, F32) for a in (c128, s128, c64, s64))


def _layout_attn_w_in(w):
    d = w.shape[0]
    o = A_Q_W + 2 * A_KV_W
    qb = w[:, o:o + B_Q_W].reshape(d, B_HEADS, B_NOPE + B_ROPE)
    qb = jnp.pad(qb, ((0, 0), (0, 0), (0, B_QK_PAD - B_NOPE - B_ROPE))).reshape(d, B_HEADS * B_QK_PAD)
    ckv = w[:, o + B_Q_W:o + B_Q_W + KV_RANK]
    kr = jnp.pad(w[:, o + B_Q_W + KV_RANK:], ((0, 0), (0, 128 - B_ROPE)))
    return jnp.concatenate([w[:, :o], qb, ckv, kr], axis=1).astype(BF16)


def _layout_kv_up(w):
    w3 = w.reshape(KV_RANK, B_HEADS, B_NOPE + B_V)
    return jnp.concatenate([w3[:, :, :B_NOPE].reshape(KV_RANK, -1),
                            w3[:, :, B_NOPE:].reshape(KV_RANK, -1)], axis=1).astype(BF16)


def kernel(x_prompt, x_sample, c, cache_gqa_k, cache_gqa_v, cache_mla_ckv, cache_mla_krope, c_ctx, ada_w, ada_b, norm_mix, norm_ffn, ffn_gate, ffn_up, ffn_down, attn_w_in, attn_q_norm, attn_k_norm, attn_kv_norm, attn_w_kv_up, attn_w_out, cmlp_w_in, cmlp_v_norm, cmlp_w_s, cmlp_b_s, cmlp_w_out, final_norm):
    n_p, s_p, d = x_prompt.shape
    n_s, s_s, _ = x_sample.shape
    past = cache_gqa_k.shape[2]

    cond = jnp.concatenate([c_ctx[None, :], c, jnp.zeros((MOD_ROWS - 1 - n_s, d), F32)], axis=0)
    mods = _ada_mod(cond, ada_w, ada_b)

    xp = x_prompt.reshape(1, n_p * s_p, d)
    xs = x_sample
    rope = _rope_tables(s_s)
    cache_k = cache_gqa_k.reshape(n_s, -1, past, A_KV_W)
    cache_v = cache_gqa_v.reshape(n_s, -1, past, A_KV_W)
    final_g = final_norm.reshape(1, d)

    states = ([], [], [], [])
    for i in range(DEPTH):
        j = i // 2
        mod = mods[i]
        g_mix = norm_mix[i].reshape(1, d)
        if i % 2 == 0:
            w_main = _layout_attn_w_in(attn_w_in[j])
            w_kvup = _layout_kv_up(attn_w_kv_up[j])
            w_out = attn_w_out[j].astype(BF16)
            qn = attn_q_norm[j].reshape(1, HEAD_DIM)
            kn = attn_k_norm[j].reshape(1, HEAD_DIM)
            kvn = attn_kv_norm[j].reshape(1, KV_RANK)

            qa, ka, va, qb, kb, vb, st_k, st_v, st_c, st_r = _attn_project(
                xp, mod, 0, g_mix, w_main, w_kvup, qn, kn, kvn)
            for lst, st in zip(states, (st_k, st_v, st_c, st_r)):
                lst.append(st)

            def per_seq(a):
                return a.reshape(n_p, s_p, a.shape[-1])

            oa = _attention(per_seq(qa), [(per_seq(ka), per_seq(va))], hb=A_HEADS,
                            group=A_HEADS // A_KV_HEADS, dq=HEAD_DIM, scale=A_SCALE, tqb=s_p)
            ob = _attention(per_seq(qb), [(per_seq(kb), per_seq(vb))], hb=B_HEADS, group=1,
                            dq=B_QK_PAD, scale=B_SCALE, tqb=s_p)
            xp = _attn_out(xp, mod, 0, oa.reshape(1, n_p * s_p, -1), ob.reshape(1, n_p * s_p, -1), w_out)

            qa, ka, va, qb, kb, vb = _attn_project(xs, mod, 1, g_mix, w_main, w_kvup, qn, kn, kvn, rope=rope)
            kb_ctx, vb_ctx = _ctx_kv_expand(cache_mla_ckv, cache_mla_krope, j, w_kvup)
            oa = _attention(qa, [(ka, va), (cache_k, cache_v)], hb=A_HEADS // A_KV_HEADS,
                            group=A_HEADS // A_KV_HEADS, dq=HEAD_DIM, scale=A_SCALE, tqb=1024, seg_layer=j)
            ob = _attention(qb, [(kb, vb), (kb_ctx, vb_ctx)], hb=2, group=1, dq=B_QK_PAD,
                            scale=B_SCALE, tqb=1024)
            xs = _attn_out(xs, mod, 1, oa, ob, w_out)
        else:
            w_in = cmlp_w_in[j].astype(BF16)
            w_out = cmlp_w_out[j].astype(BF16)
            w_s = cmlp_w_s[j].astype(BF16)
            vn = cmlp_v_norm[j].reshape(1, d)
            bias = jnp.repeat(cmlp_b_s[j].T, C_GROUP_W, axis=1)
            xp = _chunk_mlp(xp, mod, 0, g_mix, w_in, vn, w_s, bias, w_out)
            xs = _chunk_mlp(xs, mod, 1, g_mix, w_in, vn, w_s, bias, w_out)

        g_ffn = norm_ffn[i].reshape(1, d)
        wg = ffn_gate[i].astype(BF16)
        wu = ffn_up[i].astype(BF16)
        wd = ffn_down[i].astype(BF16)
        fin = final_g if i == DEPTH - 1 else None
        xp = _ffn(xp, mod, 0, g_ffn, wg, wu, wd, fin)
        xs = _ffn(xs, mod, 1, g_ffn, wg, wu, wd, fin)

    y_prompt = xp.reshape(n_p, s_p, d)
    st_k, st_v, st_c, st_r = (jnp.stack([a.reshape(n_p, s_p, a.shape[-1]) for a in lst], axis=1)
                              for lst in states)
    return (y_prompt, xs,
            st_k.reshape(n_p, -1, s_p, A_KV_HEADS, HEAD_DIM),
            st_v.reshape(n_p, -1, s_p, A_KV_HEADS, HEAD_DIM),
            st_c, st_r)
```

```python
import functools
from typing import NamedTuple

import numpy as np
import jax
import jax.numpy as jnp
from jax import lax
from jax.experimental import pallas as pl
from jax.experimental.pallas import tpu as pltpu

F32 = jnp.float32
BF16 = jnp.bfloat16

D_MODEL = 2048
DEPTH = 4
GRID_W = 64
ROPE_BASE = 10000.0
EPS = 1e-6
A_HEADS = 8
A_KV_HEADS = 2
HEAD_DIM = 128
A_SCALE = HEAD_DIM ** -0.5
B_HEADS = 8
B_NOPE = 128
B_ROPE = 64
B_V = 128
KV_RANK = 512
B_SCALE = (B_NOPE + B_ROPE) ** -0.5
B_QK_PAD = 256
A_Q_W = A_HEADS * HEAD_DIM
A_KV_W = A_KV_HEADS * HEAD_DIM
B_Q_W = B_HEADS * (B_NOPE + B_ROPE)
CHUNK = 128
C_GROUPS = 8
C_GROUP_W = D_MODEL // C_GROUPS
FFN_HIDDEN = 5632
LOG2E = 1.4426950408889634

Z_QA = 0
Z_KA = Z_QA + A_Q_W
Z_VA = Z_KA + A_KV_W
Z_QB = Z_VA + A_KV_W
Z_CKV = Z_QB + B_HEADS * B_QK_PAD
Z_KR = Z_CKV + KV_RANK
Z_W = Z_KR + 128

MOD_ROWS = 8
_NT_DIMS = (((1,), (1,)), ((), ()))


def _cparams(sem, vmem_mib):
    return pltpu.CompilerParams(dimension_semantics=sem, vmem_limit_bytes=vmem_mib << 20)


def _resident(shape):
    n = len(shape)
    return pl.BlockSpec(shape, lambda *_: (0,) * n, pipeline_mode=pl.Buffered(1))


def _rms(x):
    return x * lax.rsqrt(jnp.mean(x * x, axis=-1, keepdims=True) + EPS)


def _modulate(x, g, shift, scale):
    return (_rms(x) * g) * (1.0 + scale) + shift


def _swap_halves(v, n):
    lane = lax.broadcasted_iota(jnp.int32, v.shape, 1)
    fwd = pltpu.roll(v, 128 - n, 1)
    bwd = pltpu.roll(v, n, 1)
    return jnp.where((lane & (2 * n - 1)) < n, fwd, bwd)


def _ada_kernel(cond_ref, w_ref, b_ref, o_ref):
    c = cond_ref[...]
    a = (c * jax.nn.sigmoid(c)).astype(BF16)
    o_ref[...] = jnp.dot(a, w_ref[...].astype(BF16), preferred_element_type=F32) + b_ref[...]


def _ada_mod(cond, ada_w, ada_b):
    tn = 1024
    n6 = 6 * D_MODEL
    out = pl.pallas_call(
        _ada_kernel,
        grid=(DEPTH, n6 // tn),
        in_specs=[
            pl.BlockSpec((MOD_ROWS, D_MODEL), lambda l, n: (0, 0)),
            pl.BlockSpec((None, D_MODEL, tn), lambda l, n: (l, 0, n)),
            pl.BlockSpec((None, 1, tn), lambda l, n: (l, 0, n)),
        ],
        out_specs=pl.BlockSpec((None, MOD_ROWS, tn), lambda l, n: (l, 0, n)),
        out_shape=jax.ShapeDtypeStruct((DEPTH, MOD_ROWS, n6), F32),
        compiler_params=_cparams(("arbitrary", "arbitrary"), 40),
        name="ada_mod",
    )(cond, ada_w, ada_b.reshape(DEPTH, 1, n6))
    return out.reshape(DEPTH, MOD_ROWS, 6, D_MODEL)


def _ffn_kernel(*refs, n_chunks, final):
    if final:
        x_ref, mod_ref, g_ref, wg_ref, wu_ref, wd_ref, fn_ref, o_ref, xb_ref = refs
    else:
        x_ref, mod_ref, g_ref, wg_ref, wu_ref, wd_ref, o_ref, xb_ref = refs
    j = pl.program_id(2)

    @pl.when(j == 0)
    def _():
        xm = _modulate(x_ref[...], g_ref[...], mod_ref[3:4, :], mod_ref[4:5, :])
        xb_ref[...] = xm.astype(BF16)
        o_ref[...] = jnp.zeros_like(o_ref)

    xb = xb_ref[...]
    gate = jnp.dot(xb, wg_ref[...], preferred_element_type=F32)
    up = jnp.dot(xb, wu_ref[...], preferred_element_type=F32)
    h = (gate * jax.nn.sigmoid(gate) * up).astype(BF16)
    o_ref[...] += jnp.dot(h, wd_ref[...], preferred_element_type=F32)

    @pl.when(j == n_chunks - 1)
    def _():
        y = x_ref[...] + mod_ref[5:6, :] * o_ref[...]
        if final:
            y = _rms(y) * fn_ref[...]
        o_ref[...] = y


def _ffn(x, mod, mod_row0, g, wg, wu, wd, final_g=None):
    nb, s, d = x.shape
    tm, tc = 512, 512
    n_chunks = FFN_HIDDEN // tc
    final = final_g is not None
    in_specs = [
        pl.BlockSpec((None, tm, d), lambda b, i, j: (b, i, 0)),
        pl.BlockSpec((None, 6, d), lambda b, i, j: (mod_row0 + b, 0, 0)),
        pl.BlockSpec((1, d), lambda b, i, j: (0, 0)),
        pl.BlockSpec((d, tc), lambda b, i, j: (0, j)),
        pl.BlockSpec((d, tc), lambda b, i, j: (0, j)),
        pl.BlockSpec((tc, d), lambda b, i, j: (j, 0)),
    ]
    args = [x, mod, g, wg, wu, wd]
    if final:
        in_specs.append(pl.BlockSpec((1, d), lambda b, i, j: (0, 0)))
        args.append(final_g)
    return pl.pallas_call(
        functools.partial(_ffn_kernel, n_chunks=n_chunks, final=final),
        grid=(nb, s // tm, n_chunks),
        in_specs=in_specs,
        out_specs=pl.BlockSpec((None, tm, d), lambda b, i, j: (b, i, 0)),
        out_shape=jax.ShapeDtypeStruct((nb, s, d), F32),
        scratch_shapes=[pltpu.VMEM((tm, d), BF16)],
        compiler_params=_cparams(("parallel", "parallel", "arbitrary"), 48),
        name="ffn_final" if final else "ffn",
    )(*args)


def _proj_kernel(*refs, prompt):
    if prompt:
        (x_ref, mod_ref, g_ref, w_ref, wvat_ref, wkv_ref, wkvt_ref, qn_ref, kn_ref, kvn_ref,
         qa_o, ka_o, vat_o, qb_o, kb_o, vbt_o, sk_o, sv_o, sc_o, sr_o) = refs
    else:
        (x_ref, mod_ref, g_ref, w_ref, wvat_ref, wkv_ref, wkvt_ref, qn_ref, kn_ref, kvn_ref,
         c128_ref, s128_ref, c64_ref, s64_ref,
         qa_o, ka_o, vat_o, qb_o, kb_o, vbt_o) = refs
        c128, s128 = c128_ref[...], s128_ref[...]
        c64, s64 = c64_ref[...], s64_ref[...]

    def rope128(v):
        return v if prompt else v * c128 + _swap_halves(v, 32) * s128

    def rope64(v):
        return v if prompt else v * c64 + _swap_halves(v, 16) * s64

    xb = _modulate(x_ref[...], g_ref[...], mod_ref[0:1, :], mod_ref[1:2, :]).astype(BF16)
    z = jnp.dot(xb, w_ref[...], preferred_element_type=F32)
    vat_o[...] = lax.dot_general(wvat_ref[...], xb, _NT_DIMS, preferred_element_type=F32).astype(BF16)

    for h in range(A_HEADS):
        sl = slice(h * HEAD_DIM, (h + 1) * HEAD_DIM)
        q = _rms(z[:, Z_QA + h * HEAD_DIM:Z_QA + (h + 1) * HEAD_DIM]) * qn_ref[...]
        qa_o[:, sl] = rope128(q).astype(BF16)
    for h in range(A_KV_HEADS):
        sl = slice(h * HEAD_DIM, (h + 1) * HEAD_DIM)
        k = _rms(z[:, Z_KA + h * HEAD_DIM:Z_KA + (h + 1) * HEAD_DIM]) * kn_ref[...]
        if prompt:
            sk_o[:, sl] = k
        ka_o[:, sl] = rope128(k).astype(BF16)
    if prompt:
        sv_o[...] = z[:, Z_VA:Z_VA + A_KV_W]
    for h in range(B_HEADS):
        base = Z_QB + h * B_QK_PAD
        qb_o[:, h * B_QK_PAD:h * B_QK_PAD + B_NOPE] = z[:, base:base + B_NOPE].astype(BF16)
        qb_o[:, h * B_QK_PAD + B_NOPE:(h + 1) * B_QK_PAD] = (
            rope64(z[:, base + B_NOPE:base + B_QK_PAD]).astype(BF16))

    ckv = _rms(z[:, Z_CKV:Z_CKV + KV_RANK]) * kvn_ref[...]
    kr = z[:, Z_KR:Z_KR + 128]
    if prompt:
        sc_o[...] = ckv
        sr_o[...] = kr[:, :B_ROPE]
    krb = rope64(kr).astype(BF16)
    ckvb = ckv.astype(BF16)
    kn_all = jnp.dot(ckvb, wkv_ref[...], preferred_element_type=F32)
    for h in range(B_HEADS):
        kb_o[:, h * B_QK_PAD:h * B_QK_PAD + B_NOPE] = kn_all[:, h * B_NOPE:(h + 1) * B_NOPE].astype(BF16)
        kb_o[:, h * B_QK_PAD + B_NOPE:(h + 1) * B_QK_PAD] = krb
    vbt_o[...] = lax.dot_general(wkvt_ref[...], ckvb, _NT_DIMS, preferred_element_type=F32).astype(BF16)


def _attn_project(x, mod, mod_row0, g, w, qn, kn, kvn, rope=None):
    nb, s, d = x.shape
    tm = 256
    prompt = rope is None

    def row(wd):
        return pl.BlockSpec((None, tm, wd), lambda b, i: (b, i, 0))

    def col(wd):
        return pl.BlockSpec((None, wd, tm), lambda b, i: (b, 0, i))

    in_specs = [
        row(d),
        pl.BlockSpec((None, 6, d), lambda b, i: (mod_row0 + b, 0, 0)),
        pl.BlockSpec((1, d), lambda b, i: (0, 0)),
        _resident((d, Z_W)),
        _resident((A_KV_W, d)),
        _resident((KV_RANK, B_HEADS * B_NOPE)),
        _resident((B_HEADS * B_V, KV_RANK)),
        pl.BlockSpec((1, HEAD_DIM), lambda b, i: (0, 0)),
        pl.BlockSpec((1, HEAD_DIM), lambda b, i: (0, 0)),
        pl.BlockSpec((1, KV_RANK), lambda b, i: (0, 0)),
    ]
    args = [x, mod, g, w.main, w.va_t, w.kv_k, w.kv_vt, qn, kn, kvn]
    out_specs = [row(A_Q_W), row(A_KV_W), col(A_KV_W), row(B_HEADS * B_QK_PAD), row(B_HEADS * B_QK_PAD),
                 col(B_HEADS * B_V)]
    out_shape = [jax.ShapeDtypeStruct(shp, BF16) for shp in (
        (nb, s, A_Q_W), (nb, s, A_KV_W), (nb, A_KV_W, s), (nb, s, B_HEADS * B_QK_PAD),
        (nb, s, B_HEADS * B_QK_PAD), (nb, B_HEADS * B_V, s))]
    if prompt:
        for w in (A_KV_W, A_KV_W, KV_RANK, B_ROPE):
            out_specs.append(row(w))
            out_shape.append(jax.ShapeDtypeStruct((nb, s, w), F32))
    else:
        in_specs += [pl.BlockSpec((tm, 128), lambda b, i: (i, 0))] * 4
        args += list(rope)
    return pl.pallas_call(
        functools.partial(_proj_kernel, prompt=prompt),
        grid=(nb, s // tm),
        in_specs=in_specs,
        out_specs=out_specs,
        out_shape=out_shape,
        compiler_params=_cparams(("parallel", "parallel"), 52),
        name="attn_proj_prompt" if prompt else "attn_proj_latent",
    )(*args)


def _ctx_kv_kernel(ckv_ref, kr_ref, wkv_ref, wkvt_ref, kb_o, vbt_o):
    ckvb = ckv_ref[...].astype(BF16)
    kn_all = jnp.dot(ckvb, wkv_ref[...], preferred_element_type=F32)
    krb = kr_ref[...].astype(BF16)
    zero = jnp.zeros((krb.shape[0], B_QK_PAD - B_NOPE - B_ROPE), BF16)
    for h in range(B_HEADS):
        base = h * B_QK_PAD
        kb_o[:, base:base + B_NOPE] = kn_all[:, h * B_NOPE:(h + 1) * B_NOPE].astype(BF16)
        kb_o[:, base + B_NOPE:base + B_NOPE + B_ROPE] = krb
        kb_o[:, base + B_NOPE + B_ROPE:base + B_QK_PAD] = zero
    vbt_o[...] = lax.dot_general(wkvt_ref[...], ckvb, _NT_DIMS, preferred_element_type=F32).astype(BF16)


def _ctx_kv_expand(cache_ckv, cache_kr, layer, w):
    nb, _, p, _ = cache_ckv.shape
    return pl.pallas_call(
        _ctx_kv_kernel,
        grid=(nb,),
        in_specs=[
            pl.BlockSpec((None, None, p, KV_RANK), lambda b: (b, layer, 0, 0)),
            pl.BlockSpec((None, None, p, B_ROPE), lambda b: (b, layer, 0, 0)),
            pl.BlockSpec((KV_RANK, B_HEADS * B_NOPE), lambda b: (0, 0)),
            pl.BlockSpec((B_HEADS * B_V, KV_RANK), lambda b: (0, 0)),
        ],
        out_specs=[
            pl.BlockSpec((None, p, B_HEADS * B_QK_PAD), lambda b: (b, 0, 0)),
            pl.BlockSpec((None, B_HEADS * B_V, p), lambda b: (b, 0, 0)),
        ],
        out_shape=[
            jax.ShapeDtypeStruct((nb, p, B_HEADS * B_QK_PAD), BF16),
            jax.ShapeDtypeStruct((nb, B_HEADS * B_V, p), BF16),
        ],
        compiler_params=_cparams(("parallel",), 32),
        name="ctx_kv_expand",
    )(cache_ckv, cache_kr, w.kv_k, w.kv_vt)


def _attn_kernel(*refs, nseg, v_transposed, hb, group, dq, dv, tq, exp2_scale):
    q_ref = refs[0]
    kv_refs = refs[1:1 + 2 * nseg]
    o_ref = refs[1 + 2 * nseg]
    n_sub = q_ref.shape[0] // tq

    def scores(h, t):
        kvh = h // group
        q = q_ref[t * tq:(t + 1) * tq, h * dq:(h + 1) * dq]
        sts = []
        for i in range(nseg):
            k = kv_refs[2 * i][:, kvh * dq:(kvh + 1) * dq].astype(BF16)
            sts.append(lax.dot_general(k, q, _NT_DIMS, preferred_element_type=F32))
        m = functools.reduce(jnp.maximum, [jnp.max(st, axis=0, keepdims=True) for st in sts])
        return sts, m

    def finish(h, t, sts, m):
        kvh = h // group
        denom = None
        acc = None
        for i in range(nseg):
            p = jnp.exp2((sts[i] - m) * exp2_scale)
            v_ref = kv_refs[2 * i + 1]
            if v_transposed[i]:
                vt = v_ref[kvh * dv:(kvh + 1) * dv, :]
            else:
                vt = v_ref[:, kvh * dv:(kvh + 1) * dv].T
            li = jnp.sum(p, axis=0, keepdims=True)
            oi = jnp.dot(vt.astype(BF16), p.astype(BF16), preferred_element_type=F32)
            denom = li if denom is None else denom + li
            acc = oi if acc is None else acc + oi
        o_ref[t * tq:(t + 1) * tq, h * dv:(h + 1) * dv] = (acc / denom).T.astype(o_ref.dtype)

    pending = None
    for h in range(hb):
        for t in range(n_sub):
            cur = scores(h, t)
            if pending is not None:
                finish(*pending)
            pending = (h, t) + cur
    finish(*pending)


def _attention(q, segs, *, hb, group, dq, scale, tqb, tq, seg_layer=None):
    nb, s, qw = q.shape
    dv = HEAD_DIM
    n_heads = qw // dq
    nkv = hb // group
    in_specs = [pl.BlockSpec((None, tqb, hb * dq), lambda b, h, t: (b, t, h))]
    args = [q]
    v_transposed = []
    for k, v in segs:
        sk = k.shape[-2]
        if k.ndim == 4:
            in_specs.append(pl.BlockSpec((None, None, sk, nkv * dq), lambda b, h, t: (b, seg_layer, 0, h)))
        else:
            in_specs.append(pl.BlockSpec((None, sk, nkv * dq), lambda b, h, t: (b, 0, h)))
        if v.ndim == 4:
            in_specs.append(pl.BlockSpec((None, None, sk, nkv * dv), lambda b, h, t: (b, seg_layer, 0, h)))
            v_transposed.append(False)
        else:
            per = v.shape[-1] // sk
            in_specs.append(pl.BlockSpec((None, nkv * dv, sk),
                                         lambda b, h, t, per=per: (b // per, h, b % per)))
            v_transposed.append(True)
        args += [k, v]
    return pl.pallas_call(
        functools.partial(_attn_kernel, nseg=len(segs), v_transposed=tuple(v_transposed), hb=hb, group=group,
                          dq=dq, dv=dv, tq=tq, exp2_scale=scale * LOG2E),
        grid=(nb, n_heads // hb, s // tqb),
        in_specs=in_specs,
        out_specs=pl.BlockSpec((None, tqb, hb * dv), lambda b, h, t: (b, t, h)),
        out_shape=jax.ShapeDtypeStruct((nb, s, n_heads * dv), BF16),
        compiler_params=_cparams(("parallel", "parallel", "arbitrary"), 56),
        name="attention",
    )(*args)


def _attn_out_kernel(x_ref, mod_ref, oa_ref, ob_ref, w_ref, o_ref):
    y = jnp.dot(oa_ref[...], w_ref[:A_Q_W, :], preferred_element_type=F32)
    y = y + jnp.dot(ob_ref[...], w_ref[A_Q_W:, :], preferred_element_type=F32)
    o_ref[...] = x_ref[...] + mod_ref[2:3, :] * y


def _attn_out(x, mod, mod_row0, oa, ob, w_out):
    nb, s, d = x.shape
    tm = 512
    return pl.pallas_call(
        _attn_out_kernel,
        grid=(nb, s // tm),
        in_specs=[
            pl.BlockSpec((None, tm, d), lambda b, i: (b, i, 0)),
            pl.BlockSpec((None, 6, d), lambda b, i: (mod_row0 + b, 0, 0)),
            pl.BlockSpec((None, tm, A_Q_W), lambda b, i: (b, i, 0)),
            pl.BlockSpec((None, tm, B_HEADS * B_V), lambda b, i: (b, i, 0)),
            _resident((A_Q_W + B_HEADS * B_V, d)),
        ],
        out_specs=pl.BlockSpec((None, tm, d), lambda b, i: (b, i, 0)),
        out_shape=jax.ShapeDtypeStruct((nb, s, d), F32),
        compiler_params=_cparams(("parallel", "parallel"), 40),
        name="attn_out",
    )(x, mod, oa, ob, w_out)


def _gelu_tanh(x):
    return 0.5 * x * (1.0 + jnp.tanh(0.7978845608028654 * (x + 0.044715 * (x * x * x))))


def _cmlp_kernel(x_ref, mod_ref, g_ref, win_ref, vn_ref, ws_ref, bias_ref, wout_ref, o_ref, t_ref):
    x = x_ref[...]
    xm = _modulate(x, g_ref[...], mod_ref[0:1, :], mod_ref[1:2, :])
    z = _gelu_tanh(jnp.dot(xm.astype(BF16), win_ref[...], preferred_element_type=F32))
    u = z[:, :D_MODEL]
    vb = (_rms(z[:, D_MODEL:]) * vn_ref[...]).astype(BF16)
    for n in range(x.shape[0] // CHUNK):
        rows = slice(n * CHUNK, (n + 1) * CHUNK)
        for g in range(C_GROUPS):
            cols = slice(g * C_GROUP_W, (g + 1) * C_GROUP_W)
            sv = jnp.dot(ws_ref[g], vb[rows, cols], preferred_element_type=F32) + bias_ref[:, cols]
            t_ref[rows, cols] = (u[rows, cols] * sv).astype(BF16)
    y = jnp.dot(t_ref[...], wout_ref[...], preferred_element_type=F32)
    o_ref[...] = x + mod_ref[2:3, :] * y


def _chunk_mlp(x, mod, mod_row0, g, w_in, v_norm, w_s, bias, w_out):
    nb, s, d = x.shape
    tm = 256
    return pl.pallas_call(
        _cmlp_kernel,
        grid=(nb, s // tm),
        in_specs=[
            pl.BlockSpec((None, tm, d), lambda b, i: (b, i, 0)),
            pl.BlockSpec((None, 6, d), lambda b, i: (mod_row0 + b, 0, 0)),
            pl.BlockSpec((1, d), lambda b, i: (0, 0)),
            _resident((d, 2 * d)),
            pl.BlockSpec((1, d), lambda b, i: (0, 0)),
            _resident((C_GROUPS, CHUNK, CHUNK)),
            _resident((CHUNK, d)),
            _resident((d, d)),
        ],
        out_specs=pl.BlockSpec((None, tm, d), lambda b, i: (b, i, 0)),
        out_shape=jax.ShapeDtypeStruct((nb, s, d), F32),
        scratch_shapes=[pltpu.VMEM((tm, d), BF16)],
        compiler_params=_cparams(("parallel", "parallel"), 52),
        name="chunk_mlp",
    )(x, mod, g, w_in, v_norm, w_s, bias, w_out)


def _rope_tables(n_tokens):
    t = np.arange(n_tokens)
    row = (t // GRID_W).astype(np.float32)
    col = (t % GRID_W).astype(np.float32)

    def axis_tables(pos, half):
        inv = (1.0 / (ROPE_BASE ** (np.arange(0, half, 2, dtype=np.float32) / half))).astype(np.float32)
        ang = (pos[:, None] * inv).astype(np.float32).astype(np.float64)
        c, s = np.cos(ang), np.sin(ang)
        return np.concatenate([c, c], axis=1), np.concatenate([-s, s], axis=1)

    def tables(dim):
        cr, sr = axis_tables(row, dim // 2)
        cc, sc = axis_tables(col, dim // 2)
        return np.concatenate([cr, cc], axis=1), np.concatenate([sr, sc], axis=1)

    c128, s128 = tables(HEAD_DIM)
    c64, s64 = tables(B_ROPE)
    pad = np.zeros((n_tokens, 128 - B_ROPE))
    c64 = np.concatenate([c64, pad], axis=1)
    s64 = np.concatenate([s64, pad], axis=1)
    return tuple(jnp.asarray(a, F32) for a in (c128, s128, c64, s64))


class _AttnWeights(NamedTuple):
    main: jax.Array
    va_t: jax.Array
    kv_k: jax.Array
    kv_vt: jax.Array
    out: jax.Array


def _layout_attn_weights(w_in, w_kv_up, w_out):
    d = w_in.shape[0]
    o = A_Q_W + 2 * A_KV_W
    qb = w_in[:, o:o + B_Q_W].reshape(d, B_HEADS, B_NOPE + B_ROPE)
    qb = jnp.pad(qb, ((0, 0), (0, 0), (0, B_QK_PAD - B_NOPE - B_ROPE))).reshape(d, B_HEADS * B_QK_PAD)
    ckv = w_in[:, o + B_Q_W:o + B_Q_W + KV_RANK]
    kr = jnp.pad(w_in[:, o + B_Q_W + KV_RANK:], ((0, 0), (0, 128 - B_ROPE)))
    main = jnp.concatenate([w_in[:, :o], qb, ckv, kr], axis=1)
    kv3 = w_kv_up.reshape(KV_RANK, B_HEADS, B_NOPE + B_V)
    return _AttnWeights(
        main=main.astype(BF16),
        va_t=w_in[:, Z_VA:Z_VA + A_KV_W].T.astype(BF16),
        kv_k=kv3[:, :, :B_NOPE].reshape(KV_RANK, -1).astype(BF16),
        kv_vt=kv3[:, :, B_NOPE:].reshape(KV_RANK, -1).T.astype(BF16),
        out=w_out.astype(BF16))


def kernel(x_prompt, x_sample, c, cache_gqa_k, cache_gqa_v, cache_mla_ckv, cache_mla_krope, c_ctx, ada_w, ada_b, norm_mix, norm_ffn, ffn_gate, ffn_up, ffn_down, attn_w_in, attn_q_norm, attn_k_norm, attn_kv_norm, attn_w_kv_up, attn_w_out, cmlp_w_in, cmlp_v_norm, cmlp_w_s, cmlp_b_s, cmlp_w_out, final_norm):
    n_p, s_p, d = x_prompt.shape
    n_s, s_s, _ = x_sample.shape
    past = cache_gqa_k.shape[2]

    cond = jnp.concatenate([c_ctx[None, :], c, jnp.zeros((MOD_ROWS - 1 - n_s, d), F32)], axis=0)
    mods = _ada_mod(cond, ada_w, ada_b)

    xp = x_prompt.reshape(1, n_p * s_p, d)
    xs = x_sample
    rope = _rope_tables(s_s)
    cache_k = cache_gqa_k.reshape(n_s, -1, past, A_KV_W)
    cache_v = cache_gqa_v.reshape(n_s, -1, past, A_KV_W)
    final_g = final_norm.reshape(1, d)

    states = ([], [], [], [])
    for i in range(DEPTH):
        j = i // 2
        mod = mods[i]
        g_mix = norm_mix[i].reshape(1, d)
        if i % 2 == 0:
            w = _layout_attn_weights(attn_w_in[j], attn_w_kv_up[j], attn_w_out[j])
            qn = attn_q_norm[j].reshape(1, HEAD_DIM)
            kn = attn_k_norm[j].reshape(1, HEAD_DIM)
            kvn = attn_kv_norm[j].reshape(1, KV_RANK)
            a_group = A_HEADS // A_KV_HEADS

            qa, ka, vat, qb, kb, vbt, st_k, st_v, st_c, st_r = _attn_project(xp, mod, 0, g_mix, w, qn, kn, kvn)
            for lst, st in zip(states, (st_k, st_v, st_c, st_r)):
                lst.append(st)

            def per_seq(a):
                return a.reshape(n_p, s_p, a.shape[-1])

            oa = _attention(per_seq(qa), [(per_seq(ka), vat)], hb=A_HEADS, group=a_group, dq=HEAD_DIM,
                            scale=A_SCALE, tqb=s_p, tq=s_p)
            ob = _attention(per_seq(qb), [(per_seq(kb), vbt)], hb=B_HEADS, group=1, dq=B_QK_PAD,
                            scale=B_SCALE, tqb=s_p, tq=s_p)
            xp = _attn_out(xp, mod, 0, oa.reshape(1, n_p * s_p, -1), ob.reshape(1, n_p * s_p, -1), w.out)

            qa, ka, vat, qb, kb, vbt = _attn_project(xs, mod, 1, g_mix, w, qn, kn, kvn, rope=rope)
            kb_ctx, vbt_ctx = _ctx_kv_expand(cache_mla_ckv, cache_mla_krope, j, w)
            oa = _attention(qa, [(ka, vat), (cache_k, cache_v)], hb=a_group, group=a_group, dq=HEAD_DIM,
                            scale=A_SCALE, tqb=512, tq=512, seg_layer=j)
            ob = _attention(qb, [(kb, vbt), (kb_ctx, vbt_ctx)], hb=1, group=1, dq=B_QK_PAD,
                            scale=B_SCALE, tqb=2048, tq=512)
            xs = _attn_out(xs, mod, 1, oa, ob, w.out)
        else:
            w_in = cmlp_w_in[j].astype(BF16)
            w_out = cmlp_w_out[j].astype(BF16)
            w_s = cmlp_w_s[j].astype(BF16)
            vn = cmlp_v_norm[j].reshape(1, d)
            bias = jnp.repeat(cmlp_b_s[j].T, C_GROUP_W, axis=1)
            xp = _chunk_mlp(xp, mod, 0, g_mix, w_in, vn, w_s, bias, w_out)
            xs = _chunk_mlp(xs, mod, 1, g_mix, w_in, vn, w_s, bias, w_out)

        g_ffn = norm_ffn[i].reshape(1, d)
        wg = ffn_gate[i].astype(BF16)
        wu = ffn_up[i].astype(BF16)
        wd = ffn_down[i].astype(BF16)
        fin = final_g if i == DEPTH - 1 else None
        xp = _ffn(xp, mod, 0, g_ffn, wg, wu, wd, fin)
        xs = _ffn(xs, mod, 1, g_ffn, wg, wu, wd, fin)

    y_prompt = xp.reshape(n_p, s_p, d)
    st_k, st_v, st_c, st_r = (jnp.stack([a.reshape(n_p, s_p, a.shape[-1]) for a in lst], axis=1)
                              for lst in states)
    return (y_prompt, xs,
            st_k.reshape(n_p, -1, s_p, A_KV_HEADS, HEAD_DIM),
            st_v.reshape(n_p, -1, s_p, A_KV_HEADS, HEAD_DIM),
            st_c, st_r)
```

```python
import functools
from typing import NamedTuple

import numpy as np
import jax
import jax.numpy as jnp
from jax import lax
from jax.experimental import pallas as pl
from jax.experimental.pallas import tpu as pltpu

F32 = jnp.float32
BF16 = jnp.bfloat16

D_MODEL = 2048
DEPTH = 4
GRID_W = 64
ROPE_BASE = 10000.0
EPS = 1e-6
A_HEADS = 8
A_KV_HEADS = 2
A_GROUP = A_HEADS // A_KV_HEADS
HEAD_DIM = 128
A_SCALE = HEAD_DIM ** -0.5
B_HEADS = 8
B_NOPE = 128
B_ROPE = 64
B_V = 128
KV_RANK = 512
B_SCALE = (B_NOPE + B_ROPE) ** -0.5
B_QK_PAD = 256
A_Q_W = A_HEADS * HEAD_DIM
A_KV_W = A_KV_HEADS * HEAD_DIM
B_Q_W = B_HEADS * (B_NOPE + B_ROPE)
CHUNK = 128
C_GROUPS = 8
C_GROUP_W = D_MODEL // C_GROUPS
FFN_HIDDEN = 5632
LOG2E = 1.4426950408889634
LANES = 128

Z_QA = 0
Z_KA = Z_QA + A_Q_W
Z_VA = Z_KA + A_KV_W
Z_QB = Z_VA + A_KV_W
Z_CKV = Z_QB + B_HEADS * B_QK_PAD
Z_KR = Z_CKV + KV_RANK
Z_W = Z_KR + LANES

MOD_ROWS = 8
_NT_DIMS = (((1,), (1,)), ((), ()))

TM_FFN, TC_FFN = 512, 512
TM_PROJ = 256
TM_ATTN_OUT = 512
TM_CMLP = 256
TN_ADA = 1024
TQ_ATTN = 512


def _cparams(sem, vmem_mib):
    return pltpu.CompilerParams(dimension_semantics=sem, vmem_limit_bytes=vmem_mib << 20)


def _layer_resident(shape, layer):
    n = len(shape)
    return pl.BlockSpec((None,) + tuple(shape), lambda *_: (layer,) + (0,) * n, pipeline_mode=pl.Buffered(1))


def _layer_row(width, layer):
    return pl.BlockSpec((None, 1, width), lambda *_: (layer, 0, 0))


def _mod_spec(layer, row0):
    return pl.BlockSpec((None, None, 6, D_MODEL), lambda b, *_: (layer, row0 + b, 0, 0))


def _rms(x):
    return x * lax.rsqrt(jnp.mean(x * x, axis=-1, keepdims=True) + EPS)


def _modulate(x, g, shift, scale):
    return (_rms(x) * g) * (1.0 + scale) + shift


def _swap_halves(v, n):
    lane = lax.broadcasted_iota(jnp.int32, v.shape, 1)
    fwd = pltpu.roll(v, LANES - n, 1)
    bwd = pltpu.roll(v, n, 1)
    return jnp.where((lane & (2 * n - 1)) < n, fwd, bwd)


def _ada_kernel(cond_ref, w_ref, b_ref, o_ref):
    c = cond_ref[...]
    a = (c * jax.nn.sigmoid(c)).astype(BF16)
    o_ref[...] = jnp.dot(a, w_ref[...].astype(BF16), preferred_element_type=F32) + b_ref[...]


def _ada_mod(cond, ada_w, ada_b):
    n6 = 6 * D_MODEL
    out = pl.pallas_call(
        _ada_kernel,
        grid=(DEPTH, n6 // TN_ADA),
        in_specs=[
            pl.BlockSpec((MOD_ROWS, D_MODEL), lambda l, n: (0, 0)),
            pl.BlockSpec((None, D_MODEL, TN_ADA), lambda l, n: (l, 0, n)),
            pl.BlockSpec((None, 1, TN_ADA), lambda l, n: (l, 0, n)),
        ],
        out_specs=pl.BlockSpec((None, MOD_ROWS, TN_ADA), lambda l, n: (l, 0, n)),
        out_shape=jax.ShapeDtypeStruct((DEPTH, MOD_ROWS, n6), F32),
        compiler_params=_cparams(("arbitrary", "arbitrary"), 40),
        name="ada_mod",
    )(cond, ada_w, ada_b.reshape(DEPTH, 1, n6))
    return out.reshape(DEPTH, MOD_ROWS, 6, D_MODEL)


def _ffn_kernel(*refs, n_chunks, final):
    if final:
        x_ref, mod_ref, g_ref, wg_ref, wu_ref, wd_ref, fn_ref, o_ref, xb_ref = refs
    else:
        x_ref, mod_ref, g_ref, wg_ref, wu_ref, wd_ref, o_ref, xb_ref = refs
    j = pl.program_id(2)

    @pl.when(j == 0)
    def _():
        xm = _modulate(x_ref[...], g_ref[...], mod_ref[3:4, :], mod_ref[4:5, :])
        xb_ref[...] = xm.astype(BF16)
        o_ref[...] = jnp.zeros_like(o_ref)

    xb = xb_ref[...]
    gate = jnp.dot(xb, wg_ref[...], preferred_element_type=F32)
    up = jnp.dot(xb, wu_ref[...], preferred_element_type=F32)
    h = (gate * jax.nn.sigmoid(gate) * up).astype(BF16)
    o_ref[...] += jnp.dot(h, wd_ref[...], preferred_element_type=F32)

    @pl.when(j == n_chunks - 1)
    def _():
        y = x_ref[...] + mod_ref[5:6, :] * o_ref[...]
        if final:
            y = _rms(y) * fn_ref[...]
        o_ref[...] = y


def _ffn(x, mods, row0, layer, g, wg, wu, wd, final_g=None):
    nb, s, d = x.shape
    tm, tc = TM_FFN, TC_FFN
    n_chunks = FFN_HIDDEN // tc
    final = final_g is not None
    in_specs = [
        pl.BlockSpec((None, tm, d), lambda b, i, j: (b, i, 0)),
        _mod_spec(layer, row0),
        _layer_row(d, layer),
        pl.BlockSpec((None, d, tc), lambda b, i, j: (layer, 0, j)),
        pl.BlockSpec((None, d, tc), lambda b, i, j: (layer, 0, j)),
        pl.BlockSpec((None, tc, d), lambda b, i, j: (layer, j, 0)),
    ]
    args = [x, mods, g, wg, wu, wd]
    if final:
        in_specs.append(pl.BlockSpec((1, d), lambda b, i, j: (0, 0)))
        args.append(final_g)
    return pl.pallas_call(
        functools.partial(_ffn_kernel, n_chunks=n_chunks, final=final),
        grid=(nb, s // tm, n_chunks),
        in_specs=in_specs,
        out_specs=pl.BlockSpec((None, tm, d), lambda b, i, j: (b, i, 0)),
        out_shape=jax.ShapeDtypeStruct((nb, s, d), F32),
        scratch_shapes=[pltpu.VMEM((tm, d), BF16)],
        compiler_params=_cparams(("parallel", "parallel", "arbitrary"), 48),
        name="ffn_final" if final else "ffn",
    )(*args)


class _AttnWeights(NamedTuple):
    main: jax.Array
    va_t: jax.Array
    kv_k: jax.Array
    kv_vt: jax.Array
    out: jax.Array


def _layout_attn_weights(w_in, w_kv_up, w_out):
    n_l, d, _ = w_in.shape
    o = A_Q_W + 2 * A_KV_W
    qb = w_in[:, :, o:o + B_Q_W].reshape(n_l, d, B_HEADS, B_NOPE + B_ROPE)
    qb = jnp.pad(qb, ((0, 0), (0, 0), (0, 0), (0, B_QK_PAD - B_NOPE - B_ROPE)))
    ckv = w_in[:, :, o + B_Q_W:o + B_Q_W + KV_RANK]
    kr = jnp.pad(w_in[:, :, o + B_Q_W + KV_RANK:], ((0, 0), (0, 0), (0, LANES - B_ROPE)))
    main = jnp.concatenate([w_in[:, :, :o], qb.reshape(n_l, d, B_HEADS * B_QK_PAD), ckv, kr], axis=2)
    kv4 = w_kv_up.reshape(n_l, KV_RANK, B_HEADS, B_NOPE + B_V)
    return _AttnWeights(
        main=main.astype(BF16),
        va_t=jnp.swapaxes(w_in[:, :, Z_VA:Z_VA + A_KV_W], 1, 2).astype(BF16),
        kv_k=kv4[..., :B_NOPE].reshape(n_l, KV_RANK, -1).astype(BF16),
        kv_vt=jnp.swapaxes(kv4[..., B_NOPE:].reshape(n_l, KV_RANK, -1), 1, 2).astype(BF16),
        out=w_out.astype(BF16))


def _proj_kernel(*refs, prompt):
    if prompt:
        (x_ref, mod_ref, g_ref, w_ref, wvat_ref, wkv_ref, wkvt_ref, qn_ref, kn_ref, kvn_ref,
         qa_o, ka_o, vat_o, qb_o, kb_o, vbt_o, sk_o, sv_o, sc_o, sr_o) = refs
    else:
        (x_ref, mod_ref, g_ref, w_ref, wvat_ref, wkv_ref, wkvt_ref, qn_ref, kn_ref, kvn_ref,
         c128_ref, s128_ref, c64_ref, s64_ref,
         qa_o, ka_o, vat_o, qb_o, kb_o, vbt_o) = refs
        c128, s128 = c128_ref[...], s128_ref[...]
        c64, s64 = c64_ref[...], s64_ref[...]

    def rope128(v):
        return v if prompt else v * c128 + _swap_halves(v, 32) * s128

    def rope64(v):
        return v if prompt else v * c64 + _swap_halves(v, 16) * s64

    xb = _modulate(x_ref[...], g_ref[...], mod_ref[0:1, :], mod_ref[1:2, :]).astype(BF16)
    z = jnp.dot(xb, w_ref[...], preferred_element_type=F32)
    vat_o[...] = lax.dot_general(wvat_ref[...], xb, _NT_DIMS, preferred_element_type=F32).astype(BF16)

    for h in range(A_HEADS):
        q = _rms(z[:, Z_QA + h * HEAD_DIM:Z_QA + (h + 1) * HEAD_DIM]) * qn_ref[...]
        qa_o[h] = (rope128(q) * (A_SCALE * LOG2E)).astype(BF16)
    for h in range(A_KV_HEADS):
        sl = slice(h * HEAD_DIM, (h + 1) * HEAD_DIM)
        k = _rms(z[:, Z_KA + h * HEAD_DIM:Z_KA + (h + 1) * HEAD_DIM]) * kn_ref[...]
        if prompt:
            sk_o[:, sl] = k
        ka_o[:, sl] = rope128(k).astype(BF16)
    if prompt:
        sv_o[...] = z[:, Z_VA:Z_VA + A_KV_W]
    for h in range(B_HEADS):
        base = Z_QB + h * B_QK_PAD
        qb_o[h, :, :B_NOPE] = (z[:, base:base + B_NOPE] * (B_SCALE * LOG2E)).astype(BF16)
        qb_o[h, :, B_NOPE:] = (rope64(z[:, base + B_NOPE:base + B_QK_PAD]) * (B_SCALE * LOG2E)).astype(BF16)

    ckv = _rms(z[:, Z_CKV:Z_CKV + KV_RANK]) * kvn_ref[...]
    kr = z[:, Z_KR:Z_KR + LANES]
    if prompt:
        sc_o[...] = ckv
        sr_o[...] = kr[:, :B_ROPE]
    krb = rope64(kr).astype(BF16)
    ckvb = ckv.astype(BF16)
    kn_all = jnp.dot(ckvb, wkv_ref[...], preferred_element_type=F32)
    for h in range(B_HEADS):
        kb_o[:, h * B_QK_PAD:h * B_QK_PAD + B_NOPE] = kn_all[:, h * B_NOPE:(h + 1) * B_NOPE].astype(BF16)
        kb_o[:, h * B_QK_PAD + B_NOPE:(h + 1) * B_QK_PAD] = krb
    vbt_o[...] = lax.dot_general(wkvt_ref[...], ckvb, _NT_DIMS, preferred_element_type=F32).astype(BF16)


def _attn_project(x, mods, row0, layer, attn_layer, g, w, qn, kn, kvn, rope=None):
    nb, s, d = x.shape
    tm = TM_PROJ
    prompt = rope is None

    def row(wd):
        return pl.BlockSpec((None, tm, wd), lambda b, i: (b, i, 0))

    def col(wd):
        return pl.BlockSpec((None, wd, tm), lambda b, i: (b, 0, i))

    def heads(n, wd):
        return pl.BlockSpec((None, n, tm, wd), lambda b, i: (b, 0, i, 0))

    in_specs = [
        row(d),
        _mod_spec(layer, row0),
        _layer_row(d, layer),
        _layer_resident((d, Z_W), attn_layer),
        _layer_resident((A_KV_W, d), attn_layer),
        _layer_resident((KV_RANK, B_HEADS * B_NOPE), attn_layer),
        _layer_resident((B_HEADS * B_V, KV_RANK), attn_layer),
        _layer_row(HEAD_DIM, attn_layer),
        _layer_row(HEAD_DIM, attn_layer),
        _layer_row(KV_RANK, attn_layer),
    ]
    args = [x, mods, g, w.main, w.va_t, w.kv_k, w.kv_vt, qn, kn, kvn]
    out_specs = [heads(A_HEADS, HEAD_DIM), row(A_KV_W), col(A_KV_W), heads(B_HEADS, B_QK_PAD),
                 row(B_HEADS * B_QK_PAD), col(B_HEADS * B_V)]
    out_shape = [jax.ShapeDtypeStruct(shp, BF16) for shp in (
        (nb, A_HEADS, s, HEAD_DIM), (nb, s, A_KV_W), (nb, A_KV_W, s), (nb, B_HEADS, s, B_QK_PAD),
        (nb, s, B_HEADS * B_QK_PAD), (nb, B_HEADS * B_V, s))]
    if prompt:
        for wd in (A_KV_W, A_KV_W, KV_RANK, B_ROPE):
            out_specs.append(row(wd))
            out_shape.append(jax.ShapeDtypeStruct((nb, s, wd), F32))
    else:
        in_specs += [pl.BlockSpec((tm, LANES), lambda b, i: (i, 0))] * 4
        args += list(rope)
    return pl.pallas_call(
        functools.partial(_proj_kernel, prompt=prompt),
        grid=(nb, s // tm),
        in_specs=in_specs,
        out_specs=out_specs,
        out_shape=out_shape,
        compiler_params=_cparams(("parallel", "parallel"), 52),
        name="attn_proj_prompt" if prompt else "attn_proj_latent",
    )(*args)


def _ctx_kv_kernel(ckv_ref, kr_ref, wkv_ref, wkvt_ref, kb_o, vbt_o):
    ckvb = ckv_ref[...].astype(BF16)
    kn_all = jnp.dot(ckvb, wkv_ref[...], preferred_element_type=F32)
    krb = kr_ref[...].astype(BF16)
    zero = jnp.zeros((krb.shape[0], B_QK_PAD - B_NOPE - B_ROPE), BF16)
    for h in range(B_HEADS):
        base = h * B_QK_PAD
        kb_o[:, base:base + B_NOPE] = kn_all[:, h * B_NOPE:(h + 1) * B_NOPE].astype(BF16)
        kb_o[:, base + B_NOPE:base + B_NOPE + B_ROPE] = krb
        kb_o[:, base + B_NOPE + B_ROPE:base + B_QK_PAD] = zero
    vbt_o[...] = lax.dot_general(wkvt_ref[...], ckvb, _NT_DIMS, preferred_element_type=F32).astype(BF16)


def _ctx_kv_expand(cache_ckv, cache_kr, attn_layer, w):
    nb, _, p, _ = cache_ckv.shape
    return pl.pallas_call(
        _ctx_kv_kernel,
        grid=(nb,),
        in_specs=[
            pl.BlockSpec((None, None, p, KV_RANK), lambda b: (b, attn_layer, 0, 0)),
            pl.BlockSpec((None, None, p, B_ROPE), lambda b: (b, attn_layer, 0, 0)),
            pl.BlockSpec((None, KV_RANK, B_HEADS * B_NOPE), lambda b: (attn_layer, 0, 0)),
            pl.BlockSpec((None, B_HEADS * B_V, KV_RANK), lambda b: (attn_layer, 0, 0)),
        ],
        out_specs=[
            pl.BlockSpec((None, p, B_HEADS * B_QK_PAD), lambda b: (b, 0, 0)),
            pl.BlockSpec((None, B_HEADS * B_V, p), lambda b: (b, 0, 0)),
        ],
        out_shape=[
            jax.ShapeDtypeStruct((nb, p, B_HEADS * B_QK_PAD), BF16),
            jax.ShapeDtypeStruct((nb, B_HEADS * B_V, p), BF16),
        ],
        compiler_params=_cparams(("parallel",), 32),
        name="ctx_kv_expand",
    )(cache_ckv, cache_kr, w.kv_k, w.kv_vt)


def _attn_kernel(*refs, seg_len, v_transposed, hb, group, dq, dv, tq, rolled):
    nseg = len(seg_len)
    q_ref = refs[0]
    kv_refs = refs[1:1 + 2 * nseg]
    o_ref, s_ref, m_ref = refs[1 + 2 * nseg:]
    n_sub = q_ref.shape[1] // tq
    n_items = hb * n_sub
    seg_off = [sum(seg_len[:i]) for i in range(nseg)]

    def split(i):
        if isinstance(i, int):
            return i // n_sub, (i % n_sub) * tq
        shift = n_sub.bit_length() - 1
        return lax.shift_right_logical(i, shift), pl.multiple_of((i & (n_sub - 1)) * tq, tq)

    def kv_head(h):
        return h // group if isinstance(h, int) else 0

    def score(i, slot):
        h, r = split(i)
        kvh = kv_head(h)
        q = q_ref[h, pl.ds(r, tq), :]
        m = None
        for s in range(nseg):
            k = kv_refs[2 * s][:, kvh * dq:(kvh + 1) * dq].astype(BF16)
            st = lax.dot_general(k, q, _NT_DIMS, preferred_element_type=F32)
            s_ref[slot, seg_off[s]:seg_off[s] + seg_len[s], :] = st
            ms = jnp.max(st, axis=0, keepdims=True)
            m = ms if m is None else jnp.maximum(m, ms)
        m_ref[slot] = m

    def finish(i, slot):
        h, r = split(i)
        kvh = kv_head(h)
        m = m_ref[slot]
        denom = None
        acc = None
        for s in range(nseg):
            p = jnp.exp2(s_ref[slot, seg_off[s]:seg_off[s] + seg_len[s], :] - m)
            v_ref = kv_refs[2 * s + 1]
            if v_transposed[s]:
                vt = v_ref[kvh * dv:(kvh + 1) * dv, :]
            else:
                vt = v_ref[:, kvh * dv:(kvh + 1) * dv].T
            ls = jnp.sum(p, axis=0, keepdims=True)
            os_ = jnp.dot(vt.astype(BF16), p.astype(BF16), preferred_element_type=F32)
            denom = ls if denom is None else denom + ls
            acc = os_ if acc is None else acc + os_
        o_ref[h, pl.ds(r, tq), :] = (acc / denom).T.astype(o_ref.dtype)

    if rolled:
        assert hb == group and n_items % 2 == 0 and n_items >= 4 and n_sub & (n_sub - 1) == 0
        score(0, 0)

        def body(k, carry):
            i0 = 2 * k
            score(i0 + 1, 1)
            finish(i0, 0)
            score(i0 + 2, 0)
            finish(i0 + 1, 1)
            return carry

        lax.fori_loop(0, n_items // 2 - 1, body, 0)
        score(n_items - 1, 1)
        finish(n_items - 2, 0)
        finish(n_items - 1, 1)
    else:
        score(0, 0)
        for i in range(n_items):
            if i + 1 < n_items:
                score(i + 1, (i + 1) % 2)
            finish(i, i % 2)


def _attention(q, segs, *, seq_len, hb, group, dq, tqb, tq, rolled, seg_layer=None):
    nbq, n_heads, s_tot, _ = q.shape
    dv = HEAD_DIM
    per = s_tot // seq_len
    n_qb = seq_len // tqb
    nkv = hb // group
    in_specs = [pl.BlockSpec((None, hb, tqb, dq), lambda g, h, t: (g // per, h, (g % per) * n_qb + t, 0))]
    args = [q]
    v_transposed = []
    seg_lens = []
    for k, v in segs:
        if k.ndim == 4:
            sk = k.shape[2]
            in_specs.append(pl.BlockSpec((None, None, sk, nkv * dq), lambda g, h, t: (g, seg_layer, 0, h)))
            in_specs.append(pl.BlockSpec((None, None, sk, nkv * dv), lambda g, h, t: (g, seg_layer, 0, h)))
            v_transposed.append(False)
        else:
            sk = k.shape[1] // per
            in_specs.append(pl.BlockSpec((None, sk, nkv * dq), lambda g, h, t: (g // per, g % per, h)))
            in_specs.append(pl.BlockSpec((None, nkv * dv, sk), lambda g, h, t: (g // per, h, g % per)))
            v_transposed.append(True)
        seg_lens.append(sk)
        args += [k, v]
    return pl.pallas_call(
        functools.partial(_attn_kernel, seg_len=tuple(seg_lens), v_transposed=tuple(v_transposed), hb=hb,
                          group=group, dq=dq, dv=dv, tq=tq, rolled=rolled),
        grid=(nbq * per, n_heads // hb, n_qb),
        in_specs=in_specs,
        out_specs=pl.BlockSpec((None, hb, tqb, dv), lambda g, h, t: (g // per, h, (g % per) * n_qb + t, 0)),
        out_shape=jax.ShapeDtypeStruct((nbq, n_heads, s_tot, dv), BF16),
        scratch_shapes=[pltpu.VMEM((2, sum(seg_lens), tq), F32), pltpu.VMEM((2, 1, tq), F32)],
        compiler_params=_cparams(("parallel", "parallel", "arbitrary"), 56),
        name="attention",
    )(*args)


def _attn_out_kernel(x_ref, mod_ref, oa_ref, ob_ref, w_ref, o_ref):
    cat = jnp.concatenate([oa_ref[h] for h in range(A_HEADS)] + [ob_ref[h] for h in range(B_HEADS)], axis=-1)
    y = jnp.dot(cat, w_ref[...], preferred_element_type=F32)
    o_ref[...] = x_ref[...] + mod_ref[2:3, :] * y


def _attn_out(x, mods, row0, layer, attn_layer, oa, ob, w_out):
    nb, s, d = x.shape
    tm = TM_ATTN_OUT
    return pl.pallas_call(
        _attn_out_kernel,
        grid=(nb, s // tm),
        in_specs=[
            pl.BlockSpec((None, tm, d), lambda b, i: (b, i, 0)),
            _mod_spec(layer, row0),
            pl.BlockSpec((None, A_HEADS, tm, HEAD_DIM), lambda b, i: (b, 0, i, 0)),
            pl.BlockSpec((None, B_HEADS, tm, B_V), lambda b, i: (b, 0, i, 0)),
            _layer_resident((A_Q_W + B_HEADS * B_V, d), attn_layer),
        ],
        out_specs=pl.BlockSpec((None, tm, d), lambda b, i: (b, i, 0)),
        out_shape=jax.ShapeDtypeStruct((nb, s, d), F32),
        compiler_params=_cparams(("parallel", "parallel"), 40),
        name="attn_out",
    )(x, mods, oa, ob, w_out)


def _gelu_tanh(x):
    return 0.5 * x * (1.0 + jnp.tanh(0.7978845608028654 * (x + 0.044715 * (x * x * x))))


def _cmlp_kernel(x_ref, mod_ref, g_ref, win_ref, vn_ref, ws_ref, bias_ref, wout_ref, o_ref, t_ref):
    x = x_ref[...]
    xm = _modulate(x, g_ref[...], mod_ref[0:1, :], mod_ref[1:2, :])
    z = _gelu_tanh(jnp.dot(xm.astype(BF16), win_ref[...], preferred_element_type=F32))
    u = z[:, :D_MODEL]
    vb = (_rms(z[:, D_MODEL:]) * vn_ref[...]).astype(BF16)
    for n in range(x.shape[0] // CHUNK):
        rows = slice(n * CHUNK, (n + 1) * CHUNK)
        for g in range(C_GROUPS):
            cols = slice(g * C_GROUP_W, (g + 1) * C_GROUP_W)
            sv = jnp.dot(ws_ref[g], vb[rows, cols], preferred_element_type=F32) + bias_ref[:, cols]
            t_ref[rows, cols] = (u[rows, cols] * sv).astype(BF16)
    y = jnp.dot(t_ref[...], wout_ref[...], preferred_element_type=F32)
    o_ref[...] = x + mod_ref[2:3, :] * y


def _chunk_mlp(x, mods, row0, layer, c_layer, g, w_in, v_norm, w_s, bias, w_out):
    nb, s, d = x.shape
    tm = TM_CMLP
    return pl.pallas_call(
        _cmlp_kernel,
        grid=(nb, s // tm),
        in_specs=[
            pl.BlockSpec((None, tm, d), lambda b, i: (b, i, 0)),
            _mod_spec(layer, row0),
            _layer_row(d, layer),
            _layer_resident((d, 2 * d), c_layer),
            _layer_row(d, c_layer),
            _layer_resident((C_GROUPS, CHUNK, CHUNK), c_layer),
            _layer_resident((CHUNK, d), c_layer),
            _layer_resident((d, d), c_layer),
        ],
        out_specs=pl.BlockSpec((None, tm, d), lambda b, i: (b, i, 0)),
        out_shape=jax.ShapeDtypeStruct((nb, s, d), F32),
        scratch_shapes=[pltpu.VMEM((tm, d), BF16)],
        compiler_params=_cparams(("parallel", "parallel"), 52),
        name="chunk_mlp",
    )(x, mods, g, w_in, v_norm, w_s, bias, w_out)


def _rope_tables(n_tokens):
    t = np.arange(n_tokens)
    row = (t // GRID_W).astype(np.float32)
    col = (t % GRID_W).astype(np.float32)

    def axis_tables(pos, half):
        inv = (1.0 / (ROPE_BASE ** (np.arange(0, half, 2, dtype=np.float32) / half))).astype(np.float32)
        ang = (pos[:, None] * inv).astype(np.float32).astype(np.float64)
        c, s = np.cos(ang), np.sin(ang)
        return np.concatenate([c, c], axis=1), np.concatenate([-s, s], axis=1)

    def tables(dim):
        cr, sr = axis_tables(row, dim // 2)
        cc, sc = axis_tables(col, dim // 2)
        return np.concatenate([cr, cc], axis=1), np.concatenate([sr, sc], axis=1)

    c128, s128 = tables(HEAD_DIM)
    c64, s64 = tables(B_ROPE)
    pad = np.zeros((n_tokens, LANES - B_ROPE))
    c64 = np.concatenate([c64, pad], axis=1)
    s64 = np.concatenate([s64, pad], axis=1)
    return tuple(jnp.asarray(a, F32) for a in (c128, s128, c64, s64))


def kernel(x_prompt, x_sample, c, cache_gqa_k, cache_gqa_v, cache_mla_ckv, cache_mla_krope, c_ctx, ada_w, ada_b, norm_mix, norm_ffn, ffn_gate, ffn_up, ffn_down, attn_w_in, attn_q_norm, attn_k_norm, attn_kv_norm, attn_w_kv_up, attn_w_out, cmlp_w_in, cmlp_v_norm, cmlp_w_s, cmlp_b_s, cmlp_w_out, final_norm):
    n_p, s_p, d = x_prompt.shape
    n_s, s_s, _ = x_sample.shape
    past = cache_gqa_k.shape[2]

    cond = jnp.concatenate([c_ctx[None, :], c, jnp.zeros((MOD_ROWS - 1 - n_s, d), F32)], axis=0)
    mods = _ada_mod(cond, ada_w, ada_b)

    def rows(a):
        return a.reshape(a.shape[0], 1, a.shape[1])

    aw = _layout_attn_weights(attn_w_in, attn_w_kv_up, attn_w_out)
    g_mix, g_ffn = rows(norm_mix), rows(norm_ffn)
    qn, kn, kvn = rows(attn_q_norm), rows(attn_k_norm), rows(attn_kv_norm)
    wg, wu, wd = ffn_gate.astype(BF16), ffn_up.astype(BF16), ffn_down.astype(BF16)
    cw_in, cw_out, cw_s = cmlp_w_in.astype(BF16), cmlp_w_out.astype(BF16), cmlp_w_s.astype(BF16)
    c_vn = rows(cmlp_v_norm)
    c_bias = jnp.repeat(jnp.swapaxes(cmlp_b_s, 1, 2), C_GROUP_W, axis=2)
    final_g = final_norm.reshape(1, d)

    xp = x_prompt.reshape(1, n_p * s_p, d)
    xs = x_sample
    rope = _rope_tables(s_s)
    cache_k = cache_gqa_k.reshape(n_s, -1, past, A_KV_W)
    cache_v = cache_gqa_v.reshape(n_s, -1, past, A_KV_W)

    states = ([], [], [], [])
    for i in range(DEPTH):
        j = i // 2
        if i % 2 == 0:
            qa, ka, vat, qb, kb, vbt, st_k, st_v, st_c, st_r = _attn_project(
                xp, mods, 0, i, j, g_mix, aw, qn, kn, kvn)
            for lst, st in zip(states, (st_k, st_v, st_c, st_r)):
                lst.append(st)
            oa = _attention(qa, [(ka, vat)], seq_len=s_p, hb=A_HEADS, group=A_GROUP, dq=HEAD_DIM,
                            tqb=s_p, tq=s_p, rolled=False)
            ob = _attention(qb, [(kb, vbt)], seq_len=s_p, hb=B_HEADS, group=1, dq=B_QK_PAD,
                            tqb=s_p, tq=s_p, rolled=False)
            xp = _attn_out(xp, mods, 0, i, j, oa, ob, aw.out)

            qa, ka, vat, qb, kb, vbt = _attn_project(xs, mods, 1, i, j, g_mix, aw, qn, kn, kvn, rope=rope)
            kb_ctx, vbt_ctx = _ctx_kv_expand(cache_mla_ckv, cache_mla_krope, j, aw)
            oa = _attention(qa, [(ka, vat), (cache_k, cache_v)], seq_len=s_s, hb=A_GROUP, group=A_GROUP,
                            dq=HEAD_DIM, tqb=2048, tq=TQ_ATTN, rolled=True, seg_layer=j)
            ob = _attention(qb, [(kb, vbt), (kb_ctx, vbt_ctx)], seq_len=s_s, hb=1, group=1, dq=B_QK_PAD,
                            tqb=4096, tq=TQ_ATTN, rolled=True)
            xs = _attn_out(xs, mods, 1, i, j, oa, ob, aw.out)
        else:
            xp = _chunk_mlp(xp, mods, 0, i, j, g_mix, cw_in, c_vn, cw_s, c_bias, cw_out)
            xs = _chunk_mlp(xs, mods, 1, i, j, g_mix, cw_in, c_vn, cw_s, c_bias, cw_out)

        fin = final_g if i == DEPTH - 1 else None
        xp = _ffn(xp, mods, 0, i, g_ffn, wg, wu, wd, fin)
        xs = _ffn(xs, mods, 1, i, g_ffn, wg, wu, wd, fin)

    y_prompt = xp.reshape(n_p, s_p, d)
    st_k, st_v, st_c, st_r = (jnp.stack([a.reshape(n_p, s_p, a.shape[-1]) for a in lst], axis=1)
                              for lst in states)
    return (y_prompt, xs,
            st_k.reshape(n_p, -1, s_p, A_KV_HEADS, HEAD_DIM),
            st_v.reshape(n_p, -1, s_p, A_KV_HEADS, HEAD_DIM),
            st_c, st_r)
```

```python
import functools
from typing import NamedTuple

import numpy as np
import jax
import jax.numpy as jnp
from jax import lax
from jax.experimental import pallas as pl
from jax.experimental.pallas import tpu as pltpu

F32 = jnp.float32
BF16 = jnp.bfloat16

D_MODEL = 2048
DEPTH = 4
GRID_W = 64
ROPE_BASE = 10000.0
EPS = 1e-6
A_HEADS = 8
A_KV_HEADS = 2
A_GROUP = A_HEADS // A_KV_HEADS
HEAD_DIM = 128
A_SCALE = HEAD_DIM ** -0.5
B_HEADS = 8
B_NOPE = 128
B_ROPE = 64
B_V = 128
KV_RANK = 512
B_SCALE = (B_NOPE + B_ROPE) ** -0.5
B_QK_PAD = 256
A_Q_W = A_HEADS * HEAD_DIM
A_KV_W = A_KV_HEADS * HEAD_DIM
B_Q_W = B_HEADS * (B_NOPE + B_ROPE)
CHUNK = 128
C_GROUPS = 8
C_GROUP_W = D_MODEL // C_GROUPS
FFN_HIDDEN = 5632
LOG2E = 1.4426950408889634
LANES = 128

Z_QA = 0
Z_KA = Z_QA + A_Q_W
Z_VA = Z_KA + A_KV_W
Z_QB = Z_VA + A_KV_W
Z_CKV = Z_QB + B_HEADS * B_QK_PAD
Z_KR = Z_CKV + KV_RANK
Z_W = Z_KR + LANES

MOD_ROWS = 8
_NT_DIMS = (((1,), (1,)), ((), ()))

TM_FFN, TC_FFN = 512, 512
TM_PROJ = 256
TM_ATTN_OUT = 512
TM_CMLP = 256
TN_ADA = 1024
TQ_ATTN = 256


def _cparams(sem, vmem_mib, flags=None):
    return pltpu.CompilerParams(dimension_semantics=sem, vmem_limit_bytes=vmem_mib << 20, flags=flags)


def _layer_resident(shape, layer):
    n = len(shape)
    return pl.BlockSpec((None,) + tuple(shape), lambda *_: (layer,) + (0,) * n, pipeline_mode=pl.Buffered(1))


def _layer_row(width, layer):
    return pl.BlockSpec((None, 1, width), lambda *_: (layer, 0, 0))


def _mod_spec(layer, row0):
    return pl.BlockSpec((None, None, 6, D_MODEL), lambda b, *_: (layer, row0 + b, 0, 0))


def _rms(x):
    return x * lax.rsqrt(jnp.mean(x * x, axis=-1, keepdims=True) + EPS)


def _modulate(x, g, shift, scale):
    return (_rms(x) * g) * (1.0 + scale) + shift


def _swap_halves(v, n):
    lane = lax.broadcasted_iota(jnp.int32, v.shape, 1)
    fwd = pltpu.roll(v, LANES - n, 1)
    bwd = pltpu.roll(v, n, 1)
    return jnp.where((lane & (2 * n - 1)) < n, fwd, bwd)


def _ada_kernel(cond_ref, w_ref, b_ref, o_ref):
    c = cond_ref[...]
    a = (c * jax.nn.sigmoid(c)).astype(BF16)
    o_ref[...] = jnp.dot(a, w_ref[...].astype(BF16), preferred_element_type=F32) + b_ref[...]


def _ada_mod(cond, ada_w, ada_b):
    n6 = 6 * D_MODEL
    out = pl.pallas_call(
        _ada_kernel,
        grid=(DEPTH, n6 // TN_ADA),
        in_specs=[
            pl.BlockSpec((MOD_ROWS, D_MODEL), lambda l, n: (0, 0)),
            pl.BlockSpec((None, D_MODEL, TN_ADA), lambda l, n: (l, 0, n)),
            pl.BlockSpec((None, 1, TN_ADA), lambda l, n: (l, 0, n)),
        ],
        out_specs=pl.BlockSpec((None, MOD_ROWS, TN_ADA), lambda l, n: (l, 0, n)),
        out_shape=jax.ShapeDtypeStruct((DEPTH, MOD_ROWS, n6), F32),
        compiler_params=_cparams(("arbitrary", "arbitrary"), 40),
        name="ada_mod",
    )(cond, ada_w, ada_b.reshape(DEPTH, 1, n6))
    return out.reshape(DEPTH, MOD_ROWS, 6, D_MODEL)


def _ffn_kernel(*refs, n_chunks, final):
    if final:
        x_ref, xb_ref, mod_ref, wg_ref, wu_ref, wd_ref, fn_ref, o_ref = refs
    else:
        x_ref, xb_ref, mod_ref, wg_ref, wu_ref, wd_ref, o_ref = refs
    j = pl.program_id(2)

    @pl.when(j == 0)
    def _():
        o_ref[...] = jnp.zeros_like(o_ref)

    xb = xb_ref[...]
    gate = jnp.dot(xb, wg_ref[...], preferred_element_type=F32)
    up = jnp.dot(xb, wu_ref[...], preferred_element_type=F32)
    h = (gate * jax.nn.sigmoid(gate) * up).astype(BF16)
    o_ref[...] += jnp.dot(h, wd_ref[...], preferred_element_type=F32)

    @pl.when(j == n_chunks - 1)
    def _():
        y = x_ref[...] + mod_ref[5:6, :] * o_ref[...]
        if final:
            y = _rms(y) * fn_ref[...]
        o_ref[...] = y


def _ffn(x, xb, mods, row0, layer, wg, wu, wd, final_g=None):
    nb, s, d = x.shape
    tm, tc = TM_FFN, TC_FFN
    n_chunks = FFN_HIDDEN // tc
    final = final_g is not None
    in_specs = [
        pl.BlockSpec((None, tm, d), lambda b, i, j: (b, i, 0)),
        pl.BlockSpec((None, tm, d), lambda b, i, j: (b, i, 0)),
        _mod_spec(layer, row0),
        pl.BlockSpec((None, d, tc), lambda b, i, j: (layer, 0, j)),
        pl.BlockSpec((None, d, tc), lambda b, i, j: (layer, 0, j)),
        pl.BlockSpec((None, tc, d), lambda b, i, j: (layer, j, 0)),
    ]
    args = [x, xb, mods, wg, wu, wd]
    if final:
        in_specs.append(pl.BlockSpec((1, d), lambda b, i, j: (0, 0)))
        args.append(final_g)
    return pl.pallas_call(
        functools.partial(_ffn_kernel, n_chunks=n_chunks, final=final),
        grid=(nb, s // tm, n_chunks),
        in_specs=in_specs,
        out_specs=pl.BlockSpec((None, tm, d), lambda b, i, j: (b, i, 0)),
        out_shape=jax.ShapeDtypeStruct((nb, s, d), F32),
        compiler_params=_cparams(("parallel", "parallel", "arbitrary"), 48),
        name="ffn_final" if final else "ffn",
    )(*args)


class _AttnWeights(NamedTuple):
    main: jax.Array
    va_t: jax.Array
    kv_k: jax.Array
    kv_vt: jax.Array
    out: jax.Array


def _layout_attn_weights(w_in, w_kv_up, w_out):
    n_l, d, _ = w_in.shape
    o = A_Q_W + 2 * A_KV_W
    qb = w_in[:, :, o:o + B_Q_W].reshape(n_l, d, B_HEADS, B_NOPE + B_ROPE)
    qb = jnp.pad(qb, ((0, 0), (0, 0), (0, 0), (0, B_QK_PAD - B_NOPE - B_ROPE)))
    ckv = w_in[:, :, o + B_Q_W:o + B_Q_W + KV_RANK]
    kr = jnp.pad(w_in[:, :, o + B_Q_W + KV_RANK:], ((0, 0), (0, 0), (0, LANES - B_ROPE)))
    main = jnp.concatenate([w_in[:, :, :o], qb.reshape(n_l, d, B_HEADS * B_QK_PAD), ckv, kr], axis=2)
    kv4 = w_kv_up.reshape(n_l, KV_RANK, B_HEADS, B_NOPE + B_V)
    return _AttnWeights(
        main=main.astype(BF16),
        va_t=jnp.swapaxes(w_in[:, :, Z_VA:Z_VA + A_KV_W], 1, 2).astype(BF16),
        kv_k=kv4[..., :B_NOPE].reshape(n_l, KV_RANK, -1).astype(BF16),
        kv_vt=jnp.swapaxes(kv4[..., B_NOPE:].reshape(n_l, KV_RANK, -1), 1, 2).astype(BF16),
        out=w_out.astype(BF16))


def _proj_kernel(*refs, prompt):
    if prompt:
        (x_ref, mod_ref, g_ref, w_ref, wvat_ref, wkv_ref, wkvt_ref, qn_ref, kn_ref, kvn_ref,
         qa_o, ka_o, vat_o, qb_o, kb_o, vbt_o, sk_o, sv_o, sc_o, sr_o) = refs
    else:
        (x_ref, mod_ref, g_ref, w_ref, wvat_ref, wkv_ref, wkvt_ref, qn_ref, kn_ref, kvn_ref,
         c128_ref, s128_ref, c64_ref, s64_ref,
         qa_o, ka_o, vat_o, qb_o, kb_o, vbt_o) = refs
        c128, s128 = c128_ref[...], s128_ref[...]
        c64, s64 = c64_ref[...], s64_ref[...]

    def rope128(v):
        return v if prompt else v * c128 + _swap_halves(v, 32) * s128

    def rope64(v):
        return v if prompt else v * c64 + _swap_halves(v, 16) * s64

    xb = _modulate(x_ref[...], g_ref[...], mod_ref[0:1, :], mod_ref[1:2, :]).astype(BF16)
    z = jnp.dot(xb, w_ref[...], preferred_element_type=F32)
    vat_o[...] = lax.dot_general(wvat_ref[...], xb, _NT_DIMS, preferred_element_type=F32).astype(BF16)

    for h in range(A_HEADS):
        q = _rms(z[:, Z_QA + h * HEAD_DIM:Z_QA + (h + 1) * HEAD_DIM]) * qn_ref[...]
        qa_o[h] = (rope128(q) * (A_SCALE * LOG2E)).astype(BF16)
    for h in range(A_KV_HEADS):
        sl = slice(h * HEAD_DIM, (h + 1) * HEAD_DIM)
        k = _rms(z[:, Z_KA + h * HEAD_DIM:Z_KA + (h + 1) * HEAD_DIM]) * kn_ref[...]
        if prompt:
            sk_o[:, sl] = k
        ka_o[:, sl] = rope128(k).astype(BF16)
    if prompt:
        sv_o[...] = z[:, Z_VA:Z_VA + A_KV_W]
    for h in range(B_HEADS):
        base = Z_QB + h * B_QK_PAD
        qb_o[h, :, :B_NOPE] = (z[:, base:base + B_NOPE] * (B_SCALE * LOG2E)).astype(BF16)
        qb_o[h, :, B_NOPE:] = (rope64(z[:, base + B_NOPE:base + B_QK_PAD]) * (B_SCALE * LOG2E)).astype(BF16)

    ckv = _rms(z[:, Z_CKV:Z_CKV + KV_RANK]) * kvn_ref[...]
    kr = z[:, Z_KR:Z_KR + LANES]
    if prompt:
        sc_o[...] = ckv
        sr_o[...] = kr[:, :B_ROPE]
    krb = rope64(kr).astype(BF16)
    ckvb = ckv.astype(BF16)
    kn_all = jnp.dot(ckvb, wkv_ref[...], preferred_element_type=F32)
    for h in range(B_HEADS):
        kb_o[:, h * B_QK_PAD:h * B_QK_PAD + B_NOPE] = kn_all[:, h * B_NOPE:(h + 1) * B_NOPE].astype(BF16)
        kb_o[:, h * B_QK_PAD + B_NOPE:(h + 1) * B_QK_PAD] = krb
    vbt_o[...] = lax.dot_general(wkvt_ref[...], ckvb, _NT_DIMS, preferred_element_type=F32).astype(BF16)


def _attn_project(x, mods, row0, layer, attn_layer, g, w, qn, kn, kvn, rope=None):
    nb, s, d = x.shape
    tm = TM_PROJ
    prompt = rope is None

    def row(wd):
        return pl.BlockSpec((None, tm, wd), lambda b, i: (b, i, 0))

    def col(wd):
        return pl.BlockSpec((None, wd, tm), lambda b, i: (b, 0, i))

    def heads(n, wd):
        return pl.BlockSpec((None, n, tm, wd), lambda b, i: (b, 0, i, 0))

    in_specs = [
        row(d),
        _mod_spec(layer, row0),
        _layer_row(d, layer),
        _layer_resident((d, Z_W), attn_layer),
        _layer_resident((A_KV_W, d), attn_layer),
        _layer_resident((KV_RANK, B_HEADS * B_NOPE), attn_layer),
        _layer_resident((B_HEADS * B_V, KV_RANK), attn_layer),
        _layer_row(HEAD_DIM, attn_layer),
        _layer_row(HEAD_DIM, attn_layer),
        _layer_row(KV_RANK, attn_layer),
    ]
    args = [x, mods, g, w.main, w.va_t, w.kv_k, w.kv_vt, qn, kn, kvn]
    out_specs = [heads(A_HEADS, HEAD_DIM), row(A_KV_W), col(A_KV_W), heads(B_HEADS, B_QK_PAD),
                 row(B_HEADS * B_QK_PAD), col(B_HEADS * B_V)]
    out_shape = [jax.ShapeDtypeStruct(shp, BF16) for shp in (
        (nb, A_HEADS, s, HEAD_DIM), (nb, s, A_KV_W), (nb, A_KV_W, s), (nb, B_HEADS, s, B_QK_PAD),
        (nb, s, B_HEADS * B_QK_PAD), (nb, B_HEADS * B_V, s))]
    if prompt:
        for wd in (A_KV_W, A_KV_W, KV_RANK, B_ROPE):
            out_specs.append(row(wd))
            out_shape.append(jax.ShapeDtypeStruct((nb, s, wd), F32))
    else:
        in_specs += [pl.BlockSpec((tm, LANES), lambda b, i: (i, 0))] * 4
        args += list(rope)
    return pl.pallas_call(
        functools.partial(_proj_kernel, prompt=prompt),
        grid=(nb, s // tm),
        in_specs=in_specs,
        out_specs=out_specs,
        out_shape=out_shape,
        compiler_params=_cparams(("parallel", "parallel"), 52),
        name="attn_proj_prompt" if prompt else "attn_proj_latent",
    )(*args)


def _ctx_kv_kernel(ckv_ref, kr_ref, wkv_ref, wkvt_ref, kb_o, vbt_o):
    ckvb = ckv_ref[...].astype(BF16)
    kn_all = jnp.dot(ckvb, wkv_ref[...], preferred_element_type=F32)
    krb = kr_ref[...].astype(BF16)
    zero = jnp.zeros((krb.shape[0], B_QK_PAD - B_NOPE - B_ROPE), BF16)
    for h in range(B_HEADS):
        base = h * B_QK_PAD
        kb_o[:, base:base + B_NOPE] = kn_all[:, h * B_NOPE:(h + 1) * B_NOPE].astype(BF16)
        kb_o[:, base + B_NOPE:base + B_NOPE + B_ROPE] = krb
        kb_o[:, base + B_NOPE + B_ROPE:base + B_QK_PAD] = zero
    vbt_o[...] = lax.dot_general(wkvt_ref[...], ckvb, _NT_DIMS, preferred_element_type=F32).astype(BF16)


def _ctx_kv_expand(cache_ckv, cache_kr, attn_layer, w):
    nb, _, p, _ = cache_ckv.shape
    return pl.pallas_call(
        _ctx_kv_kernel,
        grid=(nb,),
        in_specs=[
            pl.BlockSpec((None, None, p, KV_RANK), lambda b: (b, attn_layer, 0, 0)),
            pl.BlockSpec((None, None, p, B_ROPE), lambda b: (b, attn_layer, 0, 0)),
            pl.BlockSpec((None, KV_RANK, B_HEADS * B_NOPE), lambda b: (attn_layer, 0, 0)),
            pl.BlockSpec((None, B_HEADS * B_V, KV_RANK), lambda b: (attn_layer, 0, 0)),
        ],
        out_specs=[
            pl.BlockSpec((None, p, B_HEADS * B_QK_PAD), lambda b: (b, 0, 0)),
            pl.BlockSpec((None, B_HEADS * B_V, p), lambda b: (b, 0, 0)),
        ],
        out_shape=[
            jax.ShapeDtypeStruct((nb, p, B_HEADS * B_QK_PAD), BF16),
            jax.ShapeDtypeStruct((nb, B_HEADS * B_V, p), BF16),
        ],
        compiler_params=_cparams(("parallel",), 32),
        name="ctx_kv_expand",
    )(cache_ckv, cache_kr, w.kv_k, w.kv_vt)


def _attn_kernel(*refs, seg_len, v_transposed, hb, group, dq, dv, tq, rolled):
    nseg = len(seg_len)
    q_ref = refs[0]
    kv_refs = refs[1:1 + 2 * nseg]
    o_ref, s_ref, m_ref = refs[1 + 2 * nseg:]
    n_sub = q_ref.shape[1] // tq
    n_items = hb * n_sub
    seg_off = [sum(seg_len[:i]) for i in range(nseg)]

    def split(i):
        if isinstance(i, int):
            return i // n_sub, (i % n_sub) * tq
        shift = n_sub.bit_length() - 1
        return lax.shift_right_logical(i, shift), pl.multiple_of((i & (n_sub - 1)) * tq, tq)

    def kv_head(h):
        return h // group if isinstance(h, int) else 0

    def score(i, slot):
        h, r = split(i)
        kvh = kv_head(h)
        q = q_ref[h, pl.ds(r, tq), :]
        m = None
        for s in range(nseg):
            k = kv_refs[2 * s][:, kvh * dq:(kvh + 1) * dq].astype(BF16)
            st = lax.dot_general(k, q, _NT_DIMS, preferred_element_type=F32)
            s_ref[slot, seg_off[s]:seg_off[s] + seg_len[s], :] = st
            ms = jnp.max(st, axis=0, keepdims=True)
            m = ms if m is None else jnp.maximum(m, ms)
        m_ref[slot] = m

    def finish(i, slot):
        h, r = split(i)
        kvh = kv_head(h)
        m = m_ref[slot]
        denom = None
        acc = None
        for s in range(nseg):
            p = jnp.exp2(s_ref[slot, seg_off[s]:seg_off[s] + seg_len[s], :] - m)
            v_ref = kv_refs[2 * s + 1]
            if v_transposed[s]:
                vt = v_ref[kvh * dv:(kvh + 1) * dv, :]
            else:
                vt = v_ref[:, kvh * dv:(kvh + 1) * dv].T
            ls = jnp.sum(p, axis=0, keepdims=True)
            os_ = jnp.dot(vt.astype(BF16), p.astype(BF16), preferred_element_type=F32)
            denom = ls if denom is None else denom + ls
            acc = os_ if acc is None else acc + os_
        o_ref[h, pl.ds(r, tq), :] = (acc / denom).T.astype(o_ref.dtype)

    if rolled:
        assert hb == group and n_items % 2 == 0 and n_items >= 4 and n_sub & (n_sub - 1) == 0
        score(0, 0)

        def body(k, carry):
            i0 = 2 * k
            score(i0 + 1, 1)
            finish(i0, 0)
            score(i0 + 2, 0)
            finish(i0 + 1, 1)
            return carry

        lax.fori_loop(0, n_items // 2 - 1, body, 0)
        score(n_items - 1, 1)
        finish(n_items - 2, 0)
        finish(n_items - 1, 1)
    else:
        score(0, 0)
        for i in range(n_items):
            if i + 1 < n_items:
                score(i + 1, (i + 1) % 2)
            finish(i, i % 2)


def _attention(q, segs, *, seq_len, hb, group, dq, tqb, tq, rolled, seg_layer=None):
    nbq, n_heads, s_tot, _ = q.shape
    dv = HEAD_DIM
    per = s_tot // seq_len
    n_qb = seq_len // tqb
    nkv = hb // group
    in_specs = [pl.BlockSpec((None, hb, tqb, dq), lambda g, h, t: (g // per, h, (g % per) * n_qb + t, 0))]
    args = [q]
    v_transposed = []
    seg_lens = []
    for k, v in segs:
        if k.ndim == 4:
            sk = k.shape[2]
            in_specs.append(pl.BlockSpec((None, None, sk, nkv * dq), lambda g, h, t: (g, seg_layer, 0, h)))
            in_specs.append(pl.BlockSpec((None, None, sk, nkv * dv), lambda g, h, t: (g, seg_layer, 0, h)))
            v_transposed.append(False)
        else:
            sk = k.shape[1] // per
            in_specs.append(pl.BlockSpec((None, sk, nkv * dq), lambda g, h, t: (g // per, g % per, h)))
            in_specs.append(pl.BlockSpec((None, nkv * dv, sk), lambda g, h, t: (g // per, h, g % per)))
            v_transposed.append(True)
        seg_lens.append(sk)
        args += [k, v]
    return pl.pallas_call(
        functools.partial(_attn_kernel, seg_len=tuple(seg_lens), v_transposed=tuple(v_transposed), hb=hb,
                          group=group, dq=dq, dv=dv, tq=tq, rolled=rolled),
        grid=(nbq * per, n_heads // hb, n_qb),
        in_specs=in_specs,
        out_specs=pl.BlockSpec((None, hb, tqb, dv), lambda g, h, t: (g // per, h, (g % per) * n_qb + t, 0)),
        out_shape=jax.ShapeDtypeStruct((nbq, n_heads, s_tot, dv), BF16),
        scratch_shapes=[pltpu.VMEM((2, sum(seg_lens), tq), F32), pltpu.VMEM((2, 1, tq), F32)],
        compiler_params=_cparams(("parallel", "parallel", "arbitrary"), 56),
        name="attention",
    )(*args)


def _close_mixer(x, y, mod_ref, gf_ref, o_ref, xb_o, rows=slice(None)):
    x_new = x + mod_ref[2:3, :] * y
    o_ref[rows, :] = x_new
    xb_o[rows, :] = _modulate(x_new, gf_ref[...], mod_ref[3:4, :], mod_ref[4:5, :]).astype(BF16)


def _attn_out_kernel(x_ref, mod_ref, gf_ref, oa_ref, ob_ref, w_ref, o_ref, xb_o):
    half = x_ref.shape[0] // 2
    for r in range(2):
        rows = slice(r * half, (r + 1) * half)
        cat = jnp.concatenate([oa_ref[h, rows, :] for h in range(A_HEADS)]
                              + [ob_ref[h, rows, :] for h in range(B_HEADS)], axis=-1)
        y = jnp.dot(cat, w_ref[...], preferred_element_type=F32)
        _close_mixer(x_ref[rows, :], y, mod_ref, gf_ref, o_ref, xb_o, rows)


def _attn_out(x, mods, row0, layer, attn_layer, g_ffn, oa, ob, w_out):
    nb, s, d = x.shape
    tm = TM_ATTN_OUT
    tile = pl.BlockSpec((None, tm, d), lambda b, i: (b, i, 0))
    return pl.pallas_call(
        _attn_out_kernel,
        grid=(nb, s // tm),
        in_specs=[
            tile,
            _mod_spec(layer, row0),
            _layer_row(d, layer),
            pl.BlockSpec((None, A_HEADS, tm, HEAD_DIM), lambda b, i: (b, 0, i, 0)),
            pl.BlockSpec((None, B_HEADS, tm, B_V), lambda b, i: (b, 0, i, 0)),
            _layer_resident((A_Q_W + B_HEADS * B_V, d), attn_layer),
        ],
        out_specs=[tile, tile],
        out_shape=[jax.ShapeDtypeStruct((nb, s, d), F32), jax.ShapeDtypeStruct((nb, s, d), BF16)],
        compiler_params=_cparams(("parallel", "parallel"), 44),
        name="attn_out",
    )(x, mods, g_ffn, oa, ob, w_out)


def _gelu_tanh(x):
    return 0.5 * x * (1.0 + jnp.tanh(0.7978845608028654 * (x + 0.044715 * (x * x * x))))


def _cmlp_kernel(x_ref, mod_ref, g_ref, gf_ref, win_ref, vn_ref, ws_ref, bias_ref, wout_ref, o_ref, xb_o, t_ref):
    x = x_ref[...]
    xm = _modulate(x, g_ref[...], mod_ref[0:1, :], mod_ref[1:2, :])
    z = _gelu_tanh(jnp.dot(xm.astype(BF16), win_ref[...], preferred_element_type=F32))
    u = z[:, :D_MODEL]
    vb = (_rms(z[:, D_MODEL:]) * vn_ref[...]).astype(BF16)
    for n in range(x.shape[0] // CHUNK):
        rows = slice(n * CHUNK, (n + 1) * CHUNK)
        for g in range(C_GROUPS):
            cols = slice(g * C_GROUP_W, (g + 1) * C_GROUP_W)
            sv = jnp.dot(ws_ref[g], vb[rows, cols], preferred_element_type=F32) + bias_ref[:, cols]
            t_ref[rows, cols] = (u[rows, cols] * sv).astype(BF16)
    y = jnp.dot(t_ref[...], wout_ref[...], preferred_element_type=F32)
    _close_mixer(x, y, mod_ref, gf_ref, o_ref, xb_o)


def _chunk_mlp(x, mods, row0, layer, c_layer, g, g_ffn, w_in, v_norm, w_s, bias, w_out):
    nb, s, d = x.shape
    tm = TM_CMLP
    tile = pl.BlockSpec((None, tm, d), lambda b, i: (b, i, 0))
    return pl.pallas_call(
        _cmlp_kernel,
        grid=(nb, s // tm),
        in_specs=[
            tile,
            _mod_spec(layer, row0),
            _layer_row(d, layer),
            _layer_row(d, layer),
            _layer_resident((d, 2 * d), c_layer),
            _layer_row(d, c_layer),
            _layer_resident((C_GROUPS, CHUNK, CHUNK), c_layer),
            _layer_resident((CHUNK, d), c_layer),
            _layer_resident((d, d), c_layer),
        ],
        out_specs=[tile, tile],
        out_shape=[jax.ShapeDtypeStruct((nb, s, d), F32), jax.ShapeDtypeStruct((nb, s, d), BF16)],
        scratch_shapes=[pltpu.VMEM((tm, d), BF16)],
        compiler_params=_cparams(("parallel", "parallel"), 54),
        name="chunk_mlp",
    )(x, mods, g, g_ffn, w_in, v_norm, w_s, bias, w_out)


def _rope_tables(n_tokens):
    t = np.arange(n_tokens)
    row = (t // GRID_W).astype(np.float32)
    col = (t % GRID_W).astype(np.float32)

    def axis_tables(pos, half):
        inv = (1.0 / (ROPE_BASE ** (np.arange(0, half, 2, dtype=np.float32) / half))).astype(np.float32)
        ang = (pos[:, None] * inv).astype(np.float32).astype(np.float64)
        c, s = np.cos(ang), np.sin(ang)
        return np.concatenate([c, c], axis=1), np.concatenate([-s, s], axis=1)

    def tables(dim):
        cr, sr = axis_tables(row, dim // 2)
        cc, sc = axis_tables(col, dim // 2)
        return np.concatenate([cr, cc], axis=1), np.concatenate([sr, sc], axis=1)

    c128, s128 = tables(HEAD_DIM)
    c64, s64 = tables(B_ROPE)
    pad = np.zeros((n_tokens, LANES - B_ROPE))
    c64 = np.concatenate([c64, pad], axis=1)
    s64 = np.concatenate([s64, pad], axis=1)
    return tuple(jnp.asarray(a, F32) for a in (c128, s128, c64, s64))


def kernel(x_prompt, x_sample, c, cache_gqa_k, cache_gqa_v, cache_mla_ckv, cache_mla_krope, c_ctx, ada_w, ada_b, norm_mix, norm_ffn, ffn_gate, ffn_up, ffn_down, attn_w_in, attn_q_norm, attn_k_norm, attn_kv_norm, attn_w_kv_up, attn_w_out, cmlp_w_in, cmlp_v_norm, cmlp_w_s, cmlp_b_s, cmlp_w_out, final_norm):
    n_p, s_p, d = x_prompt.shape
    n_s, s_s, _ = x_sample.shape
    past = cache_gqa_k.shape[2]

    cond = jnp.concatenate([c_ctx[None, :], c, jnp.zeros((MOD_ROWS - 1 - n_s, d), F32)], axis=0)
    mods = _ada_mod(cond, ada_w, ada_b)

    def rows(a):
        return a.reshape(a.shape[0], 1, a.shape[1])

    aw = _layout_attn_weights(attn_w_in, attn_w_kv_up, attn_w_out)
    g_mix, g_ffn = rows(norm_mix), rows(norm_ffn)
    qn, kn, kvn = rows(attn_q_norm), rows(attn_k_norm), rows(attn_kv_norm)
    wg, wu, wd = ffn_gate.astype(BF16), ffn_up.astype(BF16), ffn_down.astype(BF16)
    cw_in, cw_out, cw_s = cmlp_w_in.astype(BF16), cmlp_w_out.astype(BF16), cmlp_w_s.astype(BF16)
    c_vn = rows(cmlp_v_norm)
    c_bias = jnp.repeat(jnp.swapaxes(cmlp_b_s, 1, 2), C_GROUP_W, axis=2)
    final_g = final_norm.reshape(1, d)

    xp = x_prompt.reshape(1, n_p * s_p, d)
    xs = x_sample
    rope = _rope_tables(s_s)
    cache_k = cache_gqa_k.reshape(n_s, -1, past, A_KV_W)
    cache_v = cache_gqa_v.reshape(n_s, -1, past, A_KV_W)

    states = ([], [], [], [])
    for i in range(DEPTH):
        j = i // 2
        if i % 2 == 0:
            qa, ka, vat, qb, kb, vbt, st_k, st_v, st_c, st_r = _attn_project(
                xp, mods, 0, i, j, g_mix, aw, qn, kn, kvn)
            for lst, st in zip(states, (st_k, st_v, st_c, st_r)):
                lst.append(st)
            oa = _attention(qa, [(ka, vat)], seq_len=s_p, hb=A_HEADS, group=A_GROUP, dq=HEAD_DIM,
                            tqb=s_p, tq=s_p, rolled=False)
            ob = _attention(qb, [(kb, vbt)], seq_len=s_p, hb=B_HEADS, group=1, dq=B_QK_PAD,
                            tqb=s_p, tq=s_p, rolled=False)
            xp, xpb = _attn_out(xp, mods, 0, i, j, g_ffn, oa, ob, aw.out)

            qa, ka, vat, qb, kb, vbt = _attn_project(xs, mods, 1, i, j, g_mix, aw, qn, kn, kvn, rope=rope)
            kb_ctx, vbt_ctx = _ctx_kv_expand(cache_mla_ckv, cache_mla_krope, j, aw)
            oa = _attention(qa, [(ka, vat), (cache_k, cache_v)], seq_len=s_s, hb=A_GROUP, group=A_GROUP,
                            dq=HEAD_DIM, tqb=2048, tq=TQ_ATTN, rolled=True, seg_layer=j)
            ob = _attention(qb, [(kb, vbt), (kb_ctx, vbt_ctx)], seq_len=s_s, hb=1, group=1, dq=B_QK_PAD,
                            tqb=4096, tq=TQ_ATTN, rolled=True)
            xs, xsb = _attn_out(xs, mods, 1, i, j, g_ffn, oa, ob, aw.out)
        else:
            xp, xpb = _chunk_mlp(xp, mods, 0, i, j, g_mix, g_ffn, cw_in, c_vn, cw_s, c_bias, cw_out)
            xs, xsb = _chunk_mlp(xs, mods, 1, i, j, g_mix, g_ffn, cw_in, c_vn, cw_s, c_bias, cw_out)

        fin = final_g if i == DEPTH - 1 else None
        xp = _ffn(xp, xpb, mods, 0, i, wg, wu, wd, fin)
        xs = _ffn(xs, xsb, mods, 1, i, wg, wu, wd, fin)

    y_prompt = xp.reshape(n_p, s_p, d)
    st_k, st_v, st_c, st_r = (jnp.stack([a.reshape(n_p, s_p, a.shape[-1]) for a in lst], axis=1)
                              for lst in states)
    return (y_prompt, xs,
            st_k.reshape(n_p, -1, s_p, A_KV_HEADS, HEAD_DIM),
            st_v.reshape(n_p, -1, s_p, A_KV_HEADS, HEAD_DIM),
            st_c, st_r)
```

```python
import functools
from typing import NamedTuple

import numpy as np
import jax
import jax.numpy as jnp
from jax import lax
from jax.experimental import pallas as pl
from jax.experimental.pallas import tpu as pltpu

F32 = jnp.float32
BF16 = jnp.bfloat16

D_MODEL = 2048
DEPTH = 4
GRID_W = 64
ROPE_BASE = 10000.0
EPS = 1e-6
A_HEADS = 8
A_KV_HEADS = 2
A_GROUP = A_HEADS // A_KV_HEADS
HEAD_DIM = 128
A_SCALE = HEAD_DIM ** -0.5
B_HEADS = 8
B_NOPE = 128
B_ROPE = 64
B_V = 128
KV_RANK = 512
B_SCALE = (B_NOPE + B_ROPE) ** -0.5
B_QK_PAD = 256
A_Q_W = A_HEADS * HEAD_DIM
A_KV_W = A_KV_HEADS * HEAD_DIM
B_Q_W = B_HEADS * (B_NOPE + B_ROPE)
CHUNK = 128
C_GROUPS = 8
C_GROUP_W = D_MODEL // C_GROUPS
FFN_HIDDEN = 5632
LOG2E = 1.4426950408889634
LANES = 128

Z_QA = 0
Z_KA = Z_QA + A_Q_W
Z_VA = Z_KA + A_KV_W
Z_QB = Z_VA + A_KV_W
Z_CKV = Z_QB + B_HEADS * B_QK_PAD
Z_KR = Z_CKV + KV_RANK
Z_W = Z_KR + LANES

MOD_ROWS = 8
_NT_DIMS = (((1,), (1,)), ((), ()))

TM_FFN, TC_FFN = 1024, 512
TM_PROJ = 256
TM_ATTN_OUT = 512
TM_CMLP = 256
TN_ADA = 1024
TQ_ATTN = 256


def _cparams(sem, vmem_mib, flags=None):
    return pltpu.CompilerParams(dimension_semantics=sem, vmem_limit_bytes=vmem_mib << 20, flags=flags)


def _layer_resident(shape, layer):
    n = len(shape)
    return pl.BlockSpec((None,) + tuple(shape), lambda *_: (layer,) + (0,) * n, pipeline_mode=pl.Buffered(1))


def _layer_row(width, layer):
    return pl.BlockSpec((None, 1, width), lambda *_: (layer, 0, 0))


def _mod_spec(layer, row0):
    return pl.BlockSpec((None, None, 6, D_MODEL), lambda b, *_: (layer, row0 + b, 0, 0))


def _rms(x):
    return x * lax.rsqrt(jnp.mean(x * x, axis=-1, keepdims=True) + EPS)


def _modulate(x, g, shift, scale):
    return (_rms(x) * g) * (1.0 + scale) + shift


def _swap_halves(v, n):
    lane = lax.broadcasted_iota(jnp.int32, v.shape, 1)
    fwd = pltpu.roll(v, LANES - n, 1)
    bwd = pltpu.roll(v, n, 1)
    return jnp.where((lane & (2 * n - 1)) < n, fwd, bwd)


def _ada_kernel(cond_ref, w_ref, b_ref, o_ref):
    c = cond_ref[...]
    a = (c * jax.nn.sigmoid(c)).astype(BF16)
    o_ref[...] = jnp.dot(a, w_ref[...].astype(BF16), preferred_element_type=F32) + b_ref[...]


def _ada_mod(cond, ada_w, ada_b):
    n6 = 6 * D_MODEL
    out = pl.pallas_call(
        _ada_kernel,
        grid=(DEPTH, n6 // TN_ADA),
        in_specs=[
            pl.BlockSpec((MOD_ROWS, D_MODEL), lambda l, n: (0, 0)),
            pl.BlockSpec((None, D_MODEL, TN_ADA), lambda l, n: (l, 0, n)),
            pl.BlockSpec((None, 1, TN_ADA), lambda l, n: (l, 0, n)),
        ],
        out_specs=pl.BlockSpec((None, MOD_ROWS, TN_ADA), lambda l, n: (l, 0, n)),
        out_shape=jax.ShapeDtypeStruct((DEPTH, MOD_ROWS, n6), F32),
        compiler_params=_cparams(("arbitrary", "arbitrary"), 40),
        name="ada_mod",
    )(cond, ada_w, ada_b.reshape(DEPTH, 1, n6))
    return out.reshape(DEPTH, MOD_ROWS, 6, D_MODEL)


def _ffn_kernel(*refs, n_chunks, final):
    if final:
        x_hbm, xb_ref, mod_ref, wg_ref, wu_ref, wd_ref, fn_ref, o_ref, x_buf, x_sem = refs
    else:
        x_hbm, xb_ref, mod_ref, wg_ref, wu_ref, wd_ref, o_ref, x_buf, x_sem = refs
    b, i, j = pl.program_id(0), pl.program_id(1), pl.program_id(2)
    tm = x_buf.shape[0]
    x_copy = pltpu.make_async_copy(x_hbm.at[b, pl.ds(pl.multiple_of(i * tm, tm), tm), :], x_buf, x_sem)

    @pl.when(j == 0)
    def _():
        x_copy.start()
        o_ref[...] = jnp.zeros_like(o_ref)

    xb = xb_ref[...]
    gate = jnp.dot(xb, wg_ref[...], preferred_element_type=F32)
    up = jnp.dot(xb, wu_ref[...], preferred_element_type=F32)
    h = (gate * jax.nn.sigmoid(gate) * up).astype(BF16)
    o_ref[...] += jnp.dot(h, wd_ref[...], preferred_element_type=F32)

    @pl.when(j == n_chunks - 1)
    def _():
        x_copy.wait()
        y = x_buf[...] + mod_ref[5:6, :] * o_ref[...]
        if final:
            y = _rms(y) * fn_ref[...]
        o_ref[...] = y


def _ffn(x, xb, mods, row0, layer, wg, wu, wd, final_g=None):
    nb, s, d = x.shape
    tm, tc = TM_FFN, TC_FFN
    n_chunks = FFN_HIDDEN // tc
    final = final_g is not None
    in_specs = [
        pl.BlockSpec(memory_space=pl.ANY),
        pl.BlockSpec((None, tm, d), lambda b, i, j: (b, i, 0)),
        _mod_spec(layer, row0),
        pl.BlockSpec((None, d, tc), lambda b, i, j: (layer, 0, j)),
        pl.BlockSpec((None, d, tc), lambda b, i, j: (layer, 0, j)),
        pl.BlockSpec((None, tc, d), lambda b, i, j: (layer, j, 0)),
    ]
    args = [x, xb, mods, wg, wu, wd]
    if final:
        in_specs.append(pl.BlockSpec((1, d), lambda b, i, j: (0, 0)))
        args.append(final_g)
    return pl.pallas_call(
        functools.partial(_ffn_kernel, n_chunks=n_chunks, final=final),
        grid=(nb, s // tm, n_chunks),
        in_specs=in_specs,
        out_specs=pl.BlockSpec((None, tm, d), lambda b, i, j: (b, i, 0)),
        out_shape=jax.ShapeDtypeStruct((nb, s, d), F32),
        scratch_shapes=[pltpu.VMEM((tm, d), F32), pltpu.SemaphoreType.DMA(())],
        compiler_params=_cparams(("arbitrary", "arbitrary", "arbitrary"), 60),
        name="ffn_final" if final else "ffn",
    )(*args)


class _AttnWeights(NamedTuple):
    main: jax.Array
    va_t: jax.Array
    kv_k: jax.Array
    kv_vt: jax.Array
    out: jax.Array


def _layout_attn_weights(w_in, w_kv_up, w_out):
    n_l, d, _ = w_in.shape
    o = A_Q_W + 2 * A_KV_W
    qb = w_in[:, :, o:o + B_Q_W].reshape(n_l, d, B_HEADS, B_NOPE + B_ROPE)
    qb = jnp.pad(qb, ((0, 0), (0, 0), (0, 0), (0, B_QK_PAD - B_NOPE - B_ROPE)))
    ckv = w_in[:, :, o + B_Q_W:o + B_Q_W + KV_RANK]
    kr = jnp.pad(w_in[:, :, o + B_Q_W + KV_RANK:], ((0, 0), (0, 0), (0, LANES - B_ROPE)))
    main = jnp.concatenate([w_in[:, :, :o], qb.reshape(n_l, d, B_HEADS * B_QK_PAD), ckv, kr], axis=2)
    kv4 = w_kv_up.reshape(n_l, KV_RANK, B_HEADS, B_NOPE + B_V)
    return _AttnWeights(
        main=main.astype(BF16),
        va_t=jnp.swapaxes(w_in[:, :, Z_VA:Z_VA + A_KV_W], 1, 2).astype(BF16),
        kv_k=kv4[..., :B_NOPE].reshape(n_l, KV_RANK, -1).astype(BF16),
        kv_vt=jnp.swapaxes(kv4[..., B_NOPE:].reshape(n_l, KV_RANK, -1), 1, 2).astype(BF16),
        out=w_out.astype(BF16))


def _proj_kernel(*refs, prompt):
    if prompt:
        (x_ref, mod_ref, g_ref, w_ref, wvat_ref, wkv_ref, wkvt_ref, qn_ref, kn_ref, kvn_ref,
         qa_o, ka_o, vat_o, qb_o, kb_o, vbt_o, sk_o, sv_o, sc_o, sr_o) = refs
    else:
        (x_ref, mod_ref, g_ref, w_ref, wvat_ref, wkv_ref, wkvt_ref, qn_ref, kn_ref, kvn_ref,
         c128_ref, s128_ref, c64_ref, s64_ref,
         qa_o, ka_o, vat_o, qb_o, kb_o, vbt_o) = refs
        c128, s128 = c128_ref[...], s128_ref[...]
        c64, s64 = c64_ref[...], s64_ref[...]

    def rope128(v):
        return v if prompt else v * c128 + _swap_halves(v, 32) * s128

    def rope64(v):
        return v if prompt else v * c64 + _swap_halves(v, 16) * s64

    xb = _modulate(x_ref[...], g_ref[...], mod_ref[0:1, :], mod_ref[1:2, :]).astype(BF16)
    z = jnp.dot(xb, w_ref[...], preferred_element_type=F32)
    vat_o[...] = lax.dot_general(wvat_ref[...], xb, _NT_DIMS, preferred_element_type=F32).astype(BF16)

    for h in range(A_HEADS):
        q = _rms(z[:, Z_QA + h * HEAD_DIM:Z_QA + (h + 1) * HEAD_DIM]) * qn_ref[...]
        qa_o[h] = (rope128(q) * (A_SCALE * LOG2E)).astype(BF16)
    for h in range(A_KV_HEADS):
        sl = slice(h * HEAD_DIM, (h + 1) * HEAD_DIM)
        k = _rms(z[:, Z_KA + h * HEAD_DIM:Z_KA + (h + 1) * HEAD_DIM]) * kn_ref[...]
        if prompt:
            sk_o[:, sl] = k
        ka_o[:, sl] = rope128(k).astype(BF16)
    if prompt:
        sv_o[...] = z[:, Z_VA:Z_VA + A_KV_W]
    for h in range(B_HEADS):
        base = Z_QB + h * B_QK_PAD
        qb_o[h, :, :B_NOPE] = (z[:, base:base + B_NOPE] * (B_SCALE * LOG2E)).astype(BF16)
        qb_o[h, :, B_NOPE:] = (rope64(z[:, base + B_NOPE:base + B_QK_PAD]) * (B_SCALE * LOG2E)).astype(BF16)

    ckv = _rms(z[:, Z_CKV:Z_CKV + KV_RANK]) * kvn_ref[...]
    kr = z[:, Z_KR:Z_KR + LANES]
    if prompt:
        sc_o[...] = ckv
        sr_o[...] = kr[:, :B_ROPE]
    krb = rope64(kr).astype(BF16)
    ckvb = ckv.astype(BF16)
    kn_all = jnp.dot(ckvb, wkv_ref[...], preferred_element_type=F32)
    for h in range(B_HEADS):
        kb_o[h, :, :B_NOPE] = kn_all[:, h * B_NOPE:(h + 1) * B_NOPE].astype(BF16)
        kb_o[h, :, B_NOPE:] = krb
    vbt_o[...] = lax.dot_general(wkvt_ref[...], ckvb, _NT_DIMS, preferred_element_type=F32).astype(BF16)


def _attn_project(x, mods, row0, layer, attn_layer, g, w, qn, kn, kvn, rope=None):
    nb, s, d = x.shape
    tm = TM_PROJ
    prompt = rope is None

    def row(wd):
        return pl.BlockSpec((None, tm, wd), lambda b, i: (b, i, 0))

    def col(wd):
        return pl.BlockSpec((None, wd, tm), lambda b, i: (b, 0, i))

    def heads(n, wd):
        return pl.BlockSpec((None, n, tm, wd), lambda b, i: (b, 0, i, 0))

    in_specs = [
        row(d),
        _mod_spec(layer, row0),
        _layer_row(d, layer),
        _layer_resident((d, Z_W), attn_layer),
        _layer_resident((A_KV_W, d), attn_layer),
        _layer_resident((KV_RANK, B_HEADS * B_NOPE), attn_layer),
        _layer_resident((B_HEADS * B_V, KV_RANK), attn_layer),
        _layer_row(HEAD_DIM, attn_layer),
        _layer_row(HEAD_DIM, attn_layer),
        _layer_row(KV_RANK, attn_layer),
    ]
    args = [x, mods, g, w.main, w.va_t, w.kv_k, w.kv_vt, qn, kn, kvn]
    out_specs = [heads(A_HEADS, HEAD_DIM), row(A_KV_W), col(A_KV_W), heads(B_HEADS, B_QK_PAD),
                 heads(B_HEADS, B_QK_PAD), col(B_HEADS * B_V)]
    out_shape = [jax.ShapeDtypeStruct(shp, BF16) for shp in (
        (nb, A_HEADS, s, HEAD_DIM), (nb, s, A_KV_W), (nb, A_KV_W, s), (nb, B_HEADS, s, B_QK_PAD),
        (nb, B_HEADS, s, B_QK_PAD), (nb, B_HEADS * B_V, s))]
    if prompt:
        for wd in (A_KV_W, A_KV_W, KV_RANK, B_ROPE):
            out_specs.append(row(wd))
            out_shape.append(jax.ShapeDtypeStruct((nb, s, wd), F32))
    else:
        in_specs += [pl.BlockSpec((tm, LANES), lambda b, i: (i, 0))] * 4
        args += list(rope)
    return pl.pallas_call(
        functools.partial(_proj_kernel, prompt=prompt),
        grid=(nb, s // tm),
        in_specs=in_specs,
        out_specs=out_specs,
        out_shape=out_shape,
        compiler_params=_cparams(("parallel", "parallel"), 52),
        name="attn_proj_prompt" if prompt else "attn_proj_latent",
    )(*args)


def _ctx_kv_kernel(ckv_ref, kr_ref, wkv_ref, wkvt_ref, kb_o, vbt_o):
    ckvb = ckv_ref[...].astype(BF16)
    kn_all = jnp.dot(ckvb, wkv_ref[...], preferred_element_type=F32)
    krb = kr_ref[...].astype(BF16)
    zero = jnp.zeros((krb.shape[0], B_QK_PAD - B_NOPE - B_ROPE), BF16)
    for h in range(B_HEADS):
        kb_o[h, :, :B_NOPE] = kn_all[:, h * B_NOPE:(h + 1) * B_NOPE].astype(BF16)
        kb_o[h, :, B_NOPE:B_NOPE + B_ROPE] = krb
        kb_o[h, :, B_NOPE + B_ROPE:] = zero
    vbt_o[...] = lax.dot_general(wkvt_ref[...], ckvb, _NT_DIMS, preferred_element_type=F32).astype(BF16)


def _ctx_kv_expand(cache_ckv, cache_kr, attn_layer, w):
    nb, _, p, _ = cache_ckv.shape
    return pl.pallas_call(
        _ctx_kv_kernel,
        grid=(nb,),
        in_specs=[
            pl.BlockSpec((None, None, p, KV_RANK), lambda b: (b, attn_layer, 0, 0)),
            pl.BlockSpec((None, None, p, B_ROPE), lambda b: (b, attn_layer, 0, 0)),
            pl.BlockSpec((None, KV_RANK, B_HEADS * B_NOPE), lambda b: (attn_layer, 0, 0)),
            pl.BlockSpec((None, B_HEADS * B_V, KV_RANK), lambda b: (attn_layer, 0, 0)),
        ],
        out_specs=[
            pl.BlockSpec((None, B_HEADS, p, B_QK_PAD), lambda b: (b, 0, 0, 0)),
            pl.BlockSpec((None, B_HEADS * B_V, p), lambda b: (b, 0, 0)),
        ],
        out_shape=[
            jax.ShapeDtypeStruct((nb, B_HEADS, p, B_QK_PAD), BF16),
            jax.ShapeDtypeStruct((nb, B_HEADS * B_V, p), BF16),
        ],
        compiler_params=_cparams(("parallel",), 32),
        name="ctx_kv_expand",
    )(cache_ckv, cache_kr, w.kv_k, w.kv_vt)


KV_LANES = "lanes"
KV_HEADS = "heads"
KV_CACHE = "cache"


def _attn_kernel(*refs, seg_len, seg_layout, hb, group, dq, dv, tq, rolled):
    nseg = len(seg_len)
    q_ref = refs[0]
    kv_refs = refs[1:1 + 2 * nseg]
    o_ref, s_ref, m_ref = refs[1 + 2 * nseg:]
    n_sub = q_ref.shape[1] // tq
    n_items = hb * n_sub
    seg_off = [sum(seg_len[:i]) for i in range(nseg)]
    dynamic_kv = all(lay == KV_HEADS for lay in seg_layout)

    def split(i):
        if isinstance(i, int):
            return i // n_sub, (i % n_sub) * tq
        shift = n_sub.bit_length() - 1
        return lax.shift_right_logical(i, shift), pl.multiple_of((i & (n_sub - 1)) * tq, tq)

    def kv_head(h):
        if isinstance(h, int):
            return h // group
        return lax.shift_right_logical(h, group.bit_length() - 1) if dynamic_kv else 0

    def keys(s, kvh):
        ref = kv_refs[2 * s]
        k = ref[kvh] if seg_layout[s] == KV_HEADS else ref[:, kvh * dq:(kvh + 1) * dq]
        return k.astype(BF16)

    def values_t(s, kvh):
        ref = kv_refs[2 * s + 1]
        if seg_layout[s] == KV_HEADS:
            vt = ref[kvh]
        elif seg_layout[s] == KV_LANES:
            vt = ref[kvh * dv:(kvh + 1) * dv, :]
        else:
            vt = ref[:, kvh * dv:(kvh + 1) * dv].T
        return vt.astype(BF16)

    def score(i, slot):
        h, r = split(i)
        kvh = kv_head(h)
        q = q_ref[h, pl.ds(r, tq), :]
        m = None
        for s in range(nseg):
            st = lax.dot_general(keys(s, kvh), q, _NT_DIMS, preferred_element_type=F32)
            s_ref[slot, seg_off[s]:seg_off[s] + seg_len[s], :] = st
            ms = jnp.max(st, axis=0, keepdims=True)
            m = ms if m is None else jnp.maximum(m, ms)
        m_ref[slot] = m

    def finish(i, slot):
        h, r = split(i)
        kvh = kv_head(h)
        m = m_ref[slot]
        denom = None
        acc = None
        for s in range(nseg):
            p = jnp.exp2(s_ref[slot, seg_off[s]:seg_off[s] + seg_len[s], :] - m)
            ls = jnp.sum(p, axis=0, keepdims=True)
            os_ = jnp.dot(values_t(s, kvh), p.astype(BF16), preferred_element_type=F32)
            denom = ls if denom is None else denom + ls
            acc = os_ if acc is None else acc + os_
        o_ref[h, pl.ds(r, tq), :] = (acc / denom).T.astype(o_ref.dtype)

    if rolled:
        assert dynamic_kv or hb == group
        assert n_items % 2 == 0 and n_items >= 4 and n_sub & (n_sub - 1) == 0 and group & (group - 1) == 0
        score(0, 0)

        def body(k, carry):
            i0 = 2 * k
            score(i0 + 1, 1)
            finish(i0, 0)
            score(i0 + 2, 0)
            finish(i0 + 1, 1)
            return carry

        lax.fori_loop(0, n_items // 2 - 1, body, 0)
        score(n_items - 1, 1)
        finish(n_items - 2, 0)
        finish(n_items - 1, 1)
    else:
        score(0, 0)
        for i in range(n_items):
            if i + 1 < n_items:
                score(i + 1, (i + 1) % 2)
            finish(i, i % 2)


def _attention(q, segs, *, seq_len, hb, group, dq, tqb, tq, rolled, seg_layer=None):
    nbq, n_heads, s_tot, _ = q.shape
    dv = HEAD_DIM
    per = s_tot // seq_len
    n_qb = seq_len // tqb
    nkv = hb // group
    in_specs = [pl.BlockSpec((None, hb, tqb, dq), lambda g, h, t: (g // per, h, (g % per) * n_qb + t, 0))]
    args = [q]
    seg_lens = []
    for k, v, layout in segs:
        if layout == KV_CACHE:
            sk = k.shape[2]
            in_specs.append(pl.BlockSpec((None, None, sk, nkv * dq), lambda g, h, t: (g, seg_layer, 0, h)))
            in_specs.append(pl.BlockSpec((None, None, sk, nkv * dv), lambda g, h, t: (g, seg_layer, 0, h)))
        elif layout == KV_LANES:
            sk = k.shape[1] // per
            in_specs.append(pl.BlockSpec((None, sk, nkv * dq), lambda g, h, t: (g // per, g % per, h)))
            in_specs.append(pl.BlockSpec((None, nkv * dv, sk), lambda g, h, t: (g // per, h, g % per)))
        else:
            sk = k.shape[2] // per
            in_specs.append(pl.BlockSpec((None, nkv, sk, dq), lambda g, h, t: (g // per, h, g % per, 0)))
            in_specs.append(pl.BlockSpec((None, nkv, dv, sk), lambda g, h, t: (g // per, h, 0, g % per)))
        seg_lens.append(sk)
        args += [k, v]
    return pl.pallas_call(
        functools.partial(_attn_kernel, seg_len=tuple(seg_lens), seg_layout=tuple(s[2] for s in segs), hb=hb,
                          group=group, dq=dq, dv=dv, tq=tq, rolled=rolled),
        grid=(nbq * per, n_heads // hb, n_qb),
        in_specs=in_specs,
        out_specs=pl.BlockSpec((None, hb, tqb, dv), lambda g, h, t: (g // per, h, (g % per) * n_qb + t, 0)),
        out_shape=jax.ShapeDtypeStruct((nbq, n_heads, s_tot, dv), BF16),
        scratch_shapes=[pltpu.VMEM((2, sum(seg_lens), tq), F32), pltpu.VMEM((2, 1, tq), F32)],
        compiler_params=_cparams(("parallel", "parallel", "arbitrary"), 56),
        name="attention",
    )(*args)


def _close_mixer(x, y, mod_ref, gf_ref, o_ref, xb_o, rows=slice(None)):
    x_new = x + mod_ref[2:3, :] * y
    o_ref[rows, :] = x_new
    xb_o[rows, :] = _modulate(x_new, gf_ref[...], mod_ref[3:4, :], mod_ref[4:5, :]).astype(BF16)


def _attn_out_kernel(x_ref, mod_ref, gf_ref, oa_ref, ob_ref, w_ref, o_ref, xb_o):
    half = x_ref.shape[0] // 2
    for r in range(2):
        rows = slice(r * half, (r + 1) * half)
        cat = jnp.concatenate([oa_ref[h, rows, :] for h in range(A_HEADS)]
                              + [ob_ref[h, rows, :] for h in range(B_HEADS)], axis=-1)
        y = jnp.dot(cat, w_ref[...], preferred_element_type=F32)
        _close_mixer(x_ref[rows, :], y, mod_ref, gf_ref, o_ref, xb_o, rows)


def _attn_out(x, mods, row0, layer, attn_layer, g_ffn, oa, ob, w_out):
    nb, s, d = x.shape
    tm = TM_ATTN_OUT
    tile = pl.BlockSpec((None, tm, d), lambda b, i: (b, i, 0))
    return pl.pallas_call(
        _attn_out_kernel,
        grid=(nb, s // tm),
        in_specs=[
            tile,
            _mod_spec(layer, row0),
            _layer_row(d, layer),
            pl.BlockSpec((None, A_HEADS, tm, HEAD_DIM), lambda b, i: (b, 0, i, 0)),
            pl.BlockSpec((None, B_HEADS, tm, B_V), lambda b, i: (b, 0, i, 0)),
            _layer_resident((A_Q_W + B_HEADS * B_V, d), attn_layer),
        ],
        out_specs=[tile, tile],
        out_shape=[jax.ShapeDtypeStruct((nb, s, d), F32), jax.ShapeDtypeStruct((nb, s, d), BF16)],
        compiler_params=_cparams(("parallel", "parallel"), 44),
        name="attn_out",
    )(x, mods, g_ffn, oa, ob, w_out)


def _gelu_tanh(x):
    return 0.5 * x * (1.0 + jnp.tanh(0.7978845608028654 * (x + 0.044715 * (x * x * x))))


def _cmlp_kernel(x_ref, mod_ref, g_ref, gf_ref, win_ref, vn_ref, ws_ref, bias_ref, wout_ref, o_ref, xb_o, t_ref):
    x = x_ref[...]
    xm = _modulate(x, g_ref[...], mod_ref[0:1, :], mod_ref[1:2, :])
    z = _gelu_tanh(jnp.dot(xm.astype(BF16), win_ref[...], preferred_element_type=F32))
    u = z[:, :D_MODEL]
    vb = (_rms(z[:, D_MODEL:]) * vn_ref[...]).astype(BF16)
    for n in range(x.shape[0] // CHUNK):
        rows = slice(n * CHUNK, (n + 1) * CHUNK)
        for g in range(C_GROUPS):
            cols = slice(g * C_GROUP_W, (g + 1) * C_GROUP_W)
            sv = jnp.dot(ws_ref[g], vb[rows, cols], preferred_element_type=F32) + bias_ref[:, cols]
            t_ref[rows, cols] = (u[rows, cols] * sv).astype(BF16)
    y = jnp.dot(t_ref[...], wout_ref[...], preferred_element_type=F32)
    _close_mixer(x, y, mod_ref, gf_ref, o_ref, xb_o)


def _chunk_mlp(x, mods, row0, layer, c_layer, g, g_ffn, w_in, v_norm, w_s, bias, w_out):
    nb, s, d = x.shape
    tm = TM_CMLP
    tile = pl.BlockSpec((None, tm, d), lambda b, i: (b, i, 0))
    return pl.pallas_call(
        _cmlp_kernel,
        grid=(nb, s // tm),
        in_specs=[
            tile,
            _mod_spec(layer, row0),
            _layer_row(d, layer),
            _layer_row(d, layer),
            _layer_resident((d, 2 * d), c_layer),
            _layer_row(d, c_layer),
            _layer_resident((C_GROUPS, CHUNK, CHUNK), c_layer),
            _layer_resident((CHUNK, d), c_layer),
            _layer_resident((d, d), c_layer),
        ],
        out_specs=[tile, tile],
        out_shape=[jax.ShapeDtypeStruct((nb, s, d), F32), jax.ShapeDtypeStruct((nb, s, d), BF16)],
        scratch_shapes=[pltpu.VMEM((tm, d), BF16)],
        compiler_params=_cparams(("parallel", "parallel"), 54),
        name="chunk_mlp",
    )(x, mods, g, g_ffn, w_in, v_norm, w_s, bias, w_out)


def _rope_tables(n_tokens):
    t = np.arange(n_tokens)
    row = (t // GRID_W).astype(np.float32)
    col = (t % GRID_W).astype(np.float32)

    def axis_tables(pos, half):
        inv = (1.0 / (ROPE_BASE ** (np.arange(0, half, 2, dtype=np.float32) / half))).astype(np.float32)
        ang = (pos[:, None] * inv).astype(np.float32).astype(np.float64)
        c, s = np.cos(ang), np.sin(ang)
        return np.concatenate([c, c], axis=1), np.concatenate([-s, s], axis=1)

    def tables(dim):
        cr, sr = axis_tables(row, dim // 2)
        cc, sc = axis_tables(col, dim // 2)
        return np.concatenate([cr, cc], axis=1), np.concatenate([sr, sc], axis=1)

    c128, s128 = tables(HEAD_DIM)
    c64, s64 = tables(B_ROPE)
    pad = np.zeros((n_tokens, LANES - B_ROPE))
    c64 = np.concatenate([c64, pad], axis=1)
    s64 = np.concatenate([s64, pad], axis=1)
    return tuple(jnp.asarray(a, F32) for a in (c128, s128, c64, s64))


def kernel(x_prompt, x_sample, c, cache_gqa_k, cache_gqa_v, cache_mla_ckv, cache_mla_krope, c_ctx, ada_w, ada_b, norm_mix, norm_ffn, ffn_gate, ffn_up, ffn_down, attn_w_in, attn_q_norm, attn_k_norm, attn_kv_norm, attn_w_kv_up, attn_w_out, cmlp_w_in, cmlp_v_norm, cmlp_w_s, cmlp_b_s, cmlp_w_out, final_norm):
    n_p, s_p, d = x_prompt.shape
    n_s, s_s, _ = x_sample.shape
    past = cache_gqa_k.shape[2]

    cond = jnp.concatenate([c_ctx[None, :], c, jnp.zeros((MOD_ROWS - 1 - n_s, d), F32)], axis=0)
    mods = _ada_mod(cond, ada_w, ada_b)

    def rows(a):
        return a.reshape(a.shape[0], 1, a.shape[1])

    def heads_t(vt):
        return vt.reshape(vt.shape[0], B_HEADS, B_V, vt.shape[2])

    aw = _layout_attn_weights(attn_w_in, attn_w_kv_up, attn_w_out)
    g_mix, g_ffn = rows(norm_mix), rows(norm_ffn)
    qn, kn, kvn = rows(attn_q_norm), rows(attn_k_norm), rows(attn_kv_norm)
    wg, wu, wd = ffn_gate.astype(BF16), ffn_up.astype(BF16), ffn_down.astype(BF16)
    cw_in, cw_out, cw_s = cmlp_w_in.astype(BF16), cmlp_w_out.astype(BF16), cmlp_w_s.astype(BF16)
    c_vn = rows(cmlp_v_norm)
    c_bias = jnp.repeat(jnp.swapaxes(cmlp_b_s, 1, 2), C_GROUP_W, axis=2)
    final_g = final_norm.reshape(1, d)

    xp = x_prompt.reshape(1, n_p * s_p, d)
    xs = x_sample
    rope = _rope_tables(s_s)
    cache_k = cache_gqa_k.reshape(n_s, -1, past, A_KV_W)
    cache_v = cache_gqa_v.reshape(n_s, -1, past, A_KV_W)

    states = ([], [], [], [])
    for i in range(DEPTH):
        j = i // 2
        if i % 2 == 0:
            qa, ka, vat, qb, kb, vbt, st_k, st_v, st_c, st_r = _attn_project(
                xp, mods, 0, i, j, g_mix, aw, qn, kn, kvn)
            for lst, st in zip(states, (st_k, st_v, st_c, st_r)):
                lst.append(st)
            oa = _attention(qa, [(ka, vat, KV_LANES)], seq_len=s_p, hb=A_HEADS, group=A_GROUP, dq=HEAD_DIM,
                            tqb=s_p, tq=s_p, rolled=False)
            ob = _attention(qb, [(kb, heads_t(vbt), KV_HEADS)], seq_len=s_p, hb=B_HEADS, group=1, dq=B_QK_PAD,
                            tqb=s_p, tq=s_p, rolled=False)
            xp, xpb = _attn_out(xp, mods, 0, i, j, g_ffn, oa, ob, aw.out)

            qa, ka, vat, qb, kb, vbt = _attn_project(xs, mods, 1, i, j, g_mix, aw, qn, kn, kvn, rope=rope)
            kb_ctx, vbt_ctx = _ctx_kv_expand(cache_mla_ckv, cache_mla_krope, j, aw)
            oa = _attention(qa, [(ka, vat, KV_LANES), (cache_k, cache_v, KV_CACHE)], seq_len=s_s, hb=A_GROUP,
                            group=A_GROUP, dq=HEAD_DIM, tqb=2048, tq=TQ_ATTN, rolled=True, seg_layer=j)
            ob = _attention(qb, [(kb, heads_t(vbt), KV_HEADS), (kb_ctx, heads_t(vbt_ctx), KV_HEADS)],
                            seq_len=s_s, hb=2, group=1, dq=B_QK_PAD, tqb=4096, tq=TQ_ATTN, rolled=True)
            xs, xsb = _attn_out(xs, mods, 1, i, j, g_ffn, oa, ob, aw.out)
        else:
            xp, xpb = _chunk_mlp(xp, mods, 0, i, j, g_mix, g_ffn, cw_in, c_vn, cw_s, c_bias, cw_out)
            xs, xsb = _chunk_mlp(xs, mods, 1, i, j, g_mix, g_ffn, cw_in, c_vn, cw_s, c_bias, cw_out)

        fin = final_g if i == DEPTH - 1 else None
        xp = _ffn(xp, xpb, mods, 0, i, wg, wu, wd, fin)
        xs = _ffn(xs, xsb, mods, 1, i, wg, wu, wd, fin)

    y_prompt = xp.reshape(n_p, s_p, d)
    st_k, st_v, st_c, st_r = (jnp.stack([a.reshape(n_p, s_p, a.shape[-1]) for a in lst], axis=1)
                              for lst in states)
    return (y_prompt, xs,
            st_k.reshape(n_p, -1, s_p, A_KV_HEADS, HEAD_DIM),
            st_v.reshape(n_p, -1, s_p, A_KV_HEADS, HEAD_DIM),
            st_c, st_r)
```

```python
import functools
from typing import NamedTuple

import numpy as np
import jax
import jax.numpy as jnp
from jax import lax
from jax.experimental import pallas as pl
from jax.experimental.pallas import tpu as pltpu

F32 = jnp.float32
BF16 = jnp.bfloat16

D_MODEL = 2048
DEPTH = 4
GRID_W = 64
ROPE_BASE = 10000.0
EPS = 1e-6
A_HEADS = 8
A_KV_HEADS = 2
A_GROUP = A_HEADS // A_KV_HEADS
HEAD_DIM = 128
A_SCALE = HEAD_DIM ** -0.5
B_HEADS = 8
B_NOPE = 128
B_ROPE = 64
B_V = 128
KV_RANK = 512
B_SCALE = (B_NOPE + B_ROPE) ** -0.5
B_QK_PAD = 256
A_Q_W = A_HEADS * HEAD_DIM
A_KV_W = A_KV_HEADS * HEAD_DIM
B_Q_W = B_HEADS * (B_NOPE + B_ROPE)
CHUNK = 128
C_GROUPS = 8
C_GROUP_W = D_MODEL // C_GROUPS
FFN_HIDDEN = 5632
LOG2E = 1.4426950408889634
LANES = 128

Z_QA = 0
Z_KA = Z_QA + A_Q_W
Z_VA = Z_KA + A_KV_W
Z_QB = Z_VA + A_KV_W
Z_CKV = Z_QB + B_HEADS * B_QK_PAD
Z_KR = Z_CKV + KV_RANK
Z_W = Z_KR + LANES

MOD_ROWS = 8
_NT_DIMS = (((1,), (1,)), ((), ()))

TM_FFN, TC_FFN = 1024, 512
TM_PROJ = 512
TM_ATTN_OUT = 512
TM_CMLP = 512
TN_ADA = 1024
TQ_ATTN = 256
ATTN_UNROLL = 4


def _cparams(sem, vmem_mib, flags=None):
    return pltpu.CompilerParams(dimension_semantics=sem, vmem_limit_bytes=vmem_mib << 20, flags=flags)


def _layer_resident(shape, layer):
    n = len(shape)
    return pl.BlockSpec((None,) + tuple(shape), lambda *_: (layer,) + (0,) * n, pipeline_mode=pl.Buffered(1))


def _layer_row(width, layer):
    return pl.BlockSpec((None, 1, width), lambda *_: (layer, 0, 0))


def _mod_spec(layer, row0):
    return pl.BlockSpec((None, None, 6, D_MODEL), lambda b, *_: (layer, row0 + b, 0, 0))


def _rms(x):
    return x * lax.rsqrt(jnp.mean(x * x, axis=-1, keepdims=True) + EPS)


def _modulate(x, g, shift, scale):
    return (_rms(x) * g) * (1.0 + scale) + shift


def _swap_halves(v, n):
    lane = lax.broadcasted_iota(jnp.int32, v.shape, 1)
    fwd = pltpu.roll(v, LANES - n, 1)
    bwd = pltpu.roll(v, n, 1)
    return jnp.where((lane & (2 * n - 1)) < n, fwd, bwd)


def _ada_kernel(cond_ref, w_ref, b_ref, o_ref):
    c = cond_ref[...]
    a = (c * jax.nn.sigmoid(c)).astype(BF16)
    o_ref[...] = jnp.dot(a, w_ref[...].astype(BF16), preferred_element_type=F32) + b_ref[...]


def _ada_mod(cond, ada_w, ada_b):
    n6 = 6 * D_MODEL
    out = pl.pallas_call(
        _ada_kernel,
        grid=(DEPTH, n6 // TN_ADA),
        in_specs=[
            pl.BlockSpec((MOD_ROWS, D_MODEL), lambda l, n: (0, 0)),
            pl.BlockSpec((None, D_MODEL, TN_ADA), lambda l, n: (l, 0, n)),
            pl.BlockSpec((None, 1, TN_ADA), lambda l, n: (l, 0, n)),
        ],
        out_specs=pl.BlockSpec((None, MOD_ROWS, TN_ADA), lambda l, n: (l, 0, n)),
        out_shape=jax.ShapeDtypeStruct((DEPTH, MOD_ROWS, n6), F32),
        compiler_params=_cparams(("arbitrary", "arbitrary"), 40),
        name="ada_mod",
    )(cond, ada_w, ada_b.reshape(DEPTH, 1, n6))
    return out.reshape(DEPTH, MOD_ROWS, 6, D_MODEL)


def _ffn_kernel(*refs, n_chunks, final):
    if final:
        x_hbm, xb_ref, mod_ref, wg_ref, wu_ref, wd_ref, fn_ref, o_ref, x_buf, x_sem = refs
    else:
        x_hbm, xb_ref, mod_ref, wg_ref, wu_ref, wd_ref, o_ref, x_buf, x_sem = refs
    b, i, j = pl.program_id(0), pl.program_id(1), pl.program_id(2)
    tm = x_buf.shape[0]
    x_copy = pltpu.make_async_copy(x_hbm.at[b, pl.ds(pl.multiple_of(i * tm, tm), tm), :], x_buf, x_sem)

    @pl.when(j == 0)
    def _():
        x_copy.start()
        o_ref[...] = jnp.zeros_like(o_ref)

    xb = xb_ref[...]
    gate = jnp.dot(xb, wg_ref[...], preferred_element_type=F32)
    up = jnp.dot(xb, wu_ref[...], preferred_element_type=F32)
    h = (gate * jax.nn.sigmoid(gate) * up).astype(BF16)
    o_ref[...] += jnp.dot(h, wd_ref[...], preferred_element_type=F32)

    @pl.when(j == n_chunks - 1)
    def _():
        x_copy.wait()
        y = x_buf[...] + mod_ref[5:6, :] * o_ref[...]
        if final:
            y = _rms(y) * fn_ref[...]
        o_ref[...] = y


def _ffn(x, xb, mods, row0, layer, wg, wu, wd, final_g=None):
    nb, s, d = x.shape
    tm, tc = TM_FFN, TC_FFN
    n_chunks = FFN_HIDDEN // tc
    final = final_g is not None
    in_specs = [
        pl.BlockSpec(memory_space=pl.ANY),
        pl.BlockSpec((None, tm, d), lambda b, i, j: (b, i, 0)),
        _mod_spec(layer, row0),
        pl.BlockSpec((None, d, tc), lambda b, i, j: (layer, 0, j)),
        pl.BlockSpec((None, d, tc), lambda b, i, j: (layer, 0, j)),
        pl.BlockSpec((None, tc, d), lambda b, i, j: (layer, j, 0)),
    ]
    args = [x, xb, mods, wg, wu, wd]
    if final:
        in_specs.append(pl.BlockSpec((1, d), lambda b, i, j: (0, 0)))
        args.append(final_g)
    return pl.pallas_call(
        functools.partial(_ffn_kernel, n_chunks=n_chunks, final=final),
        grid=(nb, s // tm, n_chunks),
        in_specs=in_specs,
        out_specs=pl.BlockSpec((None, tm, d), lambda b, i, j: (b, i, 0)),
        out_shape=jax.ShapeDtypeStruct((nb, s, d), F32),
        scratch_shapes=[pltpu.VMEM((tm, d), F32), pltpu.SemaphoreType.DMA(())],
        compiler_params=_cparams(("arbitrary", "arbitrary", "arbitrary"), 60),
        name="ffn_final" if final else "ffn",
    )(*args)


class _AttnWeights(NamedTuple):
    main: jax.Array
    va_t: jax.Array
    kv_k: jax.Array
    kv_vt: jax.Array
    out: jax.Array


def _layout_attn_weights(w_in, w_kv_up, w_out):
    n_l, d, _ = w_in.shape
    o = A_Q_W + 2 * A_KV_W
    qb = w_in[:, :, o:o + B_Q_W].reshape(n_l, d, B_HEADS, B_NOPE + B_ROPE)
    qb = jnp.pad(qb, ((0, 0), (0, 0), (0, 0), (0, B_QK_PAD - B_NOPE - B_ROPE)))
    ckv = w_in[:, :, o + B_Q_W:o + B_Q_W + KV_RANK]
    kr = jnp.pad(w_in[:, :, o + B_Q_W + KV_RANK:], ((0, 0), (0, 0), (0, LANES - B_ROPE)))
    main = jnp.concatenate([w_in[:, :, :o], qb.reshape(n_l, d, B_HEADS * B_QK_PAD), ckv, kr], axis=2)
    kv4 = w_kv_up.reshape(n_l, KV_RANK, B_HEADS, B_NOPE + B_V)
    return _AttnWeights(
        main=main.astype(BF16),
        va_t=jnp.swapaxes(w_in[:, :, Z_VA:Z_VA + A_KV_W], 1, 2).astype(BF16),
        kv_k=kv4[..., :B_NOPE].reshape(n_l, KV_RANK, -1).astype(BF16),
        kv_vt=jnp.swapaxes(kv4[..., B_NOPE:].reshape(n_l, KV_RANK, -1), 1, 2).astype(BF16),
        out=w_out.astype(BF16))


def _proj_kernel(*refs, prompt):
    if prompt:
        (x_ref, mod_ref, g_ref, w_ref, wvat_ref, wkv_ref, wkvt_ref, qn_ref, kn_ref, kvn_ref,
         qa_o, ka_o, vat_o, qb_o, kb_o, vbt_o, sk_o, sv_o, sc_o, sr_o) = refs
    else:
        (x_ref, mod_ref, g_ref, w_ref, wvat_ref, wkv_ref, wkvt_ref, qn_ref, kn_ref, kvn_ref,
         c128_ref, s128_ref, c64_ref, s64_ref,
         qa_o, ka_o, vat_o, qb_o, kb_o, vbt_o) = refs

    half = x_ref.shape[0] // 2
    for r in range(2):
        rows = slice(r * half, (r + 1) * half)
        if not prompt:
            c128, s128 = c128_ref[rows, :], s128_ref[rows, :]
            c64, s64 = c64_ref[rows, :], s64_ref[rows, :]

        def rope128(v):
            return v if prompt else v * c128 + _swap_halves(v, 32) * s128

        def rope64(v):
            return v if prompt else v * c64 + _swap_halves(v, 16) * s64

        xb = _modulate(x_ref[rows, :], g_ref[...], mod_ref[0:1, :], mod_ref[1:2, :]).astype(BF16)
        z = jnp.dot(xb, w_ref[...], preferred_element_type=F32)
        vat_o[:, rows] = lax.dot_general(wvat_ref[...], xb, _NT_DIMS, preferred_element_type=F32).astype(BF16)

        for h in range(A_HEADS):
            q = _rms(z[:, Z_QA + h * HEAD_DIM:Z_QA + (h + 1) * HEAD_DIM]) * qn_ref[...]
            qa_o[h, rows, :] = (rope128(q) * (A_SCALE * LOG2E)).astype(BF16)
        for h in range(A_KV_HEADS):
            sl = slice(h * HEAD_DIM, (h + 1) * HEAD_DIM)
            k = _rms(z[:, Z_KA + h * HEAD_DIM:Z_KA + (h + 1) * HEAD_DIM]) * kn_ref[...]
            if prompt:
                sk_o[rows, sl] = k
            ka_o[rows, sl] = rope128(k).astype(BF16)
        if prompt:
            sv_o[rows, :] = z[:, Z_VA:Z_VA + A_KV_W]
        for h in range(B_HEADS):
            base = Z_QB + h * B_QK_PAD
            qb_o[h, rows, :B_NOPE] = (z[:, base:base + B_NOPE] * (B_SCALE * LOG2E)).astype(BF16)
            qb_o[h, rows, B_NOPE:] = (
                rope64(z[:, base + B_NOPE:base + B_QK_PAD]) * (B_SCALE * LOG2E)).astype(BF16)

        ckv = _rms(z[:, Z_CKV:Z_CKV + KV_RANK]) * kvn_ref[...]
        kr = z[:, Z_KR:Z_KR + LANES]
        if prompt:
            sc_o[rows, :] = ckv
            sr_o[rows, :] = kr[:, :B_ROPE]
        krb = rope64(kr).astype(BF16)
        ckvb = ckv.astype(BF16)
        kn_all = jnp.dot(ckvb, wkv_ref[...], preferred_element_type=F32)
        for h in range(B_HEADS):
            kb_o[h, rows, :B_NOPE] = kn_all[:, h * B_NOPE:(h + 1) * B_NOPE].astype(BF16)
            kb_o[h, rows, B_NOPE:] = krb
        vbt_o[:, rows] = lax.dot_general(wkvt_ref[...], ckvb, _NT_DIMS, preferred_element_type=F32).astype(BF16)


def _attn_project(x, mods, row0, layer, attn_layer, g, w, qn, kn, kvn, rope=None):
    nb, s, d = x.shape
    tm = TM_PROJ
    prompt = rope is None

    def row(wd):
        return pl.BlockSpec((None, tm, wd), lambda b, i: (b, i, 0))

    def col(wd):
        return pl.BlockSpec((None, wd, tm), lambda b, i: (b, 0, i))

    def heads(n, wd):
        return pl.BlockSpec((None, n, tm, wd), lambda b, i: (b, 0, i, 0))

    in_specs = [
        row(d),
        _mod_spec(layer, row0),
        _layer_row(d, layer),
        _layer_resident((d, Z_W), attn_layer),
        _layer_resident((A_KV_W, d), attn_layer),
        _layer_resident((KV_RANK, B_HEADS * B_NOPE), attn_layer),
        _layer_resident((B_HEADS * B_V, KV_RANK), attn_layer),
        _layer_row(HEAD_DIM, attn_layer),
        _layer_row(HEAD_DIM, attn_layer),
        _layer_row(KV_RANK, attn_layer),
    ]
    args = [x, mods, g, w.main, w.va_t, w.kv_k, w.kv_vt, qn, kn, kvn]
    out_specs = [heads(A_HEADS, HEAD_DIM), row(A_KV_W), col(A_KV_W), heads(B_HEADS, B_QK_PAD),
                 heads(B_HEADS, B_QK_PAD), col(B_HEADS * B_V)]
    out_shape = [jax.ShapeDtypeStruct(shp, BF16) for shp in (
        (nb, A_HEADS, s, HEAD_DIM), (nb, s, A_KV_W), (nb, A_KV_W, s), (nb, B_HEADS, s, B_QK_PAD),
        (nb, B_HEADS, s, B_QK_PAD), (nb, B_HEADS * B_V, s))]
    if prompt:
        for wd in (A_KV_W, A_KV_W, KV_RANK, B_ROPE):
            out_specs.append(row(wd))
            out_shape.append(jax.ShapeDtypeStruct((nb, s, wd), F32))
    else:
        in_specs += [pl.BlockSpec((tm, LANES), lambda b, i: (i, 0))] * 4
        args += list(rope)
    return pl.pallas_call(
        functools.partial(_proj_kernel, prompt=prompt),
        grid=(nb, s // tm),
        in_specs=in_specs,
        out_specs=out_specs,
        out_shape=out_shape,
        compiler_params=_cparams(("parallel", "parallel"), 60),
        name="attn_proj_prompt" if prompt else "attn_proj_latent",
    )(*args)


def _ctx_kv_kernel(ckv_ref, kr_ref, wkv_ref, wkvt_ref, kb_o, vbt_o):
    ckvb = ckv_ref[...].astype(BF16)
    kn_all = jnp.dot(ckvb, wkv_ref[...], preferred_element_type=F32)
    krb = kr_ref[...].astype(BF16)
    zero = jnp.zeros((krb.shape[0], B_QK_PAD - B_NOPE - B_ROPE), BF16)
    for h in range(B_HEADS):
        kb_o[h, :, :B_NOPE] = kn_all[:, h * B_NOPE:(h + 1) * B_NOPE].astype(BF16)
        kb_o[h, :, B_NOPE:B_NOPE + B_ROPE] = krb
        kb_o[h, :, B_NOPE + B_ROPE:] = zero
    vbt_o[...] = lax.dot_general(wkvt_ref[...], ckvb, _NT_DIMS, preferred_element_type=F32).astype(BF16)


def _ctx_kv_expand(cache_ckv, cache_kr, attn_layer, w):
    nb, _, p, _ = cache_ckv.shape
    return pl.pallas_call(
        _ctx_kv_kernel,
        grid=(nb,),
        in_specs=[
            pl.BlockSpec((None, None, p, KV_RANK), lambda b: (b, attn_layer, 0, 0)),
            pl.BlockSpec((None, None, p, B_ROPE), lambda b: (b, attn_layer, 0, 0)),
            pl.BlockSpec((None, KV_RANK, B_HEADS * B_NOPE), lambda b: (attn_layer, 0, 0)),
            pl.BlockSpec((None, B_HEADS * B_V, KV_RANK), lambda b: (attn_layer, 0, 0)),
        ],
        out_specs=[
            pl.BlockSpec((None, B_HEADS, p, B_QK_PAD), lambda b: (b, 0, 0, 0)),
            pl.BlockSpec((None, B_HEADS * B_V, p), lambda b: (b, 0, 0)),
        ],
        out_shape=[
            jax.ShapeDtypeStruct((nb, B_HEADS, p, B_QK_PAD), BF16),
            jax.ShapeDtypeStruct((nb, B_HEADS * B_V, p), BF16),
        ],
        compiler_params=_cparams(("parallel",), 32),
        name="ctx_kv_expand",
    )(cache_ckv, cache_kr, w.kv_k, w.kv_vt)


KV_LANES = "lanes"
KV_HEADS = "heads"
KV_CACHE = "cache"


def _attn_kernel(*refs, seg_len, seg_layout, hb, group, dq, dv, tq, rolled):
    nseg = len(seg_len)
    q_ref = refs[0]
    kv_refs = refs[1:1 + 2 * nseg]
    o_ref, s_ref, m_ref = refs[1 + 2 * nseg:]
    n_sub = q_ref.shape[1] // tq
    n_items = hb * n_sub
    seg_off = [sum(seg_len[:i]) for i in range(nseg)]
    dynamic_kv = all(lay == KV_HEADS for lay in seg_layout)

    def split(i):
        if isinstance(i, int):
            return i // n_sub, (i % n_sub) * tq
        shift = n_sub.bit_length() - 1
        return lax.shift_right_logical(i, shift), pl.multiple_of((i & (n_sub - 1)) * tq, tq)

    def kv_head(h):
        if isinstance(h, int):
            return h // group
        return lax.shift_right_logical(h, group.bit_length() - 1) if dynamic_kv else 0

    def keys(s, kvh):
        ref = kv_refs[2 * s]
        k = ref[kvh] if seg_layout[s] == KV_HEADS else ref[:, kvh * dq:(kvh + 1) * dq]
        return k.astype(BF16)

    def values_t(s, kvh):
        ref = kv_refs[2 * s + 1]
        if seg_layout[s] == KV_HEADS:
            vt = ref[kvh]
        elif seg_layout[s] == KV_LANES:
            vt = ref[kvh * dv:(kvh + 1) * dv, :]
        else:
            vt = ref[:, kvh * dv:(kvh + 1) * dv].T
        return vt.astype(BF16)

    def score(i, slot):
        h, r = split(i)
        kvh = kv_head(h)
        q = q_ref[h, pl.ds(r, tq), :]
        m = None
        for s in range(nseg):
            st = lax.dot_general(keys(s, kvh), q, _NT_DIMS, preferred_element_type=F32)
            s_ref[slot, seg_off[s]:seg_off[s] + seg_len[s], :] = st
            ms = jnp.max(st, axis=0, keepdims=True)
            m = ms if m is None else jnp.maximum(m, ms)
        m_ref[slot] = m

    def finish(i, slot):
        h, r = split(i)
        kvh = kv_head(h)
        m = m_ref[slot]
        denom = None
        acc = None
        for s in range(nseg):
            p = jnp.exp2(s_ref[slot, seg_off[s]:seg_off[s] + seg_len[s], :] - m)
            ls = jnp.sum(p, axis=0, keepdims=True)
            os_ = jnp.dot(values_t(s, kvh), p.astype(BF16), preferred_element_type=F32)
            denom = ls if denom is None else denom + ls
            acc = os_ if acc is None else acc + os_
        o_ref[h, pl.ds(r, tq), :] = (acc / denom).T.astype(o_ref.dtype)

    if rolled:
        assert dynamic_kv or hb == group
        unroll = ATTN_UNROLL
        assert unroll % 2 == 0 and n_items % unroll == 0 and n_items >= 2 * unroll
        assert n_sub & (n_sub - 1) == 0 and group & (group - 1) == 0
        score(0, 0)

        def body(k, carry):
            for u in range(unroll):
                score(unroll * k + u + 1, (u + 1) % 2)
                finish(unroll * k + u, u % 2)
            return carry

        lax.fori_loop(0, n_items // unroll - 1, body, 0)
        for u in range(unroll - 1):
            score(n_items - unroll + u + 1, (u + 1) % 2)
            finish(n_items - unroll + u, u % 2)
        finish(n_items - 1, (unroll - 1) % 2)
    else:
        score(0, 0)
        for i in range(n_items):
            if i + 1 < n_items:
                score(i + 1, (i + 1) % 2)
            finish(i, i % 2)


def _attention(q, segs, *, seq_len, hb, group, dq, tqb, tq, rolled, seg_layer=None):
    nbq, n_heads, s_tot, _ = q.shape
    dv = HEAD_DIM
    per = s_tot // seq_len
    n_qb = seq_len // tqb
    nkv = hb // group
    in_specs = [pl.BlockSpec((None, hb, tqb, dq), lambda g, h, t: (g // per, h, (g % per) * n_qb + t, 0))]
    args = [q]
    seg_lens = []
    for k, v, layout in segs:
        if layout == KV_CACHE:
            sk = k.shape[2]
            in_specs.append(pl.BlockSpec((None, None, sk, nkv * dq), lambda g, h, t: (g, seg_layer, 0, h)))
            in_specs.append(pl.BlockSpec((None, None, sk, nkv * dv), lambda g, h, t: (g, seg_layer, 0, h)))
        elif layout == KV_LANES:
            sk = k.shape[1] // per
            in_specs.append(pl.BlockSpec((None, sk, nkv * dq), lambda g, h, t: (g // per, g % per, h)))
            in_specs.append(pl.BlockSpec((None, nkv * dv, sk), lambda g, h, t: (g // per, h, g % per)))
        else:
            sk = k.shape[2] // per
            in_specs.append(pl.BlockSpec((None, nkv, sk, dq), lambda g, h, t: (g // per, h, g % per, 0)))
            in_specs.append(pl.BlockSpec((None, nkv, dv, sk), lambda g, h, t: (g // per, h, 0, g % per)))
        seg_lens.append(sk)
        args += [k, v]
    return pl.pallas_call(
        functools.partial(_attn_kernel, seg_len=tuple(seg_lens), seg_layout=tuple(s[2] for s in segs), hb=hb,
                          group=group, dq=dq, dv=dv, tq=tq, rolled=rolled),
        grid=(nbq * per, n_heads // hb, n_qb),
        in_specs=in_specs,
        out_specs=pl.BlockSpec((None, hb, tqb, dv), lambda g, h, t: (g // per, h, (g % per) * n_qb + t, 0)),
        out_shape=jax.ShapeDtypeStruct((nbq, n_heads, s_tot, dv), BF16),
        scratch_shapes=[pltpu.VMEM((2, sum(seg_lens), tq), F32), pltpu.VMEM((2, 1, tq), F32)],
        compiler_params=_cparams(("parallel", "parallel", "arbitrary"), 56),
        name="attention",
    )(*args)


def _close_mixer(x, y, mod_ref, gf_ref, o_ref, xb_o, rows=slice(None)):
    x_new = x + mod_ref[2:3, :] * y
    o_ref[rows, :] = x_new
    xb_o[rows, :] = _modulate(x_new, gf_ref[...], mod_ref[3:4, :], mod_ref[4:5, :]).astype(BF16)


def _attn_out_kernel(x_ref, mod_ref, gf_ref, oa_ref, ob_ref, w_ref, o_ref, xb_o):
    half = x_ref.shape[0] // 2
    for r in range(2):
        rows = slice(r * half, (r + 1) * half)
        cat = jnp.concatenate([oa_ref[h, rows, :] for h in range(A_HEADS)]
                              + [ob_ref[h, rows, :] for h in range(B_HEADS)], axis=-1)
        y = jnp.dot(cat, w_ref[...], preferred_element_type=F32)
        _close_mixer(x_ref[rows, :], y, mod_ref, gf_ref, o_ref, xb_o, rows)


def _attn_out(x, mods, row0, layer, attn_layer, g_ffn, oa, ob, w_out):
    nb, s, d = x.shape
    tm = TM_ATTN_OUT
    tile = pl.BlockSpec((None, tm, d), lambda b, i: (b, i, 0))
    return pl.pallas_call(
        _attn_out_kernel,
        grid=(nb, s // tm),
        in_specs=[
            tile,
            _mod_spec(layer, row0),
            _layer_row(d, layer),
            pl.BlockSpec((None, A_HEADS, tm, HEAD_DIM), lambda b, i: (b, 0, i, 0)),
            pl.BlockSpec((None, B_HEADS, tm, B_V), lambda b, i: (b, 0, i, 0)),
            _layer_resident((A_Q_W + B_HEADS * B_V, d), attn_layer),
        ],
        out_specs=[tile, tile],
        out_shape=[jax.ShapeDtypeStruct((nb, s, d), F32), jax.ShapeDtypeStruct((nb, s, d), BF16)],
        compiler_params=_cparams(("parallel", "parallel"), 44),
        name="attn_out",
    )(x, mods, g_ffn, oa, ob, w_out)


def _gelu_tanh(x):
    return 0.5 * x * (1.0 + jnp.tanh(0.7978845608028654 * (x + 0.044715 * (x * x * x))))


def _cmlp_kernel(x_ref, mod_ref, g_ref, gf_ref, win_ref, vn_ref, ws_ref, bias_ref, wout_ref, o_ref, xb_o, t_ref):
    half = x_ref.shape[0] // 2
    for r in range(2):
        x = x_ref[r * half:(r + 1) * half, :]
        xb = _modulate(x, g_ref[...], mod_ref[0:1, :], mod_ref[1:2, :]).astype(BF16)
        v = _gelu_tanh(jnp.dot(xb, win_ref[:, D_MODEL:], preferred_element_type=F32))
        vb = (_rms(v) * vn_ref[...]).astype(BF16)
        u = _gelu_tanh(jnp.dot(xb, win_ref[:, :D_MODEL], preferred_element_type=F32))
        for n in range(half // CHUNK):
            rows = slice(n * CHUNK, (n + 1) * CHUNK)
            trows = slice(r * half + n * CHUNK, r * half + (n + 1) * CHUNK)
            for g in range(C_GROUPS):
                cols = slice(g * C_GROUP_W, (g + 1) * C_GROUP_W)
                sv = jnp.dot(ws_ref[g], vb[rows, cols], preferred_element_type=F32) + bias_ref[:, cols]
                t_ref[trows, cols] = (u[rows, cols] * sv).astype(BF16)
        hrows = slice(r * half, (r + 1) * half)
        y = jnp.dot(t_ref[hrows, :], wout_ref[...], preferred_element_type=F32)
        _close_mixer(x, y, mod_ref, gf_ref, o_ref, xb_o, hrows)


def _chunk_mlp(x, mods, row0, layer, c_layer, g, g_ffn, w_in, v_norm, w_s, bias, w_out):
    nb, s, d = x.shape
    tm = TM_CMLP
    tile = pl.BlockSpec((None, tm, d), lambda b, i: (b, i, 0))
    return pl.pallas_call(
        _cmlp_kernel,
        grid=(nb, s // tm),
        in_specs=[
            tile,
            _mod_spec(layer, row0),
            _layer_row(d, layer),
            _layer_row(d, layer),
            _layer_resident((d, 2 * d), c_layer),
            _layer_row(d, c_layer),
            _layer_resident((C_GROUPS, CHUNK, CHUNK), c_layer),
            _layer_resident((CHUNK, d), c_layer),
            _layer_resident((d, d), c_layer),
        ],
        out_specs=[tile, tile],
        out_shape=[jax.ShapeDtypeStruct((nb, s, d), F32), jax.ShapeDtypeStruct((nb, s, d), BF16)],
        scratch_shapes=[pltpu.VMEM((tm, d), BF16)],
        compiler_params=_cparams(("parallel", "parallel"), 60),
        name="chunk_mlp",
    )(x, mods, g, g_ffn, w_in, v_norm, w_s, bias, w_out)


def _rope_tables(n_tokens):
    t = np.arange(n_tokens)
    row = (t // GRID_W).astype(np.float32)
    col = (t % GRID_W).astype(np.float32)

    def axis_tables(pos, half):
        inv = (1.0 / (ROPE_BASE ** (np.arange(0, half, 2, dtype=np.float32) / half))).astype(np.float32)
        ang = (pos[:, None] * inv).astype(np.float32).astype(np.float64)
        c, s = np.cos(ang), np.sin(ang)
        return np.concatenate([c, c], axis=1), np.concatenate([-s, s], axis=1)

    def tables(dim):
        cr, sr = axis_tables(row, dim // 2)
        cc, sc = axis_tables(col, dim // 2)
        return np.concatenate([cr, cc], axis=1), np.concatenate([sr, sc], axis=1)

    c128, s128 = tables(HEAD_DIM)
    c64, s64 = tables(B_ROPE)
    pad = np.zeros((n_tokens, LANES - B_ROPE))
    c64 = np.concatenate([c64, pad], axis=1)
    s64 = np.concatenate([s64, pad], axis=1)
    return tuple(jnp.asarray(a, F32) for a in (c128, s128, c64, s64))


def kernel(x_prompt, x_sample, c, cache_gqa_k, cache_gqa_v, cache_mla_ckv, cache_mla_krope, c_ctx, ada_w, ada_b, norm_mix, norm_ffn, ffn_gate, ffn_up, ffn_down, attn_w_in, attn_q_norm, attn_k_norm, attn_kv_norm, attn_w_kv_up, attn_w_out, cmlp_w_in, cmlp_v_norm, cmlp_w_s, cmlp_b_s, cmlp_w_out, final_norm):
    n_p, s_p, d = x_prompt.shape
    n_s, s_s, _ = x_sample.shape
    past = cache_gqa_k.shape[2]

    cond = jnp.concatenate([c_ctx[None, :], c, jnp.zeros((MOD_ROWS - 1 - n_s, d), F32)], axis=0)
    mods = _ada_mod(cond, ada_w, ada_b)

    def rows(a):
        return a.reshape(a.shape[0], 1, a.shape[1])

    def heads_t(vt):
        return vt.reshape(vt.shape[0], B_HEADS, B_V, vt.shape[2])

    aw = _layout_attn_weights(attn_w_in, attn_w_kv_up, attn_w_out)
    g_mix, g_ffn = rows(norm_mix), rows(norm_ffn)
    qn, kn, kvn = rows(attn_q_norm), rows(attn_k_norm), rows(attn_kv_norm)
    wg, wu, wd = ffn_gate.astype(BF16), ffn_up.astype(BF16), ffn_down.astype(BF16)
    cw_in, cw_out, cw_s = cmlp_w_in.astype(BF16), cmlp_w_out.astype(BF16), cmlp_w_s.astype(BF16)
    c_vn = rows(cmlp_v_norm)
    c_bias = jnp.repeat(jnp.swapaxes(cmlp_b_s, 1, 2), C_GROUP_W, axis=2)
    final_g = final_norm.reshape(1, d)

    xp = x_prompt.reshape(1, n_p * s_p, d)
    xs = x_sample
    rope = _rope_tables(s_s)
    cache_k = cache_gqa_k.reshape(n_s, -1, past, A_KV_W)
    cache_v = cache_gqa_v.reshape(n_s, -1, past, A_KV_W)

    states = ([], [], [], [])
    for i in range(DEPTH):
        j = i // 2
        if i % 2 == 0:
            qa, ka, vat, qb, kb, vbt, st_k, st_v, st_c, st_r = _attn_project(
                xp, mods, 0, i, j, g_mix, aw, qn, kn, kvn)
            for lst, st in zip(states, (st_k, st_v, st_c, st_r)):
                lst.append(st)
            oa = _attention(qa, [(ka, vat, KV_LANES)], seq_len=s_p, hb=A_HEADS, group=A_GROUP, dq=HEAD_DIM,
                            tqb=s_p, tq=s_p, rolled=False)
            ob = _attention(qb, [(kb, heads_t(vbt), KV_HEADS)], seq_len=s_p, hb=B_HEADS, group=1, dq=B_QK_PAD,
                            tqb=s_p, tq=s_p, rolled=False)
            xp, xpb = _attn_out(xp, mods, 0, i, j, g_ffn, oa, ob, aw.out)

            qa, ka, vat, qb, kb, vbt = _attn_project(xs, mods, 1, i, j, g_mix, aw, qn, kn, kvn, rope=rope)
            kb_ctx, vbt_ctx = _ctx_kv_expand(cache_mla_ckv, cache_mla_krope, j, aw)
            oa = _attention(qa, [(ka, vat, KV_LANES), (cache_k, cache_v, KV_CACHE)], seq_len=s_s, hb=A_GROUP,
                            group=A_GROUP, dq=HEAD_DIM, tqb=2048, tq=TQ_ATTN, rolled=True, seg_layer=j)
            ob = _attention(qb, [(kb, heads_t(vbt), KV_HEADS), (kb_ctx, heads_t(vbt_ctx), KV_HEADS)],
                            seq_len=s_s, hb=1, group=1, dq=B_QK_PAD, tqb=4096, tq=TQ_ATTN, rolled=True)
            xs, xsb = _attn_out(xs, mods, 1, i, j, g_ffn, oa, ob, aw.out)
        else:
            xp, xpb = _chunk_mlp(xp, mods, 0, i, j, g_mix, g_ffn, cw_in, c_vn, cw_s, c_bias, cw_out)
            xs, xsb = _chunk_mlp(xs, mods, 1, i, j, g_mix, g_ffn, cw_in, c_vn, cw_s, c_bias, cw_out)

        fin = final_g if i == DEPTH - 1 else None
        xp = _ffn(xp, xpb, mods, 0, i, wg, wu, wd, fin)
        xs = _ffn(xs, xsb, mods, 1, i, wg, wu, wd, fin)

    y_prompt = xp.reshape(n_p, s_p, d)
    st_k, st_v, st_c, st_r = (jnp.stack([a.reshape(n_p, s_p, a.shape[-1]) for a in lst], axis=1)
                              for lst in states)
    return (y_prompt, xs,
            st_k.reshape(n_p, -1, s_p, A_KV_HEADS, HEAD_DIM),
            st_v.reshape(n_p, -1, s_p, A_KV_HEADS, HEAD_DIM),
            st_c, st_r)
```

```python
import functools
from typing import NamedTuple

import numpy as np
import jax
import jax.numpy as jnp
from jax import lax
from jax.experimental import pallas as pl
from jax.experimental.pallas import tpu as pltpu

F32 = jnp.float32
BF16 = jnp.bfloat16

D_MODEL = 2048
DEPTH = 4
GRID_W = 64
ROPE_BASE = 10000.0
EPS = 1e-6
A_HEADS = 8
A_KV_HEADS = 2
A_GROUP = A_HEADS // A_KV_HEADS
HEAD_DIM = 128
A_SCALE = HEAD_DIM ** -0.5
B_HEADS = 8
B_NOPE = 128
B_ROPE = 64
B_V = 128
KV_RANK = 512
B_SCALE = (B_NOPE + B_ROPE) ** -0.5
B_QK_PAD = 256
A_Q_W = A_HEADS * HEAD_DIM
A_KV_W = A_KV_HEADS * HEAD_DIM
B_Q_W = B_HEADS * (B_NOPE + B_ROPE)
CHUNK = 128
C_GROUPS = 8
C_GROUP_W = D_MODEL // C_GROUPS
FFN_HIDDEN = 5632
LOG2E = 1.4426950408889634
LANES = 128

Z_QA = 0
Z_KA = Z_QA + A_Q_W
Z_VA = Z_KA + A_KV_W
Z_QB = Z_VA + A_KV_W
Z_CKV = Z_QB + B_HEADS * B_QK_PAD
Z_KR = Z_CKV + KV_RANK
Z_W = Z_KR + LANES

MOD_ROWS = 8
_NT_DIMS = (((1,), (1,)), ((), ()))

TM_FFN, TC_FFN = 1024, 512
TM_PROJ = 512
TM_ATTN_OUT = 512
TM_CMLP = 512
TN_ADA = 1024
TQ_ATTN = 256
ATTN_UNROLL = 4


def _cparams(sem, vmem_mib, flags=None):
    return pltpu.CompilerParams(dimension_semantics=sem, vmem_limit_bytes=vmem_mib << 20, flags=flags)


def _layer_resident(shape, layer):
    n = len(shape)
    return pl.BlockSpec((None,) + tuple(shape), lambda *_: (layer,) + (0,) * n, pipeline_mode=pl.Buffered(1))


def _layer_row(width, layer):
    return pl.BlockSpec((None, 1, width), lambda *_: (layer, 0, 0))


def _mod_spec(layer, row0):
    return pl.BlockSpec((None, None, 6, D_MODEL), lambda b, *_: (layer, row0 + b, 0, 0))


def _rms(x):
    return x * lax.rsqrt(jnp.mean(x * x, axis=-1, keepdims=True) + EPS)


def _modulate(x, g, shift, scale):
    return (_rms(x) * g) * (1.0 + scale) + shift


def _swap_halves(v, n):
    lane = lax.broadcasted_iota(jnp.int32, v.shape, 1)
    fwd = pltpu.roll(v, LANES - n, 1)
    bwd = pltpu.roll(v, n, 1)
    return jnp.where((lane & (2 * n - 1)) < n, fwd, bwd)


def _ada_kernel(cond_ref, w_ref, b_ref, o_ref):
    c = cond_ref[...]
    a = (c * jax.nn.sigmoid(c)).astype(BF16)
    o_ref[...] = jnp.dot(a, w_ref[...].astype(BF16), preferred_element_type=F32) + b_ref[...]


def _ada_mod(cond, ada_w, ada_b):
    n6 = 6 * D_MODEL
    out = pl.pallas_call(
        _ada_kernel,
        grid=(DEPTH, n6 // TN_ADA),
        in_specs=[
            pl.BlockSpec((MOD_ROWS, D_MODEL), lambda l, n: (0, 0)),
            pl.BlockSpec((None, D_MODEL, TN_ADA), lambda l, n: (l, 0, n)),
            pl.BlockSpec((None, 1, TN_ADA), lambda l, n: (l, 0, n)),
        ],
        out_specs=pl.BlockSpec((None, MOD_ROWS, TN_ADA), lambda l, n: (l, 0, n)),
        out_shape=jax.ShapeDtypeStruct((DEPTH, MOD_ROWS, n6), F32),
        compiler_params=_cparams(("arbitrary", "arbitrary"), 40),
        name="ada_mod",
    )(cond, ada_w, ada_b.reshape(DEPTH, 1, n6))
    return out.reshape(DEPTH, MOD_ROWS, 6, D_MODEL)


def _ffn_kernel(*refs, n_chunks, final):
    if final:
        x_hbm, xb_ref, mod_ref, wg_ref, wu_ref, wd_ref, fn_ref, o_ref, x_buf, x_sem = refs
    else:
        x_hbm, xb_ref, mod_ref, wg_ref, wu_ref, wd_ref, o_ref, x_buf, x_sem = refs
    b, i, j = pl.program_id(0), pl.program_id(1), pl.program_id(2)
    tm = x_buf.shape[0]
    x_copy = pltpu.make_async_copy(x_hbm.at[b, pl.ds(pl.multiple_of(i * tm, tm), tm), :], x_buf, x_sem)

    @pl.when(j == 0)
    def _():
        x_copy.start()
        o_ref[...] = jnp.zeros_like(o_ref)

    xb = xb_ref[...]
    gate = jnp.dot(xb, wg_ref[...], preferred_element_type=F32)
    up = jnp.dot(xb, wu_ref[...], preferred_element_type=F32)
    h = (gate * jax.nn.sigmoid(gate) * up).astype(BF16)
    o_ref[...] += jnp.dot(h, wd_ref[...], preferred_element_type=F32)

    @pl.when(j == n_chunks - 1)
    def _():
        x_copy.wait()
        y = x_buf[...] + mod_ref[5:6, :] * o_ref[...]
        if final:
            y = _rms(y) * fn_ref[...]
        o_ref[...] = y


def _ffn(x, xb, mods, row0, layer, wg, wu, wd, final_g=None):
    nb, s, d = x.shape
    tm, tc = TM_FFN, TC_FFN
    n_chunks = FFN_HIDDEN // tc
    final = final_g is not None
    in_specs = [
        pl.BlockSpec(memory_space=pl.ANY),
        pl.BlockSpec((None, tm, d), lambda b, i, j: (b, i, 0)),
        _mod_spec(layer, row0),
        pl.BlockSpec((d, tc), lambda b, i, j: (0, j)),
        pl.BlockSpec((d, tc), lambda b, i, j: (0, j)),
        pl.BlockSpec((tc, d), lambda b, i, j: (j, 0)),
    ]
    args = [x, xb, mods, wg, wu, wd]
    if final:
        in_specs.append(pl.BlockSpec((1, d), lambda b, i, j: (0, 0)))
        args.append(final_g)
    return pl.pallas_call(
        functools.partial(_ffn_kernel, n_chunks=n_chunks, final=final),
        grid=(nb, s // tm, n_chunks),
        in_specs=in_specs,
        out_specs=pl.BlockSpec((None, tm, d), lambda b, i, j: (b, i, 0)),
        out_shape=jax.ShapeDtypeStruct((nb, s, d), F32),
        scratch_shapes=[pltpu.VMEM((tm, d), F32), pltpu.SemaphoreType.DMA(())],
        compiler_params=_cparams(("arbitrary", "arbitrary", "arbitrary"), 60),
        name="ffn_final" if final else "ffn",
    )(*args)


class _AttnWeights(NamedTuple):
    main: jax.Array
    va_t: jax.Array
    kv_k: jax.Array
    kv_vt: jax.Array
    out: jax.Array


def _layout_attn_weights(w_in, w_kv_up, w_out):
    n_l, d, _ = w_in.shape
    o = A_Q_W + 2 * A_KV_W
    qb = w_in[:, :, o:o + B_Q_W].reshape(n_l, d, B_HEADS, B_NOPE + B_ROPE)
    qb = jnp.pad(qb, ((0, 0), (0, 0), (0, 0), (0, B_QK_PAD - B_NOPE - B_ROPE)))
    ckv = w_in[:, :, o + B_Q_W:o + B_Q_W + KV_RANK]
    kr = jnp.pad(w_in[:, :, o + B_Q_W + KV_RANK:], ((0, 0), (0, 0), (0, LANES - B_ROPE)))
    main = jnp.concatenate([w_in[:, :, :o], qb.reshape(n_l, d, B_HEADS * B_QK_PAD), ckv, kr], axis=2)
    kv4 = w_kv_up.reshape(n_l, KV_RANK, B_HEADS, B_NOPE + B_V)
    return _AttnWeights(
        main=main.astype(BF16),
        va_t=jnp.swapaxes(w_in[:, :, Z_VA:Z_VA + A_KV_W], 1, 2).astype(BF16),
        kv_k=kv4[..., :B_NOPE].reshape(n_l, KV_RANK, -1).astype(BF16),
        kv_vt=jnp.swapaxes(kv4[..., B_NOPE:].reshape(n_l, KV_RANK, -1), 1, 2).astype(BF16),
        out=w_out.astype(BF16))


def _proj_kernel(*refs, prompt):
    if prompt:
        (x_ref, mod_ref, g_ref, w_ref, wvat_ref, wkv_ref, wkvt_ref, qn_ref, kn_ref, kvn_ref,
         qa_o, ka_o, vat_o, qb_o, kb_o, vbt_o, sk_o, sv_o, sc_o, sr_o) = refs
    else:
        (x_ref, mod_ref, g_ref, w_ref, wvat_ref, wkv_ref, wkvt_ref, qn_ref, kn_ref, kvn_ref,
         c128_ref, s128_ref, c64_ref, s64_ref,
         qa_o, ka_o, vat_o, qb_o, kb_o, vbt_o) = refs

    half = x_ref.shape[0] // 2
    for r in range(2):
        rows = slice(r * half, (r + 1) * half)
        if not prompt:
            c128, s128 = c128_ref[rows, :], s128_ref[rows, :]
            c64, s64 = c64_ref[rows, :], s64_ref[rows, :]

        def rope128(v):
            return v if prompt else v * c128 + _swap_halves(v, 32) * s128

        def rope64(v):
            return v if prompt else v * c64 + _swap_halves(v, 16) * s64

        xb = _modulate(x_ref[rows, :], g_ref[...], mod_ref[0:1, :], mod_ref[1:2, :]).astype(BF16)
        z = jnp.dot(xb, w_ref[...], preferred_element_type=F32)
        vat_o[:, rows] = lax.dot_general(wvat_ref[...], xb, _NT_DIMS, preferred_element_type=F32).astype(BF16)

        for h in range(A_HEADS):
            q = _rms(z[:, Z_QA + h * HEAD_DIM:Z_QA + (h + 1) * HEAD_DIM]) * qn_ref[...]
            qa_o[h, rows, :] = (rope128(q) * (A_SCALE * LOG2E)).astype(BF16)
        for h in range(A_KV_HEADS):
            sl = slice(h * HEAD_DIM, (h + 1) * HEAD_DIM)
            k = _rms(z[:, Z_KA + h * HEAD_DIM:Z_KA + (h + 1) * HEAD_DIM]) * kn_ref[...]
            if prompt:
                sk_o[rows, sl] = k
            ka_o[rows, sl] = rope128(k).astype(BF16)
        if prompt:
            sv_o[rows, :] = z[:, Z_VA:Z_VA + A_KV_W]
        for h in range(B_HEADS):
            base = Z_QB + h * B_QK_PAD
            qb_o[h, rows, :B_NOPE] = (z[:, base:base + B_NOPE] * (B_SCALE * LOG2E)).astype(BF16)
            qb_o[h, rows, B_NOPE:] = (
                rope64(z[:, base + B_NOPE:base + B_QK_PAD]) * (B_SCALE * LOG2E)).astype(BF16)

        ckv = _rms(z[:, Z_CKV:Z_CKV + KV_RANK]) * kvn_ref[...]
        kr = z[:, Z_KR:Z_KR + LANES]
        if prompt:
            sc_o[rows, :] = ckv
            sr_o[rows, :] = kr[:, :B_ROPE]
        krb = rope64(kr).astype(BF16)
        ckvb = ckv.astype(BF16)
        kn_all = jnp.dot(ckvb, wkv_ref[...], preferred_element_type=F32)
        for h in range(B_HEADS):
            kb_o[h, rows, :B_NOPE] = kn_all[:, h * B_NOPE:(h + 1) * B_NOPE].astype(BF16)
            kb_o[h, rows, B_NOPE:] = krb
        vbt_o[:, rows] = lax.dot_general(wkvt_ref[...], ckvb, _NT_DIMS, preferred_element_type=F32).astype(BF16)


def _attn_project(x, mods, row0, layer, attn_layer, g, w, qn, kn, kvn, rope=None):
    nb, s, d = x.shape
    tm = TM_PROJ
    prompt = rope is None

    def row(wd):
        return pl.BlockSpec((None, tm, wd), lambda b, i: (b, i, 0))

    def col(wd):
        return pl.BlockSpec((None, wd, tm), lambda b, i: (b, 0, i))

    def heads(n, wd):
        return pl.BlockSpec((None, n, tm, wd), lambda b, i: (b, 0, i, 0))

    in_specs = [
        row(d),
        _mod_spec(layer, row0),
        _layer_row(d, layer),
        _layer_resident((d, Z_W), attn_layer),
        _layer_resident((A_KV_W, d), attn_layer),
        _layer_resident((KV_RANK, B_HEADS * B_NOPE), attn_layer),
        _layer_resident((B_HEADS * B_V, KV_RANK), attn_layer),
        _layer_row(HEAD_DIM, attn_layer),
        _layer_row(HEAD_DIM, attn_layer),
        _layer_row(KV_RANK, attn_layer),
    ]
    args = [x, mods, g, w.main, w.va_t, w.kv_k, w.kv_vt, qn, kn, kvn]
    out_specs = [heads(A_HEADS, HEAD_DIM), row(A_KV_W), col(A_KV_W), heads(B_HEADS, B_QK_PAD),
                 heads(B_HEADS, B_QK_PAD), col(B_HEADS * B_V)]
    out_shape = [jax.ShapeDtypeStruct(shp, BF16) for shp in (
        (nb, A_HEADS, s, HEAD_DIM), (nb, s, A_KV_W), (nb, A_KV_W, s), (nb, B_HEADS, s, B_QK_PAD),
        (nb, B_HEADS, s, B_QK_PAD), (nb, B_HEADS * B_V, s))]
    if prompt:
        for wd in (A_KV_W, A_KV_W, KV_RANK, B_ROPE):
            out_specs.append(row(wd))
            out_shape.append(jax.ShapeDtypeStruct((nb, s, wd), F32))
    else:
        in_specs += [pl.BlockSpec((tm, LANES), lambda b, i: (i, 0))] * 4
        args += list(rope)
    return pl.pallas_call(
        functools.partial(_proj_kernel, prompt=prompt),
        grid=(nb, s // tm),
        in_specs=in_specs,
        out_specs=out_specs,
        out_shape=out_shape,
        compiler_params=_cparams(("parallel", "parallel"), 60),
        name="attn_proj_prompt" if prompt else "attn_proj_latent",
    )(*args)


def _ctx_kv_kernel(ckv_ref, kr_ref, wkv_ref, wkvt_ref, kb_o, vbt_o):
    ckvb = ckv_ref[...].astype(BF16)
    kn_all = jnp.dot(ckvb, wkv_ref[...], preferred_element_type=F32)
    krb = kr_ref[...].astype(BF16)
    zero = jnp.zeros((krb.shape[0], B_QK_PAD - B_NOPE - B_ROPE), BF16)
    for h in range(B_HEADS):
        kb_o[h, :, :B_NOPE] = kn_all[:, h * B_NOPE:(h + 1) * B_NOPE].astype(BF16)
        kb_o[h, :, B_NOPE:B_NOPE + B_ROPE] = krb
        kb_o[h, :, B_NOPE + B_ROPE:] = zero
    vbt_o[...] = lax.dot_general(wkvt_ref[...], ckvb, _NT_DIMS, preferred_element_type=F32).astype(BF16)


def _ctx_kv_expand(cache_ckv, cache_kr, attn_layer, w):
    nb, _, p, _ = cache_ckv.shape
    return pl.pallas_call(
        _ctx_kv_kernel,
        grid=(nb,),
        in_specs=[
            pl.BlockSpec((None, None, p, KV_RANK), lambda b: (b, attn_layer, 0, 0)),
            pl.BlockSpec((None, None, p, B_ROPE), lambda b: (b, attn_layer, 0, 0)),
            pl.BlockSpec((None, KV_RANK, B_HEADS * B_NOPE), lambda b: (attn_layer, 0, 0)),
            pl.BlockSpec((None, B_HEADS * B_V, KV_RANK), lambda b: (attn_layer, 0, 0)),
        ],
        out_specs=[
            pl.BlockSpec((None, B_HEADS, p, B_QK_PAD), lambda b: (b, 0, 0, 0)),
            pl.BlockSpec((None, B_HEADS * B_V, p), lambda b: (b, 0, 0)),
        ],
        out_shape=[
            jax.ShapeDtypeStruct((nb, B_HEADS, p, B_QK_PAD), BF16),
            jax.ShapeDtypeStruct((nb, B_HEADS * B_V, p), BF16),
        ],
        compiler_params=_cparams(("parallel",), 32),
        name="ctx_kv_expand",
    )(cache_ckv, cache_kr, w.kv_k, w.kv_vt)


KV_LANES = "lanes"
KV_HEADS = "heads"
KV_CACHE = "cache"


def _attn_kernel(*refs, seg_len, seg_layout, n_cast, hb, group, dq, dv, tq, rolled):
    nseg = len(seg_len)
    q_ref = refs[0]
    kv_refs = refs[1:1 + 2 * nseg]
    cast_in = refs[1 + 2 * nseg:1 + 2 * nseg + n_cast]
    o_ref = refs[1 + 2 * nseg + n_cast]
    cast_out = refs[2 + 2 * nseg + n_cast:2 + 2 * nseg + 2 * n_cast]
    s_ref, m_ref = refs[2 + 2 * nseg + 2 * n_cast:]
    for src, dst in zip(cast_in, cast_out):
        dst[...] = src[...].astype(BF16)
    n_sub = q_ref.shape[1] // tq
    n_items = hb * n_sub
    seg_off = [sum(seg_len[:i]) for i in range(nseg)]
    dynamic_kv = all(lay == KV_HEADS for lay in seg_layout)

    def split(i):
        if isinstance(i, int):
            return i // n_sub, (i % n_sub) * tq
        shift = n_sub.bit_length() - 1
        return lax.shift_right_logical(i, shift), pl.multiple_of((i & (n_sub - 1)) * tq, tq)

    def kv_head(h):
        if isinstance(h, int):
            return h // group
        return lax.shift_right_logical(h, group.bit_length() - 1) if dynamic_kv else 0

    def keys(s, kvh):
        ref = kv_refs[2 * s]
        k = ref[kvh] if seg_layout[s] == KV_HEADS else ref[:, kvh * dq:(kvh + 1) * dq]
        return k.astype(BF16)

    def values_t(s, kvh):
        ref = kv_refs[2 * s + 1]
        if seg_layout[s] == KV_HEADS:
            vt = ref[kvh]
        elif seg_layout[s] == KV_LANES:
            vt = ref[kvh * dv:(kvh + 1) * dv, :]
        else:
            vt = ref[:, kvh * dv:(kvh + 1) * dv].T
        return vt.astype(BF16)

    def score(i, slot):
        h, r = split(i)
        kvh = kv_head(h)
        q = q_ref[h, pl.ds(r, tq), :]
        m = None
        for s in range(nseg):
            st = lax.dot_general(keys(s, kvh), q, _NT_DIMS, preferred_element_type=F32)
            s_ref[slot, seg_off[s]:seg_off[s] + seg_len[s], :] = st
            ms = jnp.max(st, axis=0, keepdims=True)
            m = ms if m is None else jnp.maximum(m, ms)
        m_ref[slot] = m

    def finish(i, slot):
        h, r = split(i)
        kvh = kv_head(h)
        m = m_ref[slot]
        denom = None
        acc = None
        for s in range(nseg):
            p = jnp.exp2(s_ref[slot, seg_off[s]:seg_off[s] + seg_len[s], :] - m)
            ls = jnp.sum(p, axis=0, keepdims=True)
            os_ = jnp.dot(values_t(s, kvh), p.astype(BF16), preferred_element_type=F32)
            denom = ls if denom is None else denom + ls
            acc = os_ if acc is None else acc + os_
        o_ref[h, pl.ds(r, tq), :] = (acc / denom).T.astype(o_ref.dtype)

    if rolled:
        assert dynamic_kv or hb == group
        unroll = ATTN_UNROLL
        assert unroll % 2 == 0 and n_items % unroll == 0 and n_items >= 2 * unroll
        assert n_sub & (n_sub - 1) == 0 and group & (group - 1) == 0
        score(0, 0)

        def body(k, carry):
            for u in range(unroll):
                score(unroll * k + u + 1, (u + 1) % 2)
                finish(unroll * k + u, u % 2)
            return carry

        lax.fori_loop(0, n_items // unroll - 1, body, 0)
        for u in range(unroll - 1):
            score(n_items - unroll + u + 1, (u + 1) % 2)
            finish(n_items - unroll + u, u % 2)
        finish(n_items - 1, (unroll - 1) % 2)
    else:
        score(0, 0)
        for i in range(n_items):
            if i + 1 < n_items:
                score(i + 1, (i + 1) % 2)
            finish(i, i % 2)


def _attention(q, segs, *, seq_len, hb, group, dq, tqb, tq, rolled, seg_layer=None, cast=None):
    nbq, n_heads, s_tot, _ = q.shape
    dv = HEAD_DIM
    per = s_tot // seq_len
    n_qb = seq_len // tqb
    nkv = hb // group
    in_specs = [pl.BlockSpec((None, hb, tqb, dq), lambda g, h, t: (g // per, h, (g % per) * n_qb + t, 0))]
    args = [q]
    seg_lens = []
    for k, v, layout in segs:
        if layout == KV_CACHE:
            sk = k.shape[2]
            in_specs.append(pl.BlockSpec((None, None, sk, nkv * dq), lambda g, h, t: (g, seg_layer, 0, h)))
            in_specs.append(pl.BlockSpec((None, None, sk, nkv * dv), lambda g, h, t: (g, seg_layer, 0, h)))
        elif layout == KV_LANES:
            sk = k.shape[1] // per
            in_specs.append(pl.BlockSpec((None, sk, nkv * dq), lambda g, h, t: (g // per, g % per, h)))
            in_specs.append(pl.BlockSpec((None, nkv * dv, sk), lambda g, h, t: (g // per, h, g % per)))
        else:
            sk = k.shape[2] // per
            in_specs.append(pl.BlockSpec((None, nkv, sk, dq), lambda g, h, t: (g // per, h, g % per, 0)))
            in_specs.append(pl.BlockSpec((None, nkv, dv, sk), lambda g, h, t: (g // per, h, 0, g % per)))
        seg_lens.append(sk)
        args += [k, v]
    grid = (nbq * per, n_heads // hb, n_qb)
    out_specs = [pl.BlockSpec((None, hb, tqb, dv), lambda g, h, t: (g // per, h, (g % per) * n_qb + t, 0))]
    out_shape = [jax.ShapeDtypeStruct((nbq, n_heads, s_tot, dv), BF16)]
    cast_arrays, cast_layer = cast if cast is not None else ((), None)
    n_steps = grid[0] * grid[1] * grid[2]
    for a in cast_arrays:
        _, r, c = a.shape
        in_specs.append(pl.BlockSpec((None, r // n_steps, c),
                                     lambda g, h, t: (cast_layer, (g * grid[1] + h) * grid[2] + t, 0)))
        out_specs.append(pl.BlockSpec((r // n_steps, c), lambda g, h, t: ((g * grid[1] + h) * grid[2] + t, 0)))
        out_shape.append(jax.ShapeDtypeStruct((r, c), BF16))
        args.append(a)
    outs = pl.pallas_call(
        functools.partial(_attn_kernel, seg_len=tuple(seg_lens), seg_layout=tuple(s[2] for s in segs),
                          n_cast=len(cast_arrays), hb=hb, group=group, dq=dq, dv=dv, tq=tq, rolled=rolled),
        grid=grid,
        in_specs=in_specs,
        out_specs=out_specs,
        out_shape=out_shape,
        scratch_shapes=[pltpu.VMEM((2, sum(seg_lens), tq), F32), pltpu.VMEM((2, 1, tq), F32)],
        compiler_params=_cparams(("parallel", "parallel", "arbitrary"), 56),
        name="attention",
    )(*args)
    return (outs[0], tuple(outs[1:])) if cast_arrays else outs[0]


def _close_mixer(x, y, mod_ref, gf_ref, o_ref, xb_o, rows=slice(None)):
    x_new = x + mod_ref[2:3, :] * y
    o_ref[rows, :] = x_new
    xb_o[rows, :] = _modulate(x_new, gf_ref[...], mod_ref[3:4, :], mod_ref[4:5, :]).astype(BF16)


def _attn_out_kernel(x_ref, mod_ref, gf_ref, oa_ref, ob_ref, w_ref, o_ref, xb_o):
    half = x_ref.shape[0] // 2
    for r in range(2):
        rows = slice(r * half, (r + 1) * half)
        cat = jnp.concatenate([oa_ref[h, rows, :] for h in range(A_HEADS)]
                              + [ob_ref[h, rows, :] for h in range(B_HEADS)], axis=-1)
        y = jnp.dot(cat, w_ref[...], preferred_element_type=F32)
        _close_mixer(x_ref[rows, :], y, mod_ref, gf_ref, o_ref, xb_o, rows)


def _attn_out(x, mods, row0, layer, attn_layer, g_ffn, oa, ob, w_out):
    nb, s, d = x.shape
    tm = TM_ATTN_OUT
    tile = pl.BlockSpec((None, tm, d), lambda b, i: (b, i, 0))
    return pl.pallas_call(
        _attn_out_kernel,
        grid=(nb, s // tm),
        in_specs=[
            tile,
            _mod_spec(layer, row0),
            _layer_row(d, layer),
            pl.BlockSpec((None, A_HEADS, tm, HEAD_DIM), lambda b, i: (b, 0, i, 0)),
            pl.BlockSpec((None, B_HEADS, tm, B_V), lambda b, i: (b, 0, i, 0)),
            _layer_resident((A_Q_W + B_HEADS * B_V, d), attn_layer),
        ],
        out_specs=[tile, tile],
        out_shape=[jax.ShapeDtypeStruct((nb, s, d), F32), jax.ShapeDtypeStruct((nb, s, d), BF16)],
        compiler_params=_cparams(("parallel", "parallel"), 44),
        name="attn_out",
    )(x, mods, g_ffn, oa, ob, w_out)


def _gelu_tanh(x):
    return 0.5 * x * (1.0 + jnp.tanh(0.7978845608028654 * (x + 0.044715 * (x * x * x))))


def _cmlp_kernel(x_ref, mod_ref, g_ref, gf_ref, win_ref, vn_ref, ws_ref, bias_ref, wout_ref, o_ref, xb_o, t_ref):
    half = x_ref.shape[0] // 2
    for r in range(2):
        x = x_ref[r * half:(r + 1) * half, :]
        xb = _modulate(x, g_ref[...], mod_ref[0:1, :], mod_ref[1:2, :]).astype(BF16)
        v = _gelu_tanh(jnp.dot(xb, win_ref[:, D_MODEL:], preferred_element_type=F32))
        vb = (_rms(v) * vn_ref[...]).astype(BF16)
        u = _gelu_tanh(jnp.dot(xb, win_ref[:, :D_MODEL], preferred_element_type=F32))
        for n in range(half // CHUNK):
            rows = slice(n * CHUNK, (n + 1) * CHUNK)
            trows = slice(r * half + n * CHUNK, r * half + (n + 1) * CHUNK)
            for g in range(C_GROUPS):
                cols = slice(g * C_GROUP_W, (g + 1) * C_GROUP_W)
                sv = jnp.dot(ws_ref[g], vb[rows, cols], preferred_element_type=F32) + bias_ref[:, cols]
                t_ref[trows, cols] = (u[rows, cols] * sv).astype(BF16)
        hrows = slice(r * half, (r + 1) * half)
        y = jnp.dot(t_ref[hrows, :], wout_ref[...], preferred_element_type=F32)
        _close_mixer(x, y, mod_ref, gf_ref, o_ref, xb_o, hrows)


def _chunk_mlp(x, mods, row0, layer, c_layer, g, g_ffn, w_in, v_norm, w_s, bias, w_out):
    nb, s, d = x.shape
    tm = TM_CMLP
    tile = pl.BlockSpec((None, tm, d), lambda b, i: (b, i, 0))
    return pl.pallas_call(
        _cmlp_kernel,
        grid=(nb, s // tm),
        in_specs=[
            tile,
            _mod_spec(layer, row0),
            _layer_row(d, layer),
            _layer_row(d, layer),
            _layer_resident((d, 2 * d), c_layer),
            _layer_row(d, c_layer),
            _layer_resident((C_GROUPS, CHUNK, CHUNK), c_layer),
            _layer_resident((CHUNK, d), c_layer),
            _layer_resident((d, d), c_layer),
        ],
        out_specs=[tile, tile],
        out_shape=[jax.ShapeDtypeStruct((nb, s, d), F32), jax.ShapeDtypeStruct((nb, s, d), BF16)],
        scratch_shapes=[pltpu.VMEM((tm, d), BF16)],
        compiler_params=_cparams(("parallel", "parallel"), 60),
        name="chunk_mlp",
    )(x, mods, g, g_ffn, w_in, v_norm, w_s, bias, w_out)


def _rope_tables(n_tokens):
    t = np.arange(n_tokens)
    row = (t // GRID_W).astype(np.float32)
    col = (t % GRID_W).astype(np.float32)

    def axis_tables(pos, half):
        inv = (1.0 / (ROPE_BASE ** (np.arange(0, half, 2, dtype=np.float32) / half))).astype(np.float32)
        ang = (pos[:, None] * inv).astype(np.float32).astype(np.float64)
        c, s = np.cos(ang), np.sin(ang)
        return np.concatenate([c, c], axis=1), np.concatenate([-s, s], axis=1)

    def tables(dim):
        cr, sr = axis_tables(row, dim // 2)
        cc, sc = axis_tables(col, dim // 2)
        return np.concatenate([cr, cc], axis=1), np.concatenate([sr, sc], axis=1)

    c128, s128 = tables(HEAD_DIM)
    c64, s64 = tables(B_ROPE)
    pad = np.zeros((n_tokens, LANES - B_ROPE))
    c64 = np.concatenate([c64, pad], axis=1)
    s64 = np.concatenate([s64, pad], axis=1)
    return tuple(jnp.asarray(a, F32) for a in (c128, s128, c64, s64))


def kernel(x_prompt, x_sample, c, cache_gqa_k, cache_gqa_v, cache_mla_ckv, cache_mla_krope, c_ctx, ada_w, ada_b, norm_mix, norm_ffn, ffn_gate, ffn_up, ffn_down, attn_w_in, attn_q_norm, attn_k_norm, attn_kv_norm, attn_w_kv_up, attn_w_out, cmlp_w_in, cmlp_v_norm, cmlp_w_s, cmlp_b_s, cmlp_w_out, final_norm):
    n_p, s_p, d = x_prompt.shape
    n_s, s_s, _ = x_sample.shape
    past = cache_gqa_k.shape[2]

    cond = jnp.concatenate([c_ctx[None, :], c, jnp.zeros((MOD_ROWS - 1 - n_s, d), F32)], axis=0)
    mods = _ada_mod(cond, ada_w, ada_b)

    def rows(a):
        return a.reshape(a.shape[0], 1, a.shape[1])

    def heads_t(vt):
        return vt.reshape(vt.shape[0], B_HEADS, B_V, vt.shape[2])

    aw = _layout_attn_weights(attn_w_in, attn_w_kv_up, attn_w_out)
    g_mix, g_ffn = rows(norm_mix), rows(norm_ffn)
    qn, kn, kvn = rows(attn_q_norm), rows(attn_k_norm), rows(attn_kv_norm)
    ffn_f32 = (ffn_gate, ffn_up, ffn_down)
    ffn_w = {}
    cw_in, cw_out, cw_s = cmlp_w_in.astype(BF16), cmlp_w_out.astype(BF16), cmlp_w_s.astype(BF16)
    c_vn = rows(cmlp_v_norm)
    c_bias = jnp.repeat(jnp.swapaxes(cmlp_b_s, 1, 2), C_GROUP_W, axis=2)
    final_g = final_norm.reshape(1, d)

    xp = x_prompt.reshape(1, n_p * s_p, d)
    xs = x_sample
    rope = _rope_tables(s_s)
    cache_k = cache_gqa_k.reshape(n_s, -1, past, A_KV_W)
    cache_v = cache_gqa_v.reshape(n_s, -1, past, A_KV_W)

    states = ([], [], [], [])
    for i in range(DEPTH):
        j = i // 2
        if i % 2 == 0:
            qa, ka, vat, qb, kb, vbt, st_k, st_v, st_c, st_r = _attn_project(
                xp, mods, 0, i, j, g_mix, aw, qn, kn, kvn)
            for lst, st in zip(states, (st_k, st_v, st_c, st_r)):
                lst.append(st)
            oa = _attention(qa, [(ka, vat, KV_LANES)], seq_len=s_p, hb=A_HEADS, group=A_GROUP, dq=HEAD_DIM,
                            tqb=s_p, tq=s_p, rolled=False)
            ob = _attention(qb, [(kb, heads_t(vbt), KV_HEADS)], seq_len=s_p, hb=B_HEADS, group=1, dq=B_QK_PAD,
                            tqb=s_p, tq=s_p, rolled=False)
            xp, xpb = _attn_out(xp, mods, 0, i, j, g_ffn, oa, ob, aw.out)

            qa, ka, vat, qb, kb, vbt = _attn_project(xs, mods, 1, i, j, g_mix, aw, qn, kn, kvn, rope=rope)
            kb_ctx, vbt_ctx = _ctx_kv_expand(cache_mla_ckv, cache_mla_krope, j, aw)
            oa, ffn_w[i] = _attention(
                qa, [(ka, vat, KV_LANES), (cache_k, cache_v, KV_CACHE)], seq_len=s_s, hb=A_GROUP, group=A_GROUP,
                dq=HEAD_DIM, tqb=2048, tq=TQ_ATTN, rolled=True, seg_layer=j, cast=(ffn_f32, i))
            ob, ffn_w[i + 1] = _attention(
                qb, [(kb, heads_t(vbt), KV_HEADS), (kb_ctx, heads_t(vbt_ctx), KV_HEADS)], seq_len=s_s, hb=1,
                group=1, dq=B_QK_PAD, tqb=4096, tq=TQ_ATTN, rolled=True, cast=(ffn_f32, i + 1))
            xs, xsb = _attn_out(xs, mods, 1, i, j, g_ffn, oa, ob, aw.out)
        else:
            xp, xpb = _chunk_mlp(xp, mods, 0, i, j, g_mix, g_ffn, cw_in, c_vn, cw_s, c_bias, cw_out)
            xs, xsb = _chunk_mlp(xs, mods, 1, i, j, g_mix, g_ffn, cw_in, c_vn, cw_s, c_bias, cw_out)

        fin = final_g if i == DEPTH - 1 else None
        xp = _ffn(xp, xpb, mods, 0, i, *ffn_w[i], fin)
        xs = _ffn(xs, xsb, mods, 1, i, *ffn_w[i], fin)

    y_prompt = xp.reshape(n_p, s_p, d)
    st_k, st_v, st_c, st_r = (jnp.stack([a.reshape(n_p, s_p, a.shape[-1]) for a in lst], axis=1)
                              for lst in states)
    return (y_prompt, xs,
            st_k.reshape(n_p, -1, s_p, A_KV_HEADS, HEAD_DIM),
            st_v.reshape(n_p, -1, s_p, A_KV_HEADS, HEAD_DIM),
            st_c, st_r)
```

```python
import functools
from typing import NamedTuple

import numpy as np
import jax
import jax.numpy as jnp
from jax import lax
from jax.experimental import pallas as pl
from jax.experimental.pallas import tpu as pltpu

F32 = jnp.float32
BF16 = jnp.bfloat16

D_MODEL = 2048
DEPTH = 4
GRID_W = 64
ROPE_BASE = 10000.0
EPS = 1e-6
A_HEADS = 8
A_KV_HEADS = 2
A_GROUP = A_HEADS // A_KV_HEADS
HEAD_DIM = 128
A_SCALE = HEAD_DIM ** -0.5
B_HEADS = 8
B_NOPE = 128
B_ROPE = 64
B_V = 128
KV_RANK = 512
B_SCALE = (B_NOPE + B_ROPE) ** -0.5
B_QK_PAD = 256
A_Q_W = A_HEADS * HEAD_DIM
A_KV_W = A_KV_HEADS * HEAD_DIM
B_Q_W = B_HEADS * (B_NOPE + B_ROPE)
CHUNK = 128
C_GROUPS = 8
C_GROUP_W = D_MODEL // C_GROUPS
FFN_HIDDEN = 5632
LOG2E = 1.4426950408889634
LANES = 128

Z_QA = 0
Z_KA = Z_QA + A_Q_W
Z_VA = Z_KA + A_KV_W
Z_QB = Z_VA + A_KV_W
Z_CKV = Z_QB + B_HEADS * B_QK_PAD
Z_KR = Z_CKV + KV_RANK
Z_W = Z_KR + LANES

MOD_ROWS = 8
_NT_DIMS = (((1,), (1,)), ((), ()))

TM_FFN, TC_FFN = 1024, 512
TM_PROJ = 512
TM_ATTN_OUT = 512
TM_CMLP = 512
TN_ADA = 1024
TQ_ATTN = 256
UNROLL_GQA, UNROLL_MLA = 8, 4


def _cparams(sem, vmem_mib):
    return pltpu.CompilerParams(dimension_semantics=sem, vmem_limit_bytes=vmem_mib << 20)


def _layer_resident(shape, layer):
    n = len(shape)
    return pl.BlockSpec((None,) + tuple(shape), lambda *_: (layer,) + (0,) * n, pipeline_mode=pl.Buffered(1))


def _layer_row(width, layer):
    return pl.BlockSpec((None, 1, width), lambda *_: (layer, 0, 0))


def _mod_spec(layer, row0):
    return pl.BlockSpec((None, None, 6, D_MODEL), lambda b, *_: (layer, row0 + b, 0, 0))


def _rms(x):
    return x * lax.rsqrt(jnp.mean(x * x, axis=-1, keepdims=True) + EPS)


def _modulate(x, g, shift, scale):
    return (_rms(x) * g) * (1.0 + scale) + shift


def _swap_halves(v, n):
    lane = lax.broadcasted_iota(jnp.int32, v.shape, 1)
    fwd = pltpu.roll(v, LANES - n, 1)
    bwd = pltpu.roll(v, n, 1)
    return jnp.where((lane & (2 * n - 1)) < n, fwd, bwd)


def _ada_kernel(cond_ref, w_ref, b_ref, o_ref):
    c = cond_ref[...]
    a = (c * jax.nn.sigmoid(c)).astype(BF16)
    o_ref[...] = jnp.dot(a, w_ref[...].astype(BF16), preferred_element_type=F32) + b_ref[...]


def _ada_mod(cond, ada_w, ada_b):
    n6 = 6 * D_MODEL
    out = pl.pallas_call(
        _ada_kernel,
        grid=(DEPTH, n6 // TN_ADA),
        in_specs=[
            pl.BlockSpec((MOD_ROWS, D_MODEL), lambda l, n: (0, 0)),
            pl.BlockSpec((None, D_MODEL, TN_ADA), lambda l, n: (l, 0, n)),
            pl.BlockSpec((None, 1, TN_ADA), lambda l, n: (l, 0, n)),
        ],
        out_specs=pl.BlockSpec((None, MOD_ROWS, TN_ADA), lambda l, n: (l, 0, n)),
        out_shape=jax.ShapeDtypeStruct((DEPTH, MOD_ROWS, n6), F32),
        compiler_params=_cparams(("arbitrary", "arbitrary"), 40),
        name="ada_mod",
    )(cond, ada_w, ada_b.reshape(DEPTH, 1, n6))
    return out.reshape(DEPTH, MOD_ROWS, 6, D_MODEL)


def _ffn_kernel(*refs, n_chunks, final):
    if final:
        x_hbm, xb_ref, mod_ref, wg_ref, wu_ref, wd_ref, fn_ref, o_ref, x_buf, x_sem = refs
    else:
        x_hbm, xb_ref, mod_ref, wg_ref, wu_ref, wd_ref, o_ref, x_buf, x_sem = refs
    b, i, j = pl.program_id(0), pl.program_id(1), pl.program_id(2)
    tm = x_buf.shape[0]
    x_copy = pltpu.make_async_copy(x_hbm.at[b, pl.ds(pl.multiple_of(i * tm, tm), tm), :], x_buf, x_sem)

    @pl.when(j == 0)
    def _():
        x_copy.start()
        o_ref[...] = jnp.zeros_like(o_ref)

    xb = xb_ref[...]
    gate = jnp.dot(xb, wg_ref[...], preferred_element_type=F32)
    up = jnp.dot(xb, wu_ref[...], preferred_element_type=F32)
    h = (gate * jax.nn.sigmoid(gate) * up).astype(BF16)
    o_ref[...] += jnp.dot(h, wd_ref[...], preferred_element_type=F32)

    @pl.when(j == n_chunks - 1)
    def _():
        x_copy.wait()
        y = x_buf[...] + mod_ref[5:6, :] * o_ref[...]
        if final:
            y = _rms(y) * fn_ref[...]
        o_ref[...] = y


def _ffn(x, xb, mods, row0, layer, wg, wu, wd, final_g=None):
    nb, s, d = x.shape
    tm, tc = TM_FFN, TC_FFN
    n_chunks = FFN_HIDDEN // tc
    final = final_g is not None
    in_specs = [
        pl.BlockSpec(memory_space=pl.ANY),
        pl.BlockSpec((None, tm, d), lambda b, i, j: (b, i, 0)),
        _mod_spec(layer, row0),
        pl.BlockSpec((d, tc), lambda b, i, j: (0, j)),
        pl.BlockSpec((d, tc), lambda b, i, j: (0, j)),
        pl.BlockSpec((tc, d), lambda b, i, j: (j, 0)),
    ]
    args = [x, xb, mods, wg, wu, wd]
    if final:
        in_specs.append(pl.BlockSpec((1, d), lambda b, i, j: (0, 0)))
        args.append(final_g)
    return pl.pallas_call(
        functools.partial(_ffn_kernel, n_chunks=n_chunks, final=final),
        grid=(nb, s // tm, n_chunks),
        in_specs=in_specs,
        out_specs=pl.BlockSpec((None, tm, d), lambda b, i, j: (b, i, 0)),
        out_shape=jax.ShapeDtypeStruct((nb, s, d), F32),
        scratch_shapes=[pltpu.VMEM((tm, d), F32), pltpu.SemaphoreType.DMA(())],
        compiler_params=_cparams(("arbitrary", "arbitrary", "arbitrary"), 60),
        name="ffn_final" if final else "ffn",
    )(*args)


class _AttnWeights(NamedTuple):
    main: jax.Array
    va_t: jax.Array
    kv_k: jax.Array
    kv_vt: jax.Array


def _layout_attn_weights(w_in, w_kv_up):
    n_l, d, _ = w_in.shape
    o = A_Q_W + 2 * A_KV_W
    qb = w_in[:, :, o:o + B_Q_W].reshape(n_l, d, B_HEADS, B_NOPE + B_ROPE)
    qb = jnp.pad(qb, ((0, 0), (0, 0), (0, 0), (0, B_QK_PAD - B_NOPE - B_ROPE)))
    ckv = w_in[:, :, o + B_Q_W:o + B_Q_W + KV_RANK]
    kr = jnp.pad(w_in[:, :, o + B_Q_W + KV_RANK:], ((0, 0), (0, 0), (0, LANES - B_ROPE)))
    main = jnp.concatenate([w_in[:, :, :o], qb.reshape(n_l, d, B_HEADS * B_QK_PAD), ckv, kr], axis=2)
    kv4 = w_kv_up.reshape(n_l, KV_RANK, B_HEADS, B_NOPE + B_V)
    return _AttnWeights(
        main=main.astype(BF16),
        va_t=jnp.swapaxes(w_in[:, :, Z_VA:Z_VA + A_KV_W], 1, 2).astype(BF16),
        kv_k=kv4[..., :B_NOPE].reshape(n_l, KV_RANK, -1).astype(BF16),
        kv_vt=jnp.swapaxes(kv4[..., B_NOPE:].reshape(n_l, KV_RANK, -1), 1, 2).astype(BF16))


def _proj_kernel(*refs, prompt):
    if prompt:
        (x_ref, mod_ref, g_ref, w_ref, wvat_ref, wkv_ref, wkvt_ref, qn_ref, kn_ref, kvn_ref,
         qa_o, ka_o, vat_o, qb_o, kb_o, vbt_o, sk_o, sv_o, sc_o, sr_o) = refs
    else:
        (x_ref, mod_ref, g_ref, w_ref, wvat_ref, wkv_ref, wkvt_ref, qn_ref, kn_ref, kvn_ref,
         c128_ref, s128_ref, c64_ref, s64_ref,
         qa_o, ka_o, vat_o, qb_o, kb_o, vbt_o) = refs

    half = x_ref.shape[0] // 2
    for r in range(2):
        rows = slice(r * half, (r + 1) * half)
        if not prompt:
            c128, s128 = c128_ref[rows, :], s128_ref[rows, :]
            c64, s64 = c64_ref[rows, :], s64_ref[rows, :]

        def rope128(v):
            return v if prompt else v * c128 + _swap_halves(v, 32) * s128

        def rope64(v):
            return v if prompt else v * c64 + _swap_halves(v, 16) * s64

        xb = _modulate(x_ref[rows, :], g_ref[...], mod_ref[0:1, :], mod_ref[1:2, :]).astype(BF16)
        z = jnp.dot(xb, w_ref[...], preferred_element_type=F32)
        vat_o[:, rows] = lax.dot_general(wvat_ref[...], xb, _NT_DIMS, preferred_element_type=F32).astype(BF16)

        for h in range(A_HEADS):
            q = _rms(z[:, Z_QA + h * HEAD_DIM:Z_QA + (h + 1) * HEAD_DIM]) * qn_ref[...]
            qa_o[h, rows, :] = (rope128(q) * (A_SCALE * LOG2E)).astype(BF16)
        for h in range(A_KV_HEADS):
            sl = slice(h * HEAD_DIM, (h + 1) * HEAD_DIM)
            k = _rms(z[:, Z_KA + h * HEAD_DIM:Z_KA + (h + 1) * HEAD_DIM]) * kn_ref[...]
            if prompt:
                sk_o[rows, sl] = k
            ka_o[rows, sl] = rope128(k).astype(BF16)
        if prompt:
            sv_o[rows, :] = z[:, Z_VA:Z_VA + A_KV_W]
        for h in range(B_HEADS):
            base = Z_QB + h * B_QK_PAD
            qb_o[h, rows, :B_NOPE] = (z[:, base:base + B_NOPE] * (B_SCALE * LOG2E)).astype(BF16)
            qb_o[h, rows, B_NOPE:] = (
                rope64(z[:, base + B_NOPE:base + B_QK_PAD]) * (B_SCALE * LOG2E)).astype(BF16)

        ckv = _rms(z[:, Z_CKV:Z_CKV + KV_RANK]) * kvn_ref[...]
        kr = z[:, Z_KR:Z_KR + LANES]
        if prompt:
            sc_o[rows, :] = ckv
            sr_o[rows, :] = kr[:, :B_ROPE]
        krb = rope64(kr).astype(BF16)
        ckvb = ckv.astype(BF16)
        kn_all = jnp.dot(ckvb, wkv_ref[...], preferred_element_type=F32)
        for h in range(B_HEADS):
            kb_o[h, rows, :B_NOPE] = kn_all[:, h * B_NOPE:(h + 1) * B_NOPE].astype(BF16)
            kb_o[h, rows, B_NOPE:] = krb
        vbt_o[:, rows] = lax.dot_general(wkvt_ref[...], ckvb, _NT_DIMS, preferred_element_type=F32).astype(BF16)


def _attn_project(x, mods, row0, layer, attn_layer, g, w, qn, kn, kvn, rope=None):
    nb, s, d = x.shape
    tm = TM_PROJ
    prompt = rope is None

    def row(wd):
        return pl.BlockSpec((None, tm, wd), lambda b, i: (b, i, 0))

    def col(wd):
        return pl.BlockSpec((None, wd, tm), lambda b, i: (b, 0, i))

    def heads(n, wd):
        return pl.BlockSpec((None, n, tm, wd), lambda b, i: (b, 0, i, 0))

    in_specs = [
        row(d),
        _mod_spec(layer, row0),
        _layer_row(d, layer),
        _layer_resident((d, Z_W), attn_layer),
        _layer_resident((A_KV_W, d), attn_layer),
        _layer_resident((KV_RANK, B_HEADS * B_NOPE), attn_layer),
        _layer_resident((B_HEADS * B_V, KV_RANK), attn_layer),
        _layer_row(HEAD_DIM, attn_layer),
        _layer_row(HEAD_DIM, attn_layer),
        _layer_row(KV_RANK, attn_layer),
    ]
    args = [x, mods, g, w.main, w.va_t, w.kv_k, w.kv_vt, qn, kn, kvn]
    out_specs = [heads(A_HEADS, HEAD_DIM), row(A_KV_W), col(A_KV_W), heads(B_HEADS, B_QK_PAD),
                 heads(B_HEADS, B_QK_PAD), col(B_HEADS * B_V)]
    out_shape = [jax.ShapeDtypeStruct(shp, BF16) for shp in (
        (nb, A_HEADS, s, HEAD_DIM), (nb, s, A_KV_W), (nb, A_KV_W, s), (nb, B_HEADS, s, B_QK_PAD),
        (nb, B_HEADS, s, B_QK_PAD), (nb, B_HEADS * B_V, s))]
    if prompt:
        for wd in (A_KV_W, A_KV_W, KV_RANK, B_ROPE):
            out_specs.append(row(wd))
            out_shape.append(jax.ShapeDtypeStruct((nb, s, wd), F32))
    else:
        in_specs += [pl.BlockSpec((tm, LANES), lambda b, i: (i, 0))] * 4
        args += list(rope)
    return pl.pallas_call(
        functools.partial(_proj_kernel, prompt=prompt),
        grid=(nb, s // tm),
        in_specs=in_specs,
        out_specs=out_specs,
        out_shape=out_shape,
        compiler_params=_cparams(("parallel", "parallel"), 60),
        name="attn_proj_prompt" if prompt else "attn_proj_latent",
    )(*args)


def _ctx_kv_kernel(ckv_ref, kr_ref, wkv_ref, wkvt_ref, kb_o, vbt_o):
    ckvb = ckv_ref[...].astype(BF16)
    kn_all = jnp.dot(ckvb, wkv_ref[...], preferred_element_type=F32)
    krb = kr_ref[...].astype(BF16)
    zero = jnp.zeros((krb.shape[0], B_QK_PAD - B_NOPE - B_ROPE), BF16)
    for h in range(B_HEADS):
        kb_o[h, :, :B_NOPE] = kn_all[:, h * B_NOPE:(h + 1) * B_NOPE].astype(BF16)
        kb_o[h, :, B_NOPE:B_NOPE + B_ROPE] = krb
        kb_o[h, :, B_NOPE + B_ROPE:] = zero
    vbt_o[...] = lax.dot_general(wkvt_ref[...], ckvb, _NT_DIMS, preferred_element_type=F32).astype(BF16)


def _ctx_kv_expand(cache_ckv, cache_kr, attn_layer, w):
    nb, _, p, _ = cache_ckv.shape
    return pl.pallas_call(
        _ctx_kv_kernel,
        grid=(nb,),
        in_specs=[
            pl.BlockSpec((None, None, p, KV_RANK), lambda b: (b, attn_layer, 0, 0)),
            pl.BlockSpec((None, None, p, B_ROPE), lambda b: (b, attn_layer, 0, 0)),
            pl.BlockSpec((None, KV_RANK, B_HEADS * B_NOPE), lambda b: (attn_layer, 0, 0)),
            pl.BlockSpec((None, B_HEADS * B_V, KV_RANK), lambda b: (attn_layer, 0, 0)),
        ],
        out_specs=[
            pl.BlockSpec((None, B_HEADS, p, B_QK_PAD), lambda b: (b, 0, 0, 0)),
            pl.BlockSpec((None, B_HEADS * B_V, p), lambda b: (b, 0, 0)),
        ],
        out_shape=[
            jax.ShapeDtypeStruct((nb, B_HEADS, p, B_QK_PAD), BF16),
            jax.ShapeDtypeStruct((nb, B_HEADS * B_V, p), BF16),
        ],
        compiler_params=_cparams(("parallel",), 32),
        name="ctx_kv_expand",
    )(cache_ckv, cache_kr, w.kv_k, w.kv_vt)


KV_LANES = "lanes"
KV_HEADS = "heads"
KV_CACHE = "cache"


def _attn_kernel(*refs, seg_len, seg_layout, n_cast, unroll, hb, group, dq, dv, tq, rolled):
    nseg = len(seg_len)
    q_ref = refs[0]
    kv_refs = refs[1:1 + 2 * nseg]
    cast_in = refs[1 + 2 * nseg:1 + 2 * nseg + n_cast]
    o_ref = refs[1 + 2 * nseg + n_cast]
    cast_out = refs[2 + 2 * nseg + n_cast:2 + 2 * nseg + 2 * n_cast]
    s_ref, m_ref = refs[2 + 2 * nseg + 2 * n_cast:]
    for src, dst in zip(cast_in, cast_out):
        dst[...] = src[...].astype(BF16)
    n_sub = q_ref.shape[1] // tq
    n_items = hb * n_sub
    seg_off = [sum(seg_len[:i]) for i in range(nseg)]
    dynamic_kv = all(lay == KV_HEADS for lay in seg_layout)

    def split(i):
        if isinstance(i, int):
            return i // n_sub, (i % n_sub) * tq
        shift = n_sub.bit_length() - 1
        return lax.shift_right_logical(i, shift), pl.multiple_of((i & (n_sub - 1)) * tq, tq)

    def kv_head(h):
        if isinstance(h, int):
            return h // group
        return lax.shift_right_logical(h, group.bit_length() - 1) if dynamic_kv else 0

    def keys(s, kvh):
        ref = kv_refs[2 * s]
        k = ref[kvh] if seg_layout[s] == KV_HEADS else ref[:, kvh * dq:(kvh + 1) * dq]
        return k.astype(BF16)

    def values_t(s, kvh):
        ref = kv_refs[2 * s + 1]
        if seg_layout[s] == KV_HEADS:
            vt = ref[kvh]
        elif seg_layout[s] == KV_LANES:
            vt = ref[kvh * dv:(kvh + 1) * dv, :]
        else:
            vt = ref[:, kvh * dv:(kvh + 1) * dv].T
        return vt.astype(BF16)

    def score(i, slot):
        h, r = split(i)
        kvh = kv_head(h)
        q = q_ref[h, pl.ds(r, tq), :]
        m = None
        for s in range(nseg):
            st = lax.dot_general(keys(s, kvh), q, _NT_DIMS, preferred_element_type=F32)
            s_ref[slot, seg_off[s]:seg_off[s] + seg_len[s], :] = st
            ms = jnp.max(st, axis=0, keepdims=True)
            m = ms if m is None else jnp.maximum(m, ms)
        m_ref[slot] = m

    def finish(i, slot):
        h, r = split(i)
        kvh = kv_head(h)
        m = m_ref[slot]
        denom = None
        acc = None
        for s in range(nseg):
            p = jnp.exp2(s_ref[slot, seg_off[s]:seg_off[s] + seg_len[s], :] - m)
            ls = jnp.sum(p, axis=0, keepdims=True)
            os_ = jnp.dot(values_t(s, kvh), p.astype(BF16), preferred_element_type=F32)
            denom = ls if denom is None else denom + ls
            acc = os_ if acc is None else acc + os_
        o_ref[h, pl.ds(r, tq), :] = (acc / denom).T.astype(o_ref.dtype)

    if rolled:
        assert dynamic_kv or hb == group
        assert unroll % 2 == 0 and n_items % unroll == 0 and n_items >= 2 * unroll
        assert n_sub & (n_sub - 1) == 0 and group & (group - 1) == 0
        score(0, 0)

        def body(k, carry):
            for u in range(unroll):
                score(unroll * k + u + 1, (u + 1) % 2)
                finish(unroll * k + u, u % 2)
            return carry

        lax.fori_loop(0, n_items // unroll - 1, body, 0)
        for u in range(unroll - 1):
            score(n_items - unroll + u + 1, (u + 1) % 2)
            finish(n_items - unroll + u, u % 2)
        finish(n_items - 1, (unroll - 1) % 2)
    else:
        score(0, 0)
        for i in range(n_items):
            if i + 1 < n_items:
                score(i + 1, (i + 1) % 2)
            finish(i, i % 2)


def _attention(q, segs, *, seq_len, hb, group, dq, tqb, tq, rolled, unroll=2, seg_layer=None, cast=()):
    nbq, n_heads, s_tot, _ = q.shape
    dv = HEAD_DIM
    per = s_tot // seq_len
    n_qb = seq_len // tqb
    nkv = hb // group
    in_specs = [pl.BlockSpec((None, hb, tqb, dq), lambda g, h, t: (g // per, h, (g % per) * n_qb + t, 0))]
    args = [q]
    seg_lens = []
    for k, v, layout in segs:
        if layout == KV_CACHE:
            sk = k.shape[2]
            in_specs.append(pl.BlockSpec((None, None, sk, nkv * dq), lambda g, h, t: (g, seg_layer, 0, h)))
            in_specs.append(pl.BlockSpec((None, None, sk, nkv * dv), lambda g, h, t: (g, seg_layer, 0, h)))
        elif layout == KV_LANES:
            sk = k.shape[1] // per
            in_specs.append(pl.BlockSpec((None, sk, nkv * dq), lambda g, h, t: (g // per, g % per, h)))
            in_specs.append(pl.BlockSpec((None, nkv * dv, sk), lambda g, h, t: (g // per, h, g % per)))
        else:
            sk = k.shape[2] // per
            in_specs.append(pl.BlockSpec((None, nkv, sk, dq), lambda g, h, t: (g // per, h, g % per, 0)))
            in_specs.append(pl.BlockSpec((None, nkv, dv, sk), lambda g, h, t: (g // per, h, 0, g % per)))
        seg_lens.append(sk)
        args += [k, v]
    grid = (nbq * per, n_heads // hb, n_qb)
    out_specs = [pl.BlockSpec((None, hb, tqb, dv), lambda g, h, t: (g // per, h, (g % per) * n_qb + t, 0))]
    out_shape = [jax.ShapeDtypeStruct((nbq, n_heads, s_tot, dv), BF16)]
    n_steps = grid[0] * grid[1] * grid[2]
    for a, first_row, n_rows in cast:
        slab = n_rows // n_steps
        assert n_rows % n_steps == 0 and first_row % slab == 0
        in_specs.append(pl.BlockSpec(
            (slab, a.shape[1]),
            lambda g, h, t, b0=first_row // slab: (b0 + (g * grid[1] + h) * grid[2] + t, 0)))
        out_specs.append(pl.BlockSpec((slab, a.shape[1]), lambda g, h, t: ((g * grid[1] + h) * grid[2] + t, 0)))
        out_shape.append(jax.ShapeDtypeStruct((n_rows, a.shape[1]), BF16))
        args.append(a)
    outs = pl.pallas_call(
        functools.partial(_attn_kernel, seg_len=tuple(seg_lens), seg_layout=tuple(s[2] for s in segs),
                          n_cast=len(cast), unroll=unroll, hb=hb, group=group, dq=dq, dv=dv, tq=tq,
                          rolled=rolled),
        grid=grid,
        in_specs=in_specs,
        out_specs=out_specs,
        out_shape=out_shape,
        scratch_shapes=[pltpu.VMEM((2, sum(seg_lens), tq), F32), pltpu.VMEM((2, 1, tq), F32)],
        compiler_params=_cparams(("parallel", "parallel", "arbitrary"), 56),
        name="attention",
    )(*args)
    return (outs[0], tuple(outs[1:])) if cast else outs[0]


def _close_mixer(x, y, mod_ref, gf_ref, o_ref, xb_o, rows=slice(None)):
    x_new = x + mod_ref[2:3, :] * y
    o_ref[rows, :] = x_new
    xb_o[rows, :] = _modulate(x_new, gf_ref[...], mod_ref[3:4, :], mod_ref[4:5, :]).astype(BF16)


def _attn_out_kernel(x_ref, mod_ref, gf_ref, oa_ref, ob_ref, w_ref, o_ref, xb_o):
    half = x_ref.shape[0] // 2
    for r in range(2):
        rows = slice(r * half, (r + 1) * half)
        cat = jnp.concatenate([oa_ref[h, rows, :] for h in range(A_HEADS)]
                              + [ob_ref[h, rows, :] for h in range(B_HEADS)], axis=-1)
        y = jnp.dot(cat, w_ref[...], preferred_element_type=F32)
        _close_mixer(x_ref[rows, :], y, mod_ref, gf_ref, o_ref, xb_o, rows)


def _attn_out(x, mods, row0, layer, attn_layer, g_ffn, oa, ob, w_out):
    nb, s, d = x.shape
    tm = TM_ATTN_OUT
    tile = pl.BlockSpec((None, tm, d), lambda b, i: (b, i, 0))
    return pl.pallas_call(
        _attn_out_kernel,
        grid=(nb, s // tm),
        in_specs=[
            tile,
            _mod_spec(layer, row0),
            _layer_row(d, layer),
            pl.BlockSpec((None, A_HEADS, tm, HEAD_DIM), lambda b, i: (b, 0, i, 0)),
            pl.BlockSpec((None, B_HEADS, tm, B_V), lambda b, i: (b, 0, i, 0)),
            _layer_resident((A_Q_W + B_HEADS * B_V, d), attn_layer),
        ],
        out_specs=[tile, tile],
        out_shape=[jax.ShapeDtypeStruct((nb, s, d), F32), jax.ShapeDtypeStruct((nb, s, d), BF16)],
        compiler_params=_cparams(("parallel", "parallel"), 44),
        name="attn_out",
    )(x, mods, g_ffn, oa, ob, w_out)


def _gelu_tanh(x):
    return 0.5 * x * (1.0 + jnp.tanh(0.7978845608028654 * (x + 0.044715 * (x * x * x))))


def _cmlp_kernel(x_ref, mod_ref, g_ref, gf_ref, win_ref, vn_ref, ws_ref, bias_ref, wout_ref, o_ref, xb_o, t_ref):
    half = x_ref.shape[0] // 2
    for r in range(2):
        x = x_ref[r * half:(r + 1) * half, :]
        xb = _modulate(x, g_ref[...], mod_ref[0:1, :], mod_ref[1:2, :]).astype(BF16)
        v = _gelu_tanh(jnp.dot(xb, win_ref[:, D_MODEL:], preferred_element_type=F32))
        vb = (_rms(v) * vn_ref[...]).astype(BF16)
        u = _gelu_tanh(jnp.dot(xb, win_ref[:, :D_MODEL], preferred_element_type=F32))
        for n in range(half // CHUNK):
            rows = slice(n * CHUNK, (n + 1) * CHUNK)
            trows = slice(r * half + n * CHUNK, r * half + (n + 1) * CHUNK)
            for g in range(C_GROUPS):
                cols = slice(g * C_GROUP_W, (g + 1) * C_GROUP_W)
                sv = jnp.dot(ws_ref[g], vb[rows, cols], preferred_element_type=F32) + bias_ref[:, cols]
                t_ref[trows, cols] = (u[rows, cols] * sv).astype(BF16)
        hrows = slice(r * half, (r + 1) * half)
        y = jnp.dot(t_ref[hrows, :], wout_ref[...], preferred_element_type=F32)
        _close_mixer(x, y, mod_ref, gf_ref, o_ref, xb_o, hrows)


def _chunk_mlp(x, mods, row0, layer, c_layer, g, g_ffn, w_in, v_norm, w_s, bias, w_out):
    nb, s, d = x.shape
    tm = TM_CMLP
    tile = pl.BlockSpec((None, tm, d), lambda b, i: (b, i, 0))
    return pl.pallas_call(
        _cmlp_kernel,
        grid=(nb, s // tm),
        in_specs=[
            tile,
            _mod_spec(layer, row0),
            _layer_row(d, layer),
            _layer_row(d, layer),
            _layer_resident((d, 2 * d), c_layer),
            _layer_row(d, c_layer),
            _layer_resident((C_GROUPS, CHUNK, CHUNK), c_layer),
            _layer_resident((CHUNK, d), c_layer),
            _layer_resident((d, d), c_layer),
        ],
        out_specs=[tile, tile],
        out_shape=[jax.ShapeDtypeStruct((nb, s, d), F32), jax.ShapeDtypeStruct((nb, s, d), BF16)],
        scratch_shapes=[pltpu.VMEM((tm, d), BF16)],
        compiler_params=_cparams(("parallel", "parallel"), 60),
        name="chunk_mlp",
    )(x, mods, g, g_ffn, w_in, v_norm, w_s, bias, w_out)


def _rope_tables(n_tokens):
    t = np.arange(n_tokens)
    row = (t // GRID_W).astype(np.float32)
    col = (t % GRID_W).astype(np.float32)

    def axis_tables(pos, half):
        inv = (1.0 / (ROPE_BASE ** (np.arange(0, half, 2, dtype=np.float32) / half))).astype(np.float32)
        ang = (pos[:, None] * inv).astype(np.float32).astype(np.float64)
        c, s = np.cos(ang), np.sin(ang)
        return np.concatenate([c, c], axis=1), np.concatenate([-s, s], axis=1)

    def tables(dim):
        cr, sr = axis_tables(row, dim // 2)
        cc, sc = axis_tables(col, dim // 2)
        return np.concatenate([cr, cc], axis=1), np.concatenate([sr, sc], axis=1)

    c128, s128 = tables(HEAD_DIM)
    c64, s64 = tables(B_ROPE)
    pad = np.zeros((n_tokens, LANES - B_ROPE))
    c64 = np.concatenate([c64, pad], axis=1)
    s64 = np.concatenate([s64, pad], axis=1)
    return tuple(jnp.asarray(a, F32) for a in (c128, s128, c64, s64))


def kernel(x_prompt, x_sample, c, cache_gqa_k, cache_gqa_v, cache_mla_ckv, cache_mla_krope, c_ctx, ada_w, ada_b, norm_mix, norm_ffn, ffn_gate, ffn_up, ffn_down, attn_w_in, attn_q_norm, attn_k_norm, attn_kv_norm, attn_w_kv_up, attn_w_out, cmlp_w_in, cmlp_v_norm, cmlp_w_s, cmlp_b_s, cmlp_w_out, final_norm):
    n_p, s_p, d = x_prompt.shape
    n_s, s_s, _ = x_sample.shape
    past = cache_gqa_k.shape[2]

    cond = jnp.concatenate([c_ctx[None, :], c, jnp.zeros((MOD_ROWS - 1 - n_s, d), F32)], axis=0)
    mods = _ada_mod(cond, ada_w, ada_b)

    def rows(a):
        return a.reshape(a.shape[0], 1, a.shape[1])

    def heads_t(vt):
        return vt.reshape(vt.shape[0], B_HEADS, B_V, vt.shape[2])

    aw = _layout_attn_weights(attn_w_in, attn_w_kv_up)
    g_mix, g_ffn = rows(norm_mix), rows(norm_ffn)
    qn, kn, kvn = rows(attn_q_norm), rows(attn_k_norm), rows(attn_kv_norm)
    cw_s = cmlp_w_s.astype(BF16)
    c_vn = rows(cmlp_v_norm)

    def flat(a):
        return a.reshape(-1, a.shape[-1])

    def ffn_cast(layer):
        return [(flat(w), layer * w.shape[1], w.shape[1]) for w in (ffn_gate, ffn_up, ffn_down)]

    ffn_w = {}
    n_al, n_cl = attn_w_out.shape[0], cmlp_w_in.shape[0]
    whole = [(flat(w), 0, w.shape[0] * w.shape[1]) for w in (attn_w_out, cmlp_w_in, cmlp_w_out)]
    c_bias = jnp.repeat(jnp.swapaxes(cmlp_b_s, 1, 2), C_GROUP_W, axis=2)
    final_g = final_norm.reshape(1, d)

    xp = x_prompt.reshape(1, n_p * s_p, d)
    xs = x_sample
    rope = _rope_tables(s_s)
    cache_k = cache_gqa_k.reshape(n_s, -1, past, A_KV_W)
    cache_v = cache_gqa_v.reshape(n_s, -1, past, A_KV_W)

    states = ([], [], [], [])
    for i in range(DEPTH):
        j = i // 2
        if i % 2 == 0:
            qa, ka, vat, qb, kb, vbt = _attn_project(xs, mods, 1, i, j, g_mix, aw, qn, kn, kvn, rope=rope)
            kb_ctx, vbt_ctx = _ctx_kv_expand(cache_mla_ckv, cache_mla_krope, j, aw)
            oa, ffn_w[i] = _attention(
                qa, [(ka, vat, KV_LANES), (cache_k, cache_v, KV_CACHE)], seq_len=s_s, hb=A_GROUP, group=A_GROUP,
                dq=HEAD_DIM, tqb=2048, tq=TQ_ATTN, rolled=True, unroll=UNROLL_GQA, seg_layer=j, cast=ffn_cast(i))
            ob, cast_b = _attention(
                qb, [(kb, heads_t(vbt), KV_HEADS), (kb_ctx, heads_t(vbt_ctx), KV_HEADS)], seq_len=s_s, hb=1,
                group=1, dq=B_QK_PAD, tqb=4096, tq=TQ_ATTN, rolled=True, unroll=UNROLL_MLA,
                cast=ffn_cast(i + 1) + (whole if i == 0 else []))
            ffn_w[i + 1] = cast_b[:3]
            if i == 0:
                a_out = cast_b[3].reshape(n_al, -1, d)
                cw_in = cast_b[4].reshape(n_cl, d, -1)
                cw_out = cast_b[5].reshape(n_cl, -1, d)
            xs, xsb = _attn_out(xs, mods, 1, i, j, g_ffn, oa, ob, a_out)

            qa, ka, vat, qb, kb, vbt, st_k, st_v, st_c, st_r = _attn_project(
                xp, mods, 0, i, j, g_mix, aw, qn, kn, kvn)
            for lst, st in zip(states, (st_k, st_v, st_c, st_r)):
                lst.append(st)
            oa = _attention(qa, [(ka, vat, KV_LANES)], seq_len=s_p, hb=A_HEADS, group=A_GROUP, dq=HEAD_DIM,
                            tqb=s_p, tq=s_p, rolled=False)
            ob = _attention(qb, [(kb, heads_t(vbt), KV_HEADS)], seq_len=s_p, hb=B_HEADS, group=1, dq=B_QK_PAD,
                            tqb=s_p, tq=s_p, rolled=False)
            xp, xpb = _attn_out(xp, mods, 0, i, j, g_ffn, oa, ob, a_out)
        else:
            xp, xpb = _chunk_mlp(xp, mods, 0, i, j, g_mix, g_ffn, cw_in, c_vn, cw_s, c_bias, cw_out)
            xs, xsb = _chunk_mlp(xs, mods, 1, i, j, g_mix, g_ffn, cw_in, c_vn, cw_s, c_bias, cw_out)

        fin = final_g if i == DEPTH - 1 else None
        xp = _ffn(xp, xpb, mods, 0, i, *ffn_w[i], fin)
        xs = _ffn(xs, xsb, mods, 1, i, *ffn_w[i], fin)

    y_prompt = xp.reshape(n_p, s_p, d)
    st_k, st_v, st_c, st_r = (jnp.stack([a.reshape(n_p, s_p, a.shape[-1]) for a in lst], axis=1)
                              for lst in states)
    return (y_prompt, xs,
            st_k.reshape(n_p, -1, s_p, A_KV_HEADS, HEAD_DIM),
            st_v.reshape(n_p, -1, s_p, A_KV_HEADS, HEAD_DIM),
            st_c, st_r)
```

```python
import functools
from typing import NamedTuple

import numpy as np
import jax
import jax.numpy as jnp
from jax import lax
from jax.experimental import pallas as pl
from jax.experimental.pallas import tpu as pltpu

F32 = jnp.float32
BF16 = jnp.bfloat16

D_MODEL = 2048
DEPTH = 4
GRID_W = 64
ROPE_BASE = 10000.0
EPS = 1e-6
A_HEADS = 8
A_KV_HEADS = 2
A_GROUP = A_HEADS // A_KV_HEADS
HEAD_DIM = 128
A_SCALE = HEAD_DIM ** -0.5
B_HEADS = 8
B_NOPE = 128
B_ROPE = 64
B_V = 128
KV_RANK = 512
B_SCALE = (B_NOPE + B_ROPE) ** -0.5
B_QK_PAD = 256
A_Q_W = A_HEADS * HEAD_DIM
A_KV_W = A_KV_HEADS * HEAD_DIM
B_Q_W = B_HEADS * (B_NOPE + B_ROPE)
CHUNK = 128
C_GROUPS = 8
C_GROUP_W = D_MODEL // C_GROUPS
FFN_HIDDEN = 5632
LOG2E = 1.4426950408889634
LANES = 128

Z_QA = 0
Z_KA = Z_QA + A_Q_W
Z_VA = Z_KA + A_KV_W
Z_QB = Z_VA + A_KV_W
Z_CKV = Z_QB + B_HEADS * B_QK_PAD
Z_KR = Z_CKV + KV_RANK
Z_W = Z_KR + LANES

MOD_ROWS = 8
_NT_DIMS = (((1,), (1,)), ((), ()))

TM_FFN, TC_FFN = 1024, 512
TM_PROJ = 512
TM_ATTN_OUT = 512
TM_CMLP = 512
TN_ADA = 1024
TQ_ATTN = 256
UNROLL_GQA, UNROLL_MLA = 8, 4


def _cparams(sem, vmem_mib):
    return pltpu.CompilerParams(dimension_semantics=sem, vmem_limit_bytes=vmem_mib << 20)


def _layer_resident(shape, layer):
    n = len(shape)
    return pl.BlockSpec((None,) + tuple(shape), lambda *_: (layer,) + (0,) * n, pipeline_mode=pl.Buffered(1))


def _layer_row(width, layer):
    return pl.BlockSpec((None, 1, width), lambda *_: (layer, 0, 0))


def _mod_spec(layer, row0):
    return pl.BlockSpec((None, None, 6, D_MODEL), lambda b, *_: (layer, row0 + b, 0, 0))


def _rms(x):
    return x * lax.rsqrt(jnp.mean(x * x, axis=-1, keepdims=True) + EPS)


def _modulate(x, g, shift, scale):
    return (_rms(x) * g) * (1.0 + scale) + shift


def _swap_halves(v, n):
    lane = lax.broadcasted_iota(jnp.int32, v.shape, 1)
    fwd = pltpu.roll(v, LANES - n, 1)
    bwd = pltpu.roll(v, n, 1)
    return jnp.where((lane & (2 * n - 1)) < n, fwd, bwd)


def _ada_kernel(cond_ref, w_ref, b_ref, o_ref):
    c = cond_ref[...]
    a = (c * jax.nn.sigmoid(c)).astype(BF16)
    o_ref[...] = jnp.dot(a, w_ref[...].astype(BF16), preferred_element_type=F32) + b_ref[...]


def _ada_mod(cond, ada_w, ada_b):
    n6 = 6 * D_MODEL
    out = pl.pallas_call(
        _ada_kernel,
        grid=(DEPTH, n6 // TN_ADA),
        in_specs=[
            pl.BlockSpec((MOD_ROWS, D_MODEL), lambda l, n: (0, 0)),
            pl.BlockSpec((None, D_MODEL, TN_ADA), lambda l, n: (l, 0, n)),
            pl.BlockSpec((None, 1, TN_ADA), lambda l, n: (l, 0, n)),
        ],
        out_specs=pl.BlockSpec((None, MOD_ROWS, TN_ADA), lambda l, n: (l, 0, n)),
        out_shape=jax.ShapeDtypeStruct((DEPTH, MOD_ROWS, n6), F32),
        compiler_params=_cparams(("arbitrary", "arbitrary"), 40),
        name="ada_mod",
    )(cond, ada_w, ada_b.reshape(DEPTH, 1, n6))
    return out.reshape(DEPTH, MOD_ROWS, 6, D_MODEL)


def _ffn_kernel(*refs, n_chunks, final):
    if final:
        x_hbm, xb_ref, mod_ref, wg_ref, wu_ref, wd_ref, fn_ref, o_ref, x_buf, x_sem = refs
    else:
        x_hbm, xb_ref, mod_ref, wg_ref, wu_ref, wd_ref, o_ref, x_buf, x_sem = refs
    b, i, j = pl.program_id(0), pl.program_id(1), pl.program_id(2)
    tm = x_buf.shape[0]
    x_copy = pltpu.make_async_copy(x_hbm.at[b, pl.ds(pl.multiple_of(i * tm, tm), tm), :], x_buf, x_sem)

    @pl.when(j == 0)
    def _():
        x_copy.start()
        o_ref[...] = jnp.zeros_like(o_ref)

    xb = xb_ref[...]
    gate = jnp.dot(xb, wg_ref[...], preferred_element_type=F32)
    up = jnp.dot(xb, wu_ref[...], preferred_element_type=F32)
    h = (gate * jax.nn.sigmoid(gate) * up).astype(BF16)
    o_ref[...] += jnp.dot(h, wd_ref[...], preferred_element_type=F32)

    @pl.when(j == n_chunks - 1)
    def _():
        x_copy.wait()
        y = x_buf[...] + mod_ref[5:6, :] * o_ref[...]
        if final:
            y = _rms(y) * fn_ref[...]
        o_ref[...] = y


def _ffn(x, xb, mods, row0, layer, wg, wu, wd, final_g=None):
    nb, s, d = x.shape
    tm, tc = TM_FFN, TC_FFN
    n_chunks = FFN_HIDDEN // tc
    final = final_g is not None
    in_specs = [
        pl.BlockSpec(memory_space=pl.ANY),
        pl.BlockSpec((None, tm, d), lambda b, i, j: (b, i, 0)),
        _mod_spec(layer, row0),
        pl.BlockSpec((d, tc), lambda b, i, j: (0, j)),
        pl.BlockSpec((d, tc), lambda b, i, j: (0, j)),
        pl.BlockSpec((tc, d), lambda b, i, j: (j, 0)),
    ]
    args = [x, xb, mods, wg, wu, wd]
    if final:
        in_specs.append(pl.BlockSpec((1, d), lambda b, i, j: (0, 0)))
        args.append(final_g)
    return pl.pallas_call(
        functools.partial(_ffn_kernel, n_chunks=n_chunks, final=final),
        grid=(nb, s // tm, n_chunks),
        in_specs=in_specs,
        out_specs=pl.BlockSpec((None, tm, d), lambda b, i, j: (b, i, 0)),
        out_shape=jax.ShapeDtypeStruct((nb, s, d), F32),
        scratch_shapes=[pltpu.VMEM((tm, d), F32), pltpu.SemaphoreType.DMA(())],
        compiler_params=_cparams(("arbitrary", "arbitrary", "arbitrary"), 60),
        name="ffn_final" if final else "ffn",
    )(*args)


class _AttnWeights(NamedTuple):
    main: jax.Array
    va_t: jax.Array
    kv_k: jax.Array
    kv_vt: jax.Array


W_GROUP = 64


def _w_in_group_map():
    src = list(range(Z_QB // W_GROUP))
    per_head = (B_NOPE + B_ROPE) // W_GROUP
    for h in range(B_HEADS):
        first = (A_Q_W + 2 * A_KV_W) // W_GROUP + h * per_head
        src += list(range(first, first + per_head)) + [-1] * (B_QK_PAD // W_GROUP - per_head)
    first = (A_Q_W + 2 * A_KV_W + B_Q_W) // W_GROUP
    src += list(range(first, first + KV_RANK // W_GROUP))
    src += [first + KV_RANK // W_GROUP, -1]
    assert len(src) == Z_W // W_GROUP
    return src


def _w_in_layout_kernel(src_a, src_b, a_ref, b_ref, main_o):
    c = pl.program_id(1)
    top = jnp.where(src_a[c] >= 0, a_ref[...], 0.0)
    bot = jnp.where(src_b[c] >= 0, b_ref[...], 0.0)
    main_o[...] = jnp.concatenate([top, bot], axis=0).T.astype(BF16)


def _layout_attn_weights(w_in, w_kv_up):
    n_l, d, _ = w_in.shape
    w_t = jnp.swapaxes(w_in, 1, 2)
    group = _w_in_group_map()
    src_a = jnp.asarray(group[0::2], jnp.int32)
    src_b = jnp.asarray(group[1::2], jnp.int32)
    main = pl.pallas_call(
        _w_in_layout_kernel,
        grid_spec=pltpu.PrefetchScalarGridSpec(
            num_scalar_prefetch=2,
            grid=(n_l, Z_W // LANES),
            in_specs=[
                pl.BlockSpec((None, W_GROUP, d), lambda l, c, sa, sb: (l, jnp.maximum(sa[c], 0), 0)),
                pl.BlockSpec((None, W_GROUP, d), lambda l, c, sa, sb: (l, jnp.maximum(sb[c], 0), 0)),
            ],
            out_specs=pl.BlockSpec((None, d, LANES), lambda l, c, sa, sb: (l, 0, c)),
        ),
        out_shape=jax.ShapeDtypeStruct((n_l, d, Z_W), BF16),
        compiler_params=_cparams(("parallel", "parallel"), 32),
        name="w_in_layout",
    )(src_a, src_b, w_t, w_t)
    va_t = w_t[:, Z_VA:Z_VA + A_KV_W, :].astype(BF16)
    kv4 = w_kv_up.reshape(n_l, KV_RANK, B_HEADS, B_NOPE + B_V)
    return _AttnWeights(
        main=main,
        va_t=va_t,
        kv_k=kv4[..., :B_NOPE].reshape(n_l, KV_RANK, -1).astype(BF16),
        kv_vt=jnp.swapaxes(kv4[..., B_NOPE:].reshape(n_l, KV_RANK, -1), 1, 2).astype(BF16))


def _proj_kernel(*refs, prompt):
    if prompt:
        (x_ref, mod_ref, g_ref, w_ref, wvat_ref, wkv_ref, wkvt_ref, qn_ref, kn_ref, kvn_ref,
         qa_o, ka_o, vat_o, qb_o, kb_o, vbt_o, sk_o, sv_o, sc_o, sr_o) = refs
    else:
        (x_ref, mod_ref, g_ref, w_ref, wvat_ref, wkv_ref, wkvt_ref, qn_ref, kn_ref, kvn_ref,
         c128_ref, s128_ref, c64_ref, s64_ref,
         qa_o, ka_o, vat_o, qb_o, kb_o, vbt_o) = refs

    half = x_ref.shape[0] // 2
    for r in range(2):
        rows = slice(r * half, (r + 1) * half)
        if not prompt:
            c128, s128 = c128_ref[rows, :], s128_ref[rows, :]
            c64, s64 = c64_ref[rows, :], s64_ref[rows, :]

        def rope128(v):
            return v if prompt else v * c128 + _swap_halves(v, 32) * s128

        def rope64(v):
            return v if prompt else v * c64 + _swap_halves(v, 16) * s64

        xb = _modulate(x_ref[rows, :], g_ref[...], mod_ref[0:1, :], mod_ref[1:2, :]).astype(BF16)
        za = jnp.dot(xb, w_ref[:, :Z_QB if prompt else Z_VA], preferred_element_type=F32)
        zb = jnp.dot(xb, w_ref[:, Z_QB:], preferred_element_type=F32)
        vat_o[:, rows] = lax.dot_general(wvat_ref[...], xb, _NT_DIMS, preferred_element_type=F32).astype(BF16)

        for h in range(A_HEADS):
            q = _rms(za[:, Z_QA + h * HEAD_DIM:Z_QA + (h + 1) * HEAD_DIM]) * qn_ref[...]
            qa_o[h, rows, :] = (rope128(q) * (A_SCALE * LOG2E)).astype(BF16)
        for h in range(A_KV_HEADS):
            sl = slice(h * HEAD_DIM, (h + 1) * HEAD_DIM)
            k = _rms(za[:, Z_KA + h * HEAD_DIM:Z_KA + (h + 1) * HEAD_DIM]) * kn_ref[...]
            if prompt:
                sk_o[rows, sl] = k
            ka_o[rows, sl] = rope128(k).astype(BF16)
        if prompt:
            sv_o[rows, :] = za[:, Z_VA:Z_VA + A_KV_W]
        for h in range(B_HEADS):
            base = h * B_QK_PAD
            qb_o[h, rows, :B_NOPE] = (zb[:, base:base + B_NOPE] * (B_SCALE * LOG2E)).astype(BF16)
            qb_o[h, rows, B_NOPE:] = (
                rope64(zb[:, base + B_NOPE:base + B_QK_PAD]) * (B_SCALE * LOG2E)).astype(BF16)

        ckv = _rms(zb[:, Z_CKV - Z_QB:Z_CKV - Z_QB + KV_RANK]) * kvn_ref[...]
        kr = zb[:, Z_KR - Z_QB:Z_KR - Z_QB + LANES]
        if prompt:
            sc_o[rows, :] = ckv
            sr_o[rows, :] = kr[:, :B_ROPE]
        krb = rope64(kr).astype(BF16)
        ckvb = ckv.astype(BF16)
        kn_all = jnp.dot(ckvb, wkv_ref[...], preferred_element_type=F32)
        for h in range(B_HEADS):
            kb_o[h, rows, :B_NOPE] = kn_all[:, h * B_NOPE:(h + 1) * B_NOPE].astype(BF16)
            kb_o[h, rows, B_NOPE:] = krb
        vbt_o[:, rows] = lax.dot_general(wkvt_ref[...], ckvb, _NT_DIMS, preferred_element_type=F32).astype(BF16)


def _attn_project(x, mods, row0, layer, attn_layer, g, w, qn, kn, kvn, rope=None):
    nb, s, d = x.shape
    tm = TM_PROJ
    prompt = rope is None

    def row(wd):
        return pl.BlockSpec((None, tm, wd), lambda b, i: (b, i, 0))

    def col(wd):
        return pl.BlockSpec((None, wd, tm), lambda b, i: (b, 0, i))

    def heads(n, wd):
        return pl.BlockSpec((None, n, tm, wd), lambda b, i: (b, 0, i, 0))

    in_specs = [
        row(d),
        _mod_spec(layer, row0),
        _layer_row(d, layer),
        _layer_resident((d, Z_W), attn_layer),
        _layer_resident((A_KV_W, d), attn_layer),
        _layer_resident((KV_RANK, B_HEADS * B_NOPE), attn_layer),
        _layer_resident((B_HEADS * B_V, KV_RANK), attn_layer),
        _layer_row(HEAD_DIM, attn_layer),
        _layer_row(HEAD_DIM, attn_layer),
        _layer_row(KV_RANK, attn_layer),
    ]
    args = [x, mods, g, w.main, w.va_t, w.kv_k, w.kv_vt, qn, kn, kvn]
    out_specs = [heads(A_HEADS, HEAD_DIM), row(A_KV_W), col(A_KV_W), heads(B_HEADS, B_QK_PAD),
                 heads(B_HEADS, B_QK_PAD), col(B_HEADS * B_V)]
    out_shape = [jax.ShapeDtypeStruct(shp, BF16) for shp in (
        (nb, A_HEADS, s, HEAD_DIM), (nb, s, A_KV_W), (nb, A_KV_W, s), (nb, B_HEADS, s, B_QK_PAD),
        (nb, B_HEADS, s, B_QK_PAD), (nb, B_HEADS * B_V, s))]
    if prompt:
        for wd in (A_KV_W, A_KV_W, KV_RANK, B_ROPE):
            out_specs.append(row(wd))
            out_shape.append(jax.ShapeDtypeStruct((nb, s, wd), F32))
    else:
        in_specs += [pl.BlockSpec((tm, LANES), lambda b, i: (i, 0))] * 4
        args += list(rope)
    return pl.pallas_call(
        functools.partial(_proj_kernel, prompt=prompt),
        grid=(nb, s // tm),
        in_specs=in_specs,
        out_specs=out_specs,
        out_shape=out_shape,
        compiler_params=_cparams(("parallel", "parallel"), 60),
        name="attn_proj_prompt" if prompt else "attn_proj_latent",
    )(*args)


def _ctx_kv_kernel(ckv_ref, kr_ref, wkv_ref, wkvt_ref, kb_o, vbt_o):
    ckvb = ckv_ref[...].astype(BF16)
    kn_all = jnp.dot(ckvb, wkv_ref[...], preferred_element_type=F32)
    krb = kr_ref[...].astype(BF16)
    zero = jnp.zeros((krb.shape[0], B_QK_PAD - B_NOPE - B_ROPE), BF16)
    for h in range(B_HEADS):
        kb_o[h, :, :B_NOPE] = kn_all[:, h * B_NOPE:(h + 1) * B_NOPE].astype(BF16)
        kb_o[h, :, B_NOPE:B_NOPE + B_ROPE] = krb
        kb_o[h, :, B_NOPE + B_ROPE:] = zero
    vbt_o[...] = lax.dot_general(wkvt_ref[...], ckvb, _NT_DIMS, preferred_element_type=F32).astype(BF16)


def _ctx_kv_expand(cache_ckv, cache_kr, attn_layer, w):
    nb, _, p, _ = cache_ckv.shape
    return pl.pallas_call(
        _ctx_kv_kernel,
        grid=(nb,),
        in_specs=[
            pl.BlockSpec((None, None, p, KV_RANK), lambda b: (b, attn_layer, 0, 0)),
            pl.BlockSpec((None, None, p, B_ROPE), lambda b: (b, attn_layer, 0, 0)),
            pl.BlockSpec((None, KV_RANK, B_HEADS * B_NOPE), lambda b: (attn_layer, 0, 0)),
            pl.BlockSpec((None, B_HEADS * B_V, KV_RANK), lambda b: (attn_layer, 0, 0)),
        ],
        out_specs=[
            pl.BlockSpec((None, B_HEADS, p, B_QK_PAD), lambda b: (b, 0, 0, 0)),
            pl.BlockSpec((None, B_HEADS * B_V, p), lambda b: (b, 0, 0)),
        ],
        out_shape=[
            jax.ShapeDtypeStruct((nb, B_HEADS, p, B_QK_PAD), BF16),
            jax.ShapeDtypeStruct((nb, B_HEADS * B_V, p), BF16),
        ],
        compiler_params=_cparams(("parallel",), 32),
        name="ctx_kv_expand",
    )(cache_ckv, cache_kr, w.kv_k, w.kv_vt)


KV_LANES = "lanes"
KV_HEADS = "heads"
KV_CACHE = "cache"


def _attn_kernel(*refs, seg_len, seg_layout, n_cast, unroll, hb, group, dq, dv, tq, rolled):
    nseg = len(seg_len)
    q_ref = refs[0]
    kv_refs = refs[1:1 + 2 * nseg]
    cast_in = refs[1 + 2 * nseg:1 + 2 * nseg + n_cast]
    o_ref = refs[1 + 2 * nseg + n_cast]
    cast_out = refs[2 + 2 * nseg + n_cast:2 + 2 * nseg + 2 * n_cast]
    s_ref, m_ref = refs[2 + 2 * nseg + 2 * n_cast:]
    for src, dst in zip(cast_in, cast_out):
        dst[...] = src[...].astype(BF16)
    n_sub = q_ref.shape[1] // tq
    n_items = hb * n_sub
    seg_off = [sum(seg_len[:i]) for i in range(nseg)]
    dynamic_kv = all(lay == KV_HEADS for lay in seg_layout)

    def split(i):
        if isinstance(i, int):
            return i // n_sub, (i % n_sub) * tq
        shift = n_sub.bit_length() - 1
        return lax.shift_right_logical(i, shift), pl.multiple_of((i & (n_sub - 1)) * tq, tq)

    def kv_head(h):
        if isinstance(h, int):
            return h // group
        return lax.shift_right_logical(h, group.bit_length() - 1) if dynamic_kv else 0

    def keys(s, kvh):
        ref = kv_refs[2 * s]
        k = ref[kvh] if seg_layout[s] == KV_HEADS else ref[:, kvh * dq:(kvh + 1) * dq]
        return k.astype(BF16)

    def values_t(s, kvh):
        ref = kv_refs[2 * s + 1]
        if seg_layout[s] == KV_HEADS:
            vt = ref[kvh]
        elif seg_layout[s] == KV_LANES:
            vt = ref[kvh * dv:(kvh + 1) * dv, :]
        else:
            vt = ref[:, kvh * dv:(kvh + 1) * dv].T
        return vt.astype(BF16)

    def score(i, slot):
        h, r = split(i)
        kvh = kv_head(h)
        q = q_ref[h, pl.ds(r, tq), :]
        m = None
        for s in range(nseg):
            st = lax.dot_general(keys(s, kvh), q, _NT_DIMS, preferred_element_type=F32)
            s_ref[slot, seg_off[s]:seg_off[s] + seg_len[s], :] = st
            ms = jnp.max(st, axis=0, keepdims=True)
            m = ms if m is None else jnp.maximum(m, ms)
        m_ref[slot] = m

    def finish(i, slot):
        h, r = split(i)
        kvh = kv_head(h)
        m = m_ref[slot]
        denom = None
        acc = None
        for s in range(nseg):
            p = jnp.exp2(s_ref[slot, seg_off[s]:seg_off[s] + seg_len[s], :] - m)
            ls = jnp.sum(p, axis=0, keepdims=True)
            os_ = jnp.dot(values_t(s, kvh), p.astype(BF16), preferred_element_type=F32)
            denom = ls if denom is None else denom + ls
            acc = os_ if acc is None else acc + os_
        o_ref[h, pl.ds(r, tq), :] = (acc / denom).T.astype(o_ref.dtype)

    if rolled:
        assert dynamic_kv or hb == group
        assert unroll % 2 == 0 and n_items % unroll == 0 and n_items >= 2 * unroll
        assert n_sub & (n_sub - 1) == 0 and group & (group - 1) == 0
        score(0, 0)

        def body(k, carry):
            for u in range(unroll):
                score(unroll * k + u + 1, (u + 1) % 2)
                finish(unroll * k + u, u % 2)
            return carry

        lax.fori_loop(0, n_items // unroll - 1, body, 0)
        for u in range(unroll - 1):
            score(n_items - unroll + u + 1, (u + 1) % 2)
            finish(n_items - unroll + u, u % 2)
        finish(n_items - 1, (unroll - 1) % 2)
    else:
        for i in range(n_items):
            score(i, i)
        for i in range(n_items):
            finish(i, i)


def _attention(q, segs, *, seq_len, hb, group, dq, tqb, tq, rolled, unroll=2, seg_layer=None, cast=()):
    nbq, n_heads, s_tot, _ = q.shape
    dv = HEAD_DIM
    per = s_tot // seq_len
    n_qb = seq_len // tqb
    nkv = hb // group
    in_specs = [pl.BlockSpec((None, hb, tqb, dq), lambda g, h, t: (g // per, h, (g % per) * n_qb + t, 0))]
    args = [q]
    seg_lens = []
    for k, v, layout in segs:
        if layout == KV_CACHE:
            sk = k.shape[2]
            in_specs.append(pl.BlockSpec((None, None, sk, nkv * dq), lambda g, h, t: (g, seg_layer, 0, h)))
            in_specs.append(pl.BlockSpec((None, None, sk, nkv * dv), lambda g, h, t: (g, seg_layer, 0, h)))
        elif layout == KV_LANES:
            sk = k.shape[1] // per
            in_specs.append(pl.BlockSpec((None, sk, nkv * dq), lambda g, h, t: (g // per, g % per, h)))
            in_specs.append(pl.BlockSpec((None, nkv * dv, sk), lambda g, h, t: (g // per, h, g % per)))
        else:
            sk = k.shape[2] // per
            in_specs.append(pl.BlockSpec((None, nkv, sk, dq), lambda g, h, t: (g // per, h, g % per, 0)))
            in_specs.append(pl.BlockSpec((None, nkv, dv, sk), lambda g, h, t: (g // per, h, 0, g % per)))
        seg_lens.append(sk)
        args += [k, v]
    grid = (nbq * per, n_heads // hb, n_qb)
    out_specs = [pl.BlockSpec((None, hb, tqb, dv), lambda g, h, t: (g // per, h, (g % per) * n_qb + t, 0))]
    out_shape = [jax.ShapeDtypeStruct((nbq, n_heads, s_tot, dv), BF16)]
    n_steps = grid[0] * grid[1] * grid[2]
    n_slots = 2 if rolled else hb * (tqb // tq)
    for a, first_row, n_rows in cast:
        slab = n_rows // n_steps
        assert n_rows % n_steps == 0 and first_row % slab == 0
        in_specs.append(pl.BlockSpec(
            (slab, a.shape[1]),
            lambda g, h, t, b0=first_row // slab: (b0 + (g * grid[1] + h) * grid[2] + t, 0)))
        out_specs.append(pl.BlockSpec((slab, a.shape[1]), lambda g, h, t: ((g * grid[1] + h) * grid[2] + t, 0)))
        out_shape.append(jax.ShapeDtypeStruct((n_rows, a.shape[1]), BF16))
        args.append(a)
    outs = pl.pallas_call(
        functools.partial(_attn_kernel, seg_len=tuple(seg_lens), seg_layout=tuple(s[2] for s in segs),
                          n_cast=len(cast), unroll=unroll, hb=hb, group=group, dq=dq, dv=dv, tq=tq,
                          rolled=rolled),
        grid=grid,
        in_specs=in_specs,
        out_specs=out_specs,
        out_shape=out_shape,
        scratch_shapes=[pltpu.VMEM((n_slots, sum(seg_lens), tq), F32), pltpu.VMEM((n_slots, 1, tq), F32)],
        compiler_params=_cparams(("parallel", "parallel", "arbitrary"), 56),
        name="attention",
    )(*args)
    return (outs[0], tuple(outs[1:])) if cast else outs[0]


def _close_mixer(x, y, mod_ref, gf_ref, o_ref, xb_o, rows=slice(None)):
    x_new = x + mod_ref[2:3, :] * y
    o_ref[rows, :] = x_new
    xb_o[rows, :] = _modulate(x_new, gf_ref[...], mod_ref[3:4, :], mod_ref[4:5, :]).astype(BF16)


def _attn_out_kernel(x_ref, mod_ref, gf_ref, oa_ref, ob_ref, w_ref, o_ref, xb_o):
    half = x_ref.shape[0] // 2
    for r in range(2):
        rows = slice(r * half, (r + 1) * half)
        cat = jnp.concatenate([oa_ref[h, rows, :] for h in range(A_HEADS)]
                              + [ob_ref[h, rows, :] for h in range(B_HEADS)], axis=-1)
        y = jnp.dot(cat, w_ref[...], preferred_element_type=F32)
        _close_mixer(x_ref[rows, :], y, mod_ref, gf_ref, o_ref, xb_o, rows)


def _attn_out(x, mods, row0, layer, attn_layer, g_ffn, oa, ob, w_out):
    nb, s, d = x.shape
    tm = TM_ATTN_OUT
    tile = pl.BlockSpec((None, tm, d), lambda b, i: (b, i, 0))
    return pl.pallas_call(
        _attn_out_kernel,
        grid=(nb, s // tm),
        in_specs=[
            tile,
            _mod_spec(layer, row0),
            _layer_row(d, layer),
            pl.BlockSpec((None, A_HEADS, tm, HEAD_DIM), lambda b, i: (b, 0, i, 0)),
            pl.BlockSpec((None, B_HEADS, tm, B_V), lambda b, i: (b, 0, i, 0)),
            _layer_resident((A_Q_W + B_HEADS * B_V, d), attn_layer),
        ],
        out_specs=[tile, tile],
        out_shape=[jax.ShapeDtypeStruct((nb, s, d), F32), jax.ShapeDtypeStruct((nb, s, d), BF16)],
        compiler_params=_cparams(("parallel", "parallel"), 44),
        name="attn_out",
    )(x, mods, g_ffn, oa, ob, w_out)


def _gelu_tanh(x):
    return 0.5 * x * (1.0 + jnp.tanh(0.7978845608028654 * (x + 0.044715 * (x * x * x))))


def _cmlp_kernel(x_ref, mod_ref, g_ref, gf_ref, win_ref, vn_ref, ws_ref, bias_ref, wout_ref, o_ref, xb_o, t_ref):
    half = x_ref.shape[0] // 2
    for r in range(2):
        x = x_ref[r * half:(r + 1) * half, :]
        xb = _modulate(x, g_ref[...], mod_ref[0:1, :], mod_ref[1:2, :]).astype(BF16)
        v = _gelu_tanh(jnp.dot(xb, win_ref[:, D_MODEL:], preferred_element_type=F32))
        vb = (_rms(v) * vn_ref[...]).astype(BF16)
        u = _gelu_tanh(jnp.dot(xb, win_ref[:, :D_MODEL], preferred_element_type=F32))
        for n in range(half // CHUNK):
            rows = slice(n * CHUNK, (n + 1) * CHUNK)
            trows = slice(r * half + n * CHUNK, r * half + (n + 1) * CHUNK)
            for g in range(C_GROUPS):
                cols = slice(g * C_GROUP_W, (g + 1) * C_GROUP_W)
                sv = jnp.dot(ws_ref[g], vb[rows, cols], preferred_element_type=F32) + bias_ref[:, cols]
                t_ref[trows, cols] = (u[rows, cols] * sv).astype(BF16)
        hrows = slice(r * half, (r + 1) * half)
        y = jnp.dot(t_ref[hrows, :], wout_ref[...], preferred_element_type=F32)
        _close_mixer(x, y, mod_ref, gf_ref, o_ref, xb_o, hrows)


def _chunk_mlp(x, mods, row0, layer, c_layer, g, g_ffn, w_in, v_norm, w_s, bias, w_out):
    nb, s, d = x.shape
    tm = TM_CMLP
    tile = pl.BlockSpec((None, tm, d), lambda b, i: (b, i, 0))
    return pl.pallas_call(
        _cmlp_kernel,
        grid=(nb, s // tm),
        in_specs=[
            tile,
            _mod_spec(layer, row0),
            _layer_row(d, layer),
            _layer_row(d, layer),
            _layer_resident((d, 2 * d), c_layer),
            _layer_row(d, c_layer),
            _layer_resident((C_GROUPS, CHUNK, CHUNK), c_layer),
            _layer_resident((CHUNK, d), c_layer),
            _layer_resident((d, d), c_layer),
        ],
        out_specs=[tile, tile],
        out_shape=[jax.ShapeDtypeStruct((nb, s, d), F32), jax.ShapeDtypeStruct((nb, s, d), BF16)],
        scratch_shapes=[pltpu.VMEM((tm, d), BF16)],
        compiler_params=_cparams(("parallel", "parallel"), 60),
        name="chunk_mlp",
    )(x, mods, g, g_ffn, w_in, v_norm, w_s, bias, w_out)


def _rope_tables(n_tokens):
    t = np.arange(n_tokens)
    row = (t // GRID_W).astype(np.float32)
    col = (t % GRID_W).astype(np.float32)

    def axis_tables(pos, half):
        inv = (1.0 / (ROPE_BASE ** (np.arange(0, half, 2, dtype=np.float32) / half))).astype(np.float32)
        ang = (pos[:, None] * inv).astype(np.float32).astype(np.float64)
        c, s = np.cos(ang), np.sin(ang)
        return np.concatenate([c, c], axis=1), np.concatenate([-s, s], axis=1)

    def tables(dim):
        cr, sr = axis_tables(row, dim // 2)
        cc, sc = axis_tables(col, dim // 2)
        return np.concatenate([cr, cc], axis=1), np.concatenate([sr, sc], axis=1)

    c128, s128 = tables(HEAD_DIM)
    c64, s64 = tables(B_ROPE)
    pad = np.zeros((n_tokens, LANES - B_ROPE))
    c64 = np.concatenate([c64, pad], axis=1)
    s64 = np.concatenate([s64, pad], axis=1)
    return tuple(jnp.asarray(a, F32) for a in (c128, s128, c64, s64))


def kernel(x_prompt, x_sample, c, cache_gqa_k, cache_gqa_v, cache_mla_ckv, cache_mla_krope, c_ctx, ada_w, ada_b, norm_mix, norm_ffn, ffn_gate, ffn_up, ffn_down, attn_w_in, attn_q_norm, attn_k_norm, attn_kv_norm, attn_w_kv_up, attn_w_out, cmlp_w_in, cmlp_v_norm, cmlp_w_s, cmlp_b_s, cmlp_w_out, final_norm):
    n_p, s_p, d = x_prompt.shape
    n_s, s_s, _ = x_sample.shape
    past = cache_gqa_k.shape[2]

    cond = jnp.concatenate([c_ctx[None, :], c, jnp.zeros((MOD_ROWS - 1 - n_s, d), F32)], axis=0)
    mods = _ada_mod(cond, ada_w, ada_b)

    def rows(a):
        return a.reshape(a.shape[0], 1, a.shape[1])

    def heads_t(vt):
        return vt.reshape(vt.shape[0], B_HEADS, B_V, vt.shape[2])

    aw = _layout_attn_weights(attn_w_in, attn_w_kv_up)
    g_mix, g_ffn = rows(norm_mix), rows(norm_ffn)
    qn, kn, kvn = rows(attn_q_norm), rows(attn_k_norm), rows(attn_kv_norm)
    cw_s = cmlp_w_s.astype(BF16)
    c_vn = rows(cmlp_v_norm)

    def flat(a):
        return a.reshape(-1, a.shape[-1])

    def ffn_cast(layer):
        return [(flat(w), layer * w.shape[1], w.shape[1]) for w in (ffn_gate, ffn_up, ffn_down)]

    ffn_w = {}
    n_al, n_cl = attn_w_out.shape[0], cmlp_w_in.shape[0]
    whole = [(flat(w), 0, w.shape[0] * w.shape[1]) for w in (attn_w_out, cmlp_w_in, cmlp_w_out)]
    c_bias = jnp.repeat(jnp.swapaxes(cmlp_b_s, 1, 2), C_GROUP_W, axis=2)
    final_g = final_norm.reshape(1, d)

    xp = x_prompt.reshape(1, n_p * s_p, d)
    xs = x_sample
    rope = _rope_tables(s_s)
    cache_k = cache_gqa_k.reshape(n_s, -1, past, A_KV_W)
    cache_v = cache_gqa_v.reshape(n_s, -1, past, A_KV_W)

    states = ([], [], [], [])
    for i in range(DEPTH):
        j = i // 2
        if i % 2 == 0:
            qa, ka, vat, qb, kb, vbt = _attn_project(xs, mods, 1, i, j, g_mix, aw, qn, kn, kvn, rope=rope)
            kb_ctx, vbt_ctx = _ctx_kv_expand(cache_mla_ckv, cache_mla_krope, j, aw)
            oa, ffn_w[i] = _attention(
                qa, [(ka, vat, KV_LANES), (cache_k, cache_v, KV_CACHE)], seq_len=s_s, hb=A_GROUP, group=A_GROUP,
                dq=HEAD_DIM, tqb=2048, tq=TQ_ATTN, rolled=True, unroll=UNROLL_GQA, seg_layer=j, cast=ffn_cast(i))
            ob, cast_b = _attention(
                qb, [(kb, heads_t(vbt), KV_HEADS), (kb_ctx, heads_t(vbt_ctx), KV_HEADS)], seq_len=s_s, hb=1,
                group=1, dq=B_QK_PAD, tqb=4096, tq=TQ_ATTN, rolled=True, unroll=UNROLL_MLA,
                cast=ffn_cast(i + 1) + (whole if i == 0 else []))
            ffn_w[i + 1] = cast_b[:3]
            if i == 0:
                a_out = cast_b[3].reshape(n_al, -1, d)
                cw_in = cast_b[4].reshape(n_cl, d, -1)
                cw_out = cast_b[5].reshape(n_cl, -1, d)
            xs, xsb = _attn_out(xs, mods, 1, i, j, g_ffn, oa, ob, a_out)

            qa, ka, vat, qb, kb, vbt, st_k, st_v, st_c, st_r = _attn_project(
                xp, mods, 0, i, j, g_mix, aw, qn, kn, kvn)
            for lst, st in zip(states, (st_k, st_v, st_c, st_r)):
                lst.append(st)
            oa = _attention(qa, [(ka, vat, KV_LANES)], seq_len=s_p, hb=A_HEADS, group=A_GROUP, dq=HEAD_DIM,
                            tqb=s_p, tq=s_p, rolled=False)
            ob = _attention(qb, [(kb, heads_t(vbt), KV_HEADS)], seq_len=s_p, hb=B_HEADS, group=1, dq=B_QK_PAD,
                            tqb=s_p, tq=s_p, rolled=False)
            xp, xpb = _attn_out(xp, mods, 0, i, j, g_ffn, oa, ob, a_out)
        else:
            xp, xpb = _chunk_mlp(xp, mods, 0, i, j, g_mix, g_ffn, cw_in, c_vn, cw_s, c_bias, cw_out)
            xs, xsb = _chunk_mlp(xs, mods, 1, i, j, g_mix, g_ffn, cw_in, c_vn, cw_s, c_bias, cw_out)

        fin = final_g if i == DEPTH - 1 else None
        xp = _ffn(xp, xpb, mods, 0, i, *ffn_w[i], fin)
        xs = _ffn(xs, xsb, mods, 1, i, *ffn_w[i], fin)

    y_prompt = xp.reshape(n_p, s_p, d)
    st_k, st_v, st_c, st_r = (jnp.stack([a.reshape(n_p, s_p, a.shape[-1]) for a in lst], axis=1)
                              for lst in states)
    return (y_prompt, xs,
            st_k.reshape(n_p, -1, s_p, A_KV_HEADS, HEAD_DIM),
            st_v.reshape(n_p, -1, s_p, A_KV_HEADS, HEAD_DIM),
            st_c, st_r)
```

```python
import functools
from typing import NamedTuple

import numpy as np
import jax
import jax.numpy as jnp
from jax import lax
from jax.experimental import pallas as pl
from jax.experimental.pallas import tpu as pltpu

F32 = jnp.float32
BF16 = jnp.bfloat16

D_MODEL = 2048
DEPTH = 4
GRID_W = 64
ROPE_BASE = 10000.0
EPS = 1e-6
A_HEADS = 8
A_KV_HEADS = 2
A_GROUP = A_HEADS // A_KV_HEADS
HEAD_DIM = 128
A_SCALE = HEAD_DIM ** -0.5
B_HEADS = 8
B_NOPE = 128
B_ROPE = 64
B_V = 128
KV_RANK = 512
B_SCALE = (B_NOPE + B_ROPE) ** -0.5
B_QK_PAD = 256
A_Q_W = A_HEADS * HEAD_DIM
A_KV_W = A_KV_HEADS * HEAD_DIM
B_Q_W = B_HEADS * (B_NOPE + B_ROPE)
CHUNK = 128
C_GROUPS = 8
C_GROUP_W = D_MODEL // C_GROUPS
FFN_HIDDEN = 5632
LOG2E = 1.4426950408889634
LANES = 128

Z_QA = 0
Z_KA = Z_QA + A_Q_W
Z_VA = Z_KA + A_KV_W
Z_QB = Z_VA + A_KV_W
Z_CKV = Z_QB + B_HEADS * B_QK_PAD
Z_KR = Z_CKV + KV_RANK
Z_W = Z_KR + LANES

MOD_ROWS = 8
_NT_DIMS = (((1,), (1,)), ((), ()))

TM_FFN, TC_FFN = 1024, 512
TM_PROJ = 512
TM_ATTN_OUT = 512
TM_CMLP = 512
TN_ADA = 1024
TQ_ATTN = 256
UNROLL_GQA, UNROLL_MLA = 8, 4


def _cparams(sem, vmem_mib):
    return pltpu.CompilerParams(dimension_semantics=sem, vmem_limit_bytes=vmem_mib << 20)


def _layer_resident(shape, layer):
    n = len(shape)
    return pl.BlockSpec((None,) + tuple(shape), lambda *_: (layer,) + (0,) * n, pipeline_mode=pl.Buffered(1))


def _layer_row(width, layer):
    return pl.BlockSpec((None, 1, width), lambda *_: (layer, 0, 0))


def _mod_spec(layer, row0):
    return pl.BlockSpec((None, None, 6, D_MODEL), lambda b, *_: (layer, row0 + b, 0, 0))


def _rms(x):
    return x * lax.rsqrt(jnp.mean(x * x, axis=-1, keepdims=True) + EPS)


def _modulate(x, g, shift, scale):
    return (_rms(x) * g) * (1.0 + scale) + shift


def _swap_halves(v, n):
    lane = lax.broadcasted_iota(jnp.int32, v.shape, 1)
    fwd = pltpu.roll(v, LANES - n, 1)
    bwd = pltpu.roll(v, n, 1)
    return jnp.where((lane & (2 * n - 1)) < n, fwd, bwd)


def _ada_kernel(cond_ref, w_ref, b_ref, o_ref):
    c = cond_ref[...]
    a = (c * jax.nn.sigmoid(c)).astype(BF16)
    o_ref[...] = jnp.dot(a, w_ref[...].astype(BF16), preferred_element_type=F32) + b_ref[...]


def _ada_mod(cond, ada_w, ada_b):
    n6 = 6 * D_MODEL
    out = pl.pallas_call(
        _ada_kernel,
        grid=(DEPTH, n6 // TN_ADA),
        in_specs=[
            pl.BlockSpec((MOD_ROWS, D_MODEL), lambda l, n: (0, 0)),
            pl.BlockSpec((None, D_MODEL, TN_ADA), lambda l, n: (l, 0, n)),
            pl.BlockSpec((None, 1, TN_ADA), lambda l, n: (l, 0, n)),
        ],
        out_specs=pl.BlockSpec((None, MOD_ROWS, TN_ADA), lambda l, n: (l, 0, n)),
        out_shape=jax.ShapeDtypeStruct((DEPTH, MOD_ROWS, n6), F32),
        compiler_params=_cparams(("arbitrary", "arbitrary"), 40),
        name="ada_mod",
    )(cond, ada_w, ada_b.reshape(DEPTH, 1, n6))
    return out.reshape(DEPTH, MOD_ROWS, 6, D_MODEL)


def _ffn_kernel(*refs, n_chunks, final):
    if final:
        x_hbm, xb_ref, mod_ref, wg_ref, wu_ref, wd_ref, fn_ref, o_ref, x_buf, x_sem = refs
    else:
        x_hbm, xb_ref, mod_ref, wg_ref, wu_ref, wd_ref, o_ref, x_buf, x_sem = refs
    b, i, j = pl.program_id(0), pl.program_id(1), pl.program_id(2)
    tm = x_buf.shape[0]
    x_copy = pltpu.make_async_copy(x_hbm.at[b, pl.ds(pl.multiple_of(i * tm, tm), tm), :], x_buf, x_sem)

    @pl.when(j == 0)
    def _():
        x_copy.start()
        o_ref[...] = jnp.zeros_like(o_ref)

    xb = xb_ref[...]
    gate = jnp.dot(xb, wg_ref[...], preferred_element_type=F32)
    up = jnp.dot(xb, wu_ref[...], preferred_element_type=F32)
    h = (gate * jax.nn.sigmoid(gate) * up).astype(BF16)
    o_ref[...] += jnp.dot(h, wd_ref[...], preferred_element_type=F32)

    @pl.when(j == n_chunks - 1)
    def _():
        x_copy.wait()
        y = x_buf[...] + mod_ref[5:6, :] * o_ref[...]
        if final:
            y = _rms(y) * fn_ref[...]
        o_ref[...] = y


def _ffn(x, xb, mods, row0, layer, wg, wu, wd, final_g=None):
    nb, s, d = x.shape
    tm, tc = TM_FFN, TC_FFN
    n_chunks = FFN_HIDDEN // tc
    final = final_g is not None
    in_specs = [
        pl.BlockSpec(memory_space=pl.ANY),
        pl.BlockSpec((None, tm, d), lambda b, i, j: (b, i, 0)),
        _mod_spec(layer, row0),
        pl.BlockSpec((d, tc), lambda b, i, j: (0, j)),
        pl.BlockSpec((d, tc), lambda b, i, j: (0, j)),
        pl.BlockSpec((tc, d), lambda b, i, j: (j, 0)),
    ]
    args = [x, xb, mods, wg, wu, wd]
    if final:
        in_specs.append(pl.BlockSpec((1, d), lambda b, i, j: (0, 0)))
        args.append(final_g)
    return pl.pallas_call(
        functools.partial(_ffn_kernel, n_chunks=n_chunks, final=final),
        grid=(nb, s // tm, n_chunks),
        in_specs=in_specs,
        out_specs=pl.BlockSpec((None, tm, d), lambda b, i, j: (b, i, 0)),
        out_shape=jax.ShapeDtypeStruct((nb, s, d), F32),
        scratch_shapes=[pltpu.VMEM((tm, d), F32), pltpu.SemaphoreType.DMA(())],
        compiler_params=_cparams(("arbitrary", "arbitrary", "arbitrary"), 60),
        name="ffn_final" if final else "ffn",
    )(*args)


class _AttnWeights(NamedTuple):
    main: jax.Array
    va_t: jax.Array
    kv_k: jax.Array
    kv_vt: jax.Array


W_GROUP = 64


def _w_in_group_map():
    src = list(range(Z_QB // W_GROUP))
    per_head = (B_NOPE + B_ROPE) // W_GROUP
    for h in range(B_HEADS):
        first = (A_Q_W + 2 * A_KV_W) // W_GROUP + h * per_head
        src += list(range(first, first + per_head)) + [-1] * (B_QK_PAD // W_GROUP - per_head)
    first = (A_Q_W + 2 * A_KV_W + B_Q_W) // W_GROUP
    src += list(range(first, first + KV_RANK // W_GROUP))
    src += [first + KV_RANK // W_GROUP, -1]
    assert len(src) == Z_W // W_GROUP
    return src


W_GROUPS_PER_STEP = 6


def _w_in_layout_kernel(src, *refs):
    main_o = refs[-1]
    c = pl.program_id(1)
    parts = [jnp.where(src[W_GROUPS_PER_STEP * c + k] >= 0, refs[k][...], 0.0) for k in range(W_GROUPS_PER_STEP)]
    main_o[...] = jnp.concatenate(parts, axis=0).T.astype(BF16)


def _layout_attn_weights(w_in, w_kv_up):
    n_l, d, _ = w_in.shape
    w_t = jnp.swapaxes(w_in, 1, 2)
    src = jnp.asarray(_w_in_group_map(), jnp.int32)
    n_g = W_GROUPS_PER_STEP
    main = pl.pallas_call(
        _w_in_layout_kernel,
        grid_spec=pltpu.PrefetchScalarGridSpec(
            num_scalar_prefetch=1,
            grid=(n_l, Z_W // (n_g * W_GROUP)),
            in_specs=[pl.BlockSpec((None, W_GROUP, d),
                                   lambda l, c, s, k=k: (l, jnp.maximum(s[n_g * c + k], 0), 0))
                      for k in range(n_g)],
            out_specs=pl.BlockSpec((None, d, n_g * W_GROUP), lambda l, c, s: (l, 0, c)),
        ),
        out_shape=jax.ShapeDtypeStruct((n_l, d, Z_W), BF16),
        compiler_params=_cparams(("parallel", "parallel"), 32),
        name="w_in_layout",
    )(src, *([w_t] * n_g))
    va_t = w_t[:, Z_VA:Z_VA + A_KV_W, :].astype(BF16)
    kv4 = w_kv_up.reshape(n_l, KV_RANK, B_HEADS, B_NOPE + B_V)
    return _AttnWeights(
        main=main,
        va_t=va_t,
        kv_k=kv4[..., :B_NOPE].reshape(n_l, KV_RANK, -1).astype(BF16),
        kv_vt=jnp.swapaxes(kv4[..., B_NOPE:].reshape(n_l, KV_RANK, -1), 1, 2).astype(BF16))


def _proj_kernel(*refs, prompt):
    if prompt:
        (x_ref, mod_ref, g_ref, w_ref, wvat_ref, wkv_ref, wkvt_ref, qn_ref, kn_ref, kvn_ref,
         qa_o, ka_o, vat_o, qb_o, kb_o, vbt_o, sk_o, sv_o, sc_o, sr_o) = refs
    else:
        (x_ref, mod_ref, g_ref, w_ref, wvat_ref, wkv_ref, wkvt_ref, qn_ref, kn_ref, kvn_ref,
         c128_ref, s128_ref, c64_ref, s64_ref,
         qa_o, ka_o, vat_o, qb_o, kb_o, vbt_o) = refs

    half = x_ref.shape[0] // 2
    for r in range(2):
        rows = slice(r * half, (r + 1) * half)
        if not prompt:
            c128, s128 = c128_ref[rows, :], s128_ref[rows, :]
            c64, s64 = c64_ref[rows, :], s64_ref[rows, :]

        def rope128(v):
            return v if prompt else v * c128 + _swap_halves(v, 32) * s128

        def rope64(v):
            return v if prompt else v * c64 + _swap_halves(v, 16) * s64

        xb = _modulate(x_ref[rows, :], g_ref[...], mod_ref[0:1, :], mod_ref[1:2, :]).astype(BF16)
        za = jnp.dot(xb, w_ref[:, :Z_QB if prompt else Z_VA], preferred_element_type=F32)
        zb = jnp.dot(xb, w_ref[:, Z_QB:], preferred_element_type=F32)
        vat_o[:, rows] = lax.dot_general(wvat_ref[...], xb, _NT_DIMS, preferred_element_type=F32).astype(BF16)

        for h in range(A_HEADS):
            q = _rms(za[:, Z_QA + h * HEAD_DIM:Z_QA + (h + 1) * HEAD_DIM]) * qn_ref[...]
            qa_o[h, rows, :] = (rope128(q) * (A_SCALE * LOG2E)).astype(BF16)
        for h in range(A_KV_HEADS):
            sl = slice(h * HEAD_DIM, (h + 1) * HEAD_DIM)
            k = _rms(za[:, Z_KA + h * HEAD_DIM:Z_KA + (h + 1) * HEAD_DIM]) * kn_ref[...]
            if prompt:
                sk_o[rows, sl] = k
            ka_o[rows, sl] = rope128(k).astype(BF16)
        if prompt:
            sv_o[rows, :] = za[:, Z_VA:Z_VA + A_KV_W]
        for h in range(B_HEADS):
            base = h * B_QK_PAD
            qb_o[h, rows, :B_NOPE] = (zb[:, base:base + B_NOPE] * (B_SCALE * LOG2E)).astype(BF16)
            qb_o[h, rows, B_NOPE:] = (
                rope64(zb[:, base + B_NOPE:base + B_QK_PAD]) * (B_SCALE * LOG2E)).astype(BF16)

        ckv = _rms(zb[:, Z_CKV - Z_QB:Z_CKV - Z_QB + KV_RANK]) * kvn_ref[...]
        kr = zb[:, Z_KR - Z_QB:Z_KR - Z_QB + LANES]
        if prompt:
            sc_o[rows, :] = ckv
            sr_o[rows, :] = kr[:, :B_ROPE]
        krb = rope64(kr).astype(BF16)
        ckvb = ckv.astype(BF16)
        kn_all = jnp.dot(ckvb, wkv_ref[...], preferred_element_type=F32)
        for h in range(B_HEADS):
            kb_o[h, rows, :B_NOPE] = kn_all[:, h * B_NOPE:(h + 1) * B_NOPE].astype(BF16)
            kb_o[h, rows, B_NOPE:] = krb
        vbt_o[:, rows] = lax.dot_general(wkvt_ref[...], ckvb, _NT_DIMS, preferred_element_type=F32).astype(BF16)


def _attn_project(x, mods, row0, layer, attn_layer, g, w, qn, kn, kvn, rope=None):
    nb, s, d = x.shape
    tm = TM_PROJ
    prompt = rope is None

    def row(wd):
        return pl.BlockSpec((None, tm, wd), lambda b, i: (b, i, 0))

    def col(wd):
        return pl.BlockSpec((None, wd, tm), lambda b, i: (b, 0, i))

    def heads(n, wd):
        return pl.BlockSpec((None, n, tm, wd), lambda b, i: (b, 0, i, 0))

    in_specs = [
        row(d),
        _mod_spec(layer, row0),
        _layer_row(d, layer),
        _layer_resident((d, Z_W), attn_layer),
        _layer_resident((A_KV_W, d), attn_layer),
        _layer_resident((KV_RANK, B_HEADS * B_NOPE), attn_layer),
        _layer_resident((B_HEADS * B_V, KV_RANK), attn_layer),
        _layer_row(HEAD_DIM, attn_layer),
        _layer_row(HEAD_DIM, attn_layer),
        _layer_row(KV_RANK, attn_layer),
    ]
    args = [x, mods, g, w.main, w.va_t, w.kv_k, w.kv_vt, qn, kn, kvn]
    out_specs = [heads(A_HEADS, HEAD_DIM), row(A_KV_W), col(A_KV_W), heads(B_HEADS, B_QK_PAD),
                 heads(B_HEADS, B_QK_PAD), col(B_HEADS * B_V)]
    out_shape = [jax.ShapeDtypeStruct(shp, BF16) for shp in (
        (nb, A_HEADS, s, HEAD_DIM), (nb, s, A_KV_W), (nb, A_KV_W, s), (nb, B_HEADS, s, B_QK_PAD),
        (nb, B_HEADS, s, B_QK_PAD), (nb, B_HEADS * B_V, s))]
    if prompt:
        for wd in (A_KV_W, A_KV_W, KV_RANK, B_ROPE):
            out_specs.append(row(wd))
            out_shape.append(jax.ShapeDtypeStruct((nb, s, wd), F32))
    else:
        in_specs += [pl.BlockSpec((tm, LANES), lambda b, i: (i, 0))] * 4
        args += list(rope)
    return pl.pallas_call(
        functools.partial(_proj_kernel, prompt=prompt),
        grid=(nb, s // tm),
        in_specs=in_specs,
        out_specs=out_specs,
        out_shape=out_shape,
        compiler_params=_cparams(("parallel", "parallel"), 60),
        name="attn_proj_prompt" if prompt else "attn_proj_latent",
    )(*args)


def _ctx_kv_kernel(ckv_ref, kr_ref, wkv_ref, wkvt_ref, kb_o, vbt_o):
    ckvb = ckv_ref[...].astype(BF16)
    kn_all = jnp.dot(ckvb, wkv_ref[...], preferred_element_type=F32)
    krb = kr_ref[...].astype(BF16)
    zero = jnp.zeros((krb.shape[0], B_QK_PAD - B_NOPE - B_ROPE), BF16)
    for h in range(B_HEADS):
        kb_o[h, :, :B_NOPE] = kn_all[:, h * B_NOPE:(h + 1) * B_NOPE].astype(BF16)
        kb_o[h, :, B_NOPE:B_NOPE + B_ROPE] = krb
        kb_o[h, :, B_NOPE + B_ROPE:] = zero
    vbt_o[...] = lax.dot_general(wkvt_ref[...], ckvb, _NT_DIMS, preferred_element_type=F32).astype(BF16)


def _ctx_kv_expand(cache_ckv, cache_kr, attn_layer, w):
    nb, _, p, _ = cache_ckv.shape
    return pl.pallas_call(
        _ctx_kv_kernel,
        grid=(nb,),
        in_specs=[
            pl.BlockSpec((None, None, p, KV_RANK), lambda b: (b, attn_layer, 0, 0)),
            pl.BlockSpec((None, None, p, B_ROPE), lambda b: (b, attn_layer, 0, 0)),
            pl.BlockSpec((None, KV_RANK, B_HEADS * B_NOPE), lambda b: (attn_layer, 0, 0)),
            pl.BlockSpec((None, B_HEADS * B_V, KV_RANK), lambda b: (attn_layer, 0, 0)),
        ],
        out_specs=[
            pl.BlockSpec((None, B_HEADS, p, B_QK_PAD), lambda b: (b, 0, 0, 0)),
            pl.BlockSpec((None, B_HEADS * B_V, p), lambda b: (b, 0, 0)),
        ],
        out_shape=[
            jax.ShapeDtypeStruct((nb, B_HEADS, p, B_QK_PAD), BF16),
            jax.ShapeDtypeStruct((nb, B_HEADS * B_V, p), BF16),
        ],
        compiler_params=_cparams(("parallel",), 32),
        name="ctx_kv_expand",
    )(cache_ckv, cache_kr, w.kv_k, w.kv_vt)


KV_LANES = "lanes"
KV_HEADS = "heads"
KV_CACHE = "cache"


def _attn_kernel(*refs, seg_len, seg_layout, n_cast, unroll, hb, group, dq, dv, tq, rolled):
    nseg = len(seg_len)
    q_ref = refs[0]
    kv_refs = refs[1:1 + 2 * nseg]
    cast_in = refs[1 + 2 * nseg:1 + 2 * nseg + n_cast]
    o_ref = refs[1 + 2 * nseg + n_cast]
    cast_out = refs[2 + 2 * nseg + n_cast:2 + 2 * nseg + 2 * n_cast]
    s_ref, m_ref = refs[2 + 2 * nseg + 2 * n_cast:]
    for src, dst in zip(cast_in, cast_out):
        dst[...] = src[...].astype(BF16)
    n_sub = q_ref.shape[1] // tq
    n_items = hb * n_sub
    seg_off = [sum(seg_len[:i]) for i in range(nseg)]
    dynamic_kv = all(lay == KV_HEADS for lay in seg_layout)

    def split(i):
        if isinstance(i, int):
            return i // n_sub, (i % n_sub) * tq
        shift = n_sub.bit_length() - 1
        return lax.shift_right_logical(i, shift), pl.multiple_of((i & (n_sub - 1)) * tq, tq)

    def kv_head(h):
        if isinstance(h, int):
            return h // group
        return lax.shift_right_logical(h, group.bit_length() - 1) if dynamic_kv else 0

    def keys(s, kvh):
        ref = kv_refs[2 * s]
        k = ref[kvh] if seg_layout[s] == KV_HEADS else ref[:, kvh * dq:(kvh + 1) * dq]
        return k.astype(BF16)

    def values_t(s, kvh):
        ref = kv_refs[2 * s + 1]
        if seg_layout[s] == KV_HEADS:
            vt = ref[kvh]
        elif seg_layout[s] == KV_LANES:
            vt = ref[kvh * dv:(kvh + 1) * dv, :]
        else:
            vt = ref[:, kvh * dv:(kvh + 1) * dv].T
        return vt.astype(BF16)

    def score(i, slot):
        h, r = split(i)
        kvh = kv_head(h)
        q = q_ref[h, pl.ds(r, tq), :]
        m = None
        for s in range(nseg):
            st = lax.dot_general(keys(s, kvh), q, _NT_DIMS, preferred_element_type=F32)
            s_ref[slot, seg_off[s]:seg_off[s] + seg_len[s], :] = st
            ms = jnp.max(st, axis=0, keepdims=True)
            m = ms if m is None else jnp.maximum(m, ms)
        m_ref[slot] = m

    def finish(i, slot):
        h, r = split(i)
        kvh = kv_head(h)
        m = m_ref[slot]
        denom = None
        acc = None
        for s in range(nseg):
            p = jnp.exp2(s_ref[slot, seg_off[s]:seg_off[s] + seg_len[s], :] - m)
            ls = jnp.sum(p, axis=0, keepdims=True)
            os_ = jnp.dot(values_t(s, kvh), p.astype(BF16), preferred_element_type=F32)
            denom = ls if denom is None else denom + ls
            acc = os_ if acc is None else acc + os_
        o_ref[h, pl.ds(r, tq), :] = (acc / denom).T.astype(o_ref.dtype)

    if rolled:
        assert dynamic_kv or hb == group
        assert unroll % 2 == 0 and n_items % unroll == 0 and n_items >= 2 * unroll
        assert n_sub & (n_sub - 1) == 0 and group & (group - 1) == 0
        score(0, 0)

        def body(k, carry):
            for u in range(unroll):
                score(unroll * k + u + 1, (u + 1) % 2)
                finish(unroll * k + u, u % 2)
            return carry

        lax.fori_loop(0, n_items // unroll - 1, body, 0)
        for u in range(unroll - 1):
            score(n_items - unroll + u + 1, (u + 1) % 2)
            finish(n_items - unroll + u, u % 2)
        finish(n_items - 1, (unroll - 1) % 2)
    else:
        for i in range(n_items):
            score(i, i)
        for i in range(n_items):
            finish(i, i)


def _attention(q, segs, *, seq_len, hb, group, dq, tqb, tq, rolled, unroll=2, seg_layer=None, cast=()):
    nbq, n_heads, s_tot, _ = q.shape
    dv = HEAD_DIM
    per = s_tot // seq_len
    n_qb = seq_len // tqb
    nkv = hb // group
    in_specs = [pl.BlockSpec((None, hb, tqb, dq), lambda g, h, t: (g // per, h, (g % per) * n_qb + t, 0))]
    args = [q]
    seg_lens = []
    for k, v, layout in segs:
        if layout == KV_CACHE:
            sk = k.shape[2]
            in_specs.append(pl.BlockSpec((None, None, sk, nkv * dq), lambda g, h, t: (g, seg_layer, 0, h)))
            in_specs.append(pl.BlockSpec((None, None, sk, nkv * dv), lambda g, h, t: (g, seg_layer, 0, h)))
        elif layout == KV_LANES:
            sk = k.shape[1] // per
            in_specs.append(pl.BlockSpec((None, sk, nkv * dq), lambda g, h, t: (g // per, g % per, h)))
            in_specs.append(pl.BlockSpec((None, nkv * dv, sk), lambda g, h, t: (g // per, h, g % per)))
        else:
            sk = k.shape[2] // per
            in_specs.append(pl.BlockSpec((None, nkv, sk, dq), lambda g, h, t: (g // per, h, g % per, 0)))
            in_specs.append(pl.BlockSpec((None, nkv, dv, sk), lambda g, h, t: (g // per, h, 0, g % per)))
        seg_lens.append(sk)
        args += [k, v]
    grid = (nbq * per, n_heads // hb, n_qb)
    out_specs = [pl.BlockSpec((None, hb, tqb, dv), lambda g, h, t: (g // per, h, (g % per) * n_qb + t, 0))]
    out_shape = [jax.ShapeDtypeStruct((nbq, n_heads, s_tot, dv), BF16)]
    n_steps = grid[0] * grid[1] * grid[2]
    n_slots = 2 if rolled else hb * (tqb // tq)
    for a, first_row, n_rows in cast:
        slab = n_rows // n_steps
        assert n_rows % n_steps == 0 and first_row % slab == 0
        in_specs.append(pl.BlockSpec(
            (slab, a.shape[1]),
            lambda g, h, t, b0=first_row // slab: (b0 + (g * grid[1] + h) * grid[2] + t, 0)))
        out_specs.append(pl.BlockSpec((slab, a.shape[1]), lambda g, h, t: ((g * grid[1] + h) * grid[2] + t, 0)))
        out_shape.append(jax.ShapeDtypeStruct((n_rows, a.shape[1]), BF16))
        args.append(a)
    outs = pl.pallas_call(
        functools.partial(_attn_kernel, seg_len=tuple(seg_lens), seg_layout=tuple(s[2] for s in segs),
                          n_cast=len(cast), unroll=unroll, hb=hb, group=group, dq=dq, dv=dv, tq=tq,
                          rolled=rolled),
        grid=grid,
        in_specs=in_specs,
        out_specs=out_specs,
        out_shape=out_shape,
        scratch_shapes=[pltpu.VMEM((n_slots, sum(seg_lens), tq), F32), pltpu.VMEM((n_slots, 1, tq), F32)],
        compiler_params=_cparams(("parallel", "parallel", "arbitrary"), 56),
        name="attention",
    )(*args)
    return (outs[0], tuple(outs[1:])) if cast else outs[0]


def _close_mixer(x, y, mod_ref, gf_ref, o_ref, xb_o, rows=slice(None)):
    x_new = x + mod_ref[2:3, :] * y
    o_ref[rows, :] = x_new
    xb_o[rows, :] = _modulate(x_new, gf_ref[...], mod_ref[3:4, :], mod_ref[4:5, :]).astype(BF16)


def _attn_out_kernel(x_ref, mod_ref, gf_ref, oa_ref, ob_ref, w_ref, o_ref, xb_o):
    half = x_ref.shape[0] // 2
    for r in range(2):
        rows = slice(r * half, (r + 1) * half)
        cat = jnp.concatenate([oa_ref[h, rows, :] for h in range(A_HEADS)]
                              + [ob_ref[h, rows, :] for h in range(B_HEADS)], axis=-1)
        y = jnp.dot(cat, w_ref[...], preferred_element_type=F32)
        _close_mixer(x_ref[rows, :], y, mod_ref, gf_ref, o_ref, xb_o, rows)


def _attn_out(x, mods, row0, layer, attn_layer, g_ffn, oa, ob, w_out):
    nb, s, d = x.shape
    tm = TM_ATTN_OUT
    tile = pl.BlockSpec((None, tm, d), lambda b, i: (b, i, 0))
    return pl.pallas_call(
        _attn_out_kernel,
        grid=(nb, s // tm),
        in_specs=[
            tile,
            _mod_spec(layer, row0),
            _layer_row(d, layer),
            pl.BlockSpec((None, A_HEADS, tm, HEAD_DIM), lambda b, i: (b, 0, i, 0)),
            pl.BlockSpec((None, B_HEADS, tm, B_V), lambda b, i: (b, 0, i, 0)),
            _layer_resident((A_Q_W + B_HEADS * B_V, d), attn_layer),
        ],
        out_specs=[tile, tile],
        out_shape=[jax.ShapeDtypeStruct((nb, s, d), F32), jax.ShapeDtypeStruct((nb, s, d), BF16)],
        compiler_params=_cparams(("parallel", "parallel"), 44),
        name="attn_out",
    )(x, mods, g_ffn, oa, ob, w_out)


def _gelu_tanh(x):
    return 0.5 * x * (1.0 + jnp.tanh(0.7978845608028654 * (x + 0.044715 * (x * x * x))))


def _cmlp_kernel(x_ref, mod_ref, g_ref, gf_ref, win_ref, vn_ref, ws_ref, bias_ref, wout_ref, o_ref, xb_o, t_ref):
    half = x_ref.shape[0] // 2
    for r in range(2):
        x = x_ref[r * half:(r + 1) * half, :]
        xb = _modulate(x, g_ref[...], mod_ref[0:1, :], mod_ref[1:2, :]).astype(BF16)
        v = _gelu_tanh(jnp.dot(xb, win_ref[:, D_MODEL:], preferred_element_type=F32))
        vb = (_rms(v) * vn_ref[...]).astype(BF16)
        u = _gelu_tanh(jnp.dot(xb, win_ref[:, :D_MODEL], preferred_element_type=F32))
        for n in range(half // CHUNK):
            rows = slice(n * CHUNK, (n + 1) * CHUNK)
            trows = slice(r * half + n * CHUNK, r * half + (n + 1) * CHUNK)
            for g in range(C_GROUPS):
                cols = slice(g * C_GROUP_W, (g + 1) * C_GROUP_W)
                sv = jnp.dot(ws_ref[g], vb[rows, cols], preferred_element_type=F32) + bias_ref[:, cols]
                t_ref[trows, cols] = (u[rows, cols] * sv).astype(BF16)
        hrows = slice(r * half, (r + 1) * half)
        y = jnp.dot(t_ref[hrows, :], wout_ref[...], preferred_element_type=F32)
        _close_mixer(x, y, mod_ref, gf_ref, o_ref, xb_o, hrows)


def _chunk_mlp(x, mods, row0, layer, c_layer, g, g_ffn, w_in, v_norm, w_s, bias, w_out):
    nb, s, d = x.shape
    tm = TM_CMLP
    tile = pl.BlockSpec((None, tm, d), lambda b, i: (b, i, 0))
    return pl.pallas_call(
        _cmlp_kernel,
        grid=(nb, s // tm),
        in_specs=[
            tile,
            _mod_spec(layer, row0),
            _layer_row(d, layer),
            _layer_row(d, layer),
            _layer_resident((d, 2 * d), c_layer),
            _layer_row(d, c_layer),
            _layer_resident((C_GROUPS, CHUNK, CHUNK), c_layer),
            _layer_resident((CHUNK, d), c_layer),
            _layer_resident((d, d), c_layer),
        ],
        out_specs=[tile, tile],
        out_shape=[jax.ShapeDtypeStruct((nb, s, d), F32), jax.ShapeDtypeStruct((nb, s, d), BF16)],
        scratch_shapes=[pltpu.VMEM((tm, d), BF16)],
        compiler_params=_cparams(("parallel", "parallel"), 60),
        name="chunk_mlp",
    )(x, mods, g, g_ffn, w_in, v_norm, w_s, bias, w_out)


def _rope_tables(n_tokens):
    t = np.arange(n_tokens)
    row = (t // GRID_W).astype(np.float32)
    col = (t % GRID_W).astype(np.float32)

    def axis_tables(pos, half):
        inv = (1.0 / (ROPE_BASE ** (np.arange(0, half, 2, dtype=np.float32) / half))).astype(np.float32)
        ang = (pos[:, None] * inv).astype(np.float32).astype(np.float64)
        c, s = np.cos(ang), np.sin(ang)
        return np.concatenate([c, c], axis=1), np.concatenate([-s, s], axis=1)

    def tables(dim):
        cr, sr = axis_tables(row, dim // 2)
        cc, sc = axis_tables(col, dim // 2)
        return np.concatenate([cr, cc], axis=1), np.concatenate([sr, sc], axis=1)

    c128, s128 = tables(HEAD_DIM)
    c64, s64 = tables(B_ROPE)
    pad = np.zeros((n_tokens, LANES - B_ROPE))
    c64 = np.concatenate([c64, pad], axis=1)
    s64 = np.concatenate([s64, pad], axis=1)
    return tuple(jnp.asarray(a, F32) for a in (c128, s128, c64, s64))


def kernel(x_prompt, x_sample, c, cache_gqa_k, cache_gqa_v, cache_mla_ckv, cache_mla_krope, c_ctx, ada_w, ada_b, norm_mix, norm_ffn, ffn_gate, ffn_up, ffn_down, attn_w_in, attn_q_norm, attn_k_norm, attn_kv_norm, attn_w_kv_up, attn_w_out, cmlp_w_in, cmlp_v_norm, cmlp_w_s, cmlp_b_s, cmlp_w_out, final_norm):
    n_p, s_p, d = x_prompt.shape
    n_s, s_s, _ = x_sample.shape
    past = cache_gqa_k.shape[2]

    cond = jnp.concatenate([c_ctx[None, :], c, jnp.zeros((MOD_ROWS - 1 - n_s, d), F32)], axis=0)
    mods = _ada_mod(cond, ada_w, ada_b)

    def rows(a):
        return a.reshape(a.shape[0], 1, a.shape[1])

    def heads_t(vt):
        return vt.reshape(vt.shape[0], B_HEADS, B_V, vt.shape[2])

    aw = _layout_attn_weights(attn_w_in, attn_w_kv_up)
    g_mix, g_ffn = rows(norm_mix), rows(norm_ffn)
    qn, kn, kvn = rows(attn_q_norm), rows(attn_k_norm), rows(attn_kv_norm)
    cw_s = cmlp_w_s.astype(BF16)
    c_vn = rows(cmlp_v_norm)

    def flat(a):
        return a.reshape(-1, a.shape[-1])

    def ffn_cast(layer):
        return [(flat(w), layer * w.shape[1], w.shape[1]) for w in (ffn_gate, ffn_up, ffn_down)]

    ffn_w = {}
    n_al, n_cl = attn_w_out.shape[0], cmlp_w_in.shape[0]
    whole = [(flat(w), 0, w.shape[0] * w.shape[1]) for w in (attn_w_out, cmlp_w_in, cmlp_w_out)]
    c_bias = jnp.repeat(jnp.swapaxes(cmlp_b_s, 1, 2), C_GROUP_W, axis=2)
    final_g = final_norm.reshape(1, d)

    xp = x_prompt.reshape(1, n_p * s_p, d)
    xs = x_sample
    rope = _rope_tables(s_s)
    cache_k = cache_gqa_k.reshape(n_s, -1, past, A_KV_W)
    cache_v = cache_gqa_v.reshape(n_s, -1, past, A_KV_W)

    states = ([], [], [], [])
    for i in range(DEPTH):
        j = i // 2
        if i % 2 == 0:
            qa, ka, vat, qb, kb, vbt = _attn_project(xs, mods, 1, i, j, g_mix, aw, qn, kn, kvn, rope=rope)
            kb_ctx, vbt_ctx = _ctx_kv_expand(cache_mla_ckv, cache_mla_krope, j, aw)
            oa, ffn_w[i] = _attention(
                qa, [(ka, vat, KV_LANES), (cache_k, cache_v, KV_CACHE)], seq_len=s_s, hb=A_GROUP, group=A_GROUP,
                dq=HEAD_DIM, tqb=2048, tq=TQ_ATTN, rolled=True, unroll=UNROLL_GQA, seg_layer=j, cast=ffn_cast(i))
            ob, cast_b = _attention(
                qb, [(kb, heads_t(vbt), KV_HEADS), (kb_ctx, heads_t(vbt_ctx), KV_HEADS)], seq_len=s_s, hb=1,
                group=1, dq=B_QK_PAD, tqb=4096, tq=TQ_ATTN, rolled=True, unroll=UNROLL_MLA,
                cast=ffn_cast(i + 1) + (whole if i == 0 else []))
            ffn_w[i + 1] = cast_b[:3]
            if i == 0:
                a_out = cast_b[3].reshape(n_al, -1, d)
                cw_in = cast_b[4].reshape(n_cl, d, -1)
                cw_out = cast_b[5].reshape(n_cl, -1, d)
            xs, xsb = _attn_out(xs, mods, 1, i, j, g_ffn, oa, ob, a_out)

            qa, ka, vat, qb, kb, vbt, st_k, st_v, st_c, st_r = _attn_project(
                xp, mods, 0, i, j, g_mix, aw, qn, kn, kvn)
            for lst, st in zip(states, (st_k, st_v, st_c, st_r)):
                lst.append(st)
            oa = _attention(qa, [(ka, vat, KV_LANES)], seq_len=s_p, hb=A_HEADS, group=A_GROUP, dq=HEAD_DIM,
                            tqb=s_p, tq=s_p, rolled=False)
            ob = _attention(qb, [(kb, heads_t(vbt), KV_HEADS)], seq_len=s_p, hb=B_HEADS, group=1, dq=B_QK_PAD,
                            tqb=s_p, tq=s_p, rolled=False)
            xp, xpb = _attn_out(xp, mods, 0, i, j, g_ffn, oa, ob, a_out)
        else:
            xp, xpb = _chunk_mlp(xp, mods, 0, i, j, g_mix, g_ffn, cw_in, c_vn, cw_s, c_bias, cw_out)
            xs, xsb = _chunk_mlp(xs, mods, 1, i, j, g_mix, g_ffn, cw_in, c_vn, cw_s, c_bias, cw_out)

        fin = final_g if i == DEPTH - 1 else None
        xp = _ffn(xp, xpb, mods, 0, i, *ffn_w[i], fin)
        xs = _ffn(xs, xsb, mods, 1, i, *ffn_w[i], fin)

    y_prompt = xp.reshape(n_p, s_p, d)
    st_k, st_v, st_c, st_r = (jnp.stack([a.reshape(n_p, s_p, a.shape[-1]) for a in lst], axis=1)
                              for lst in states)
    return (y_prompt, xs,
            st_k.reshape(n_p, -1, s_p, A_KV_HEADS, HEAD_DIM),
            st_v.reshape(n_p, -1, s_p, A_KV_HEADS, HEAD_DIM),
            st_c, st_r)
```

```python
import functools
from typing import NamedTuple

import numpy as np
import jax
import jax.numpy as jnp
from jax import lax
from jax.experimental import pallas as pl
from jax.experimental.pallas import tpu as pltpu

F32 = jnp.float32
BF16 = jnp.bfloat16

D_MODEL = 2048
DEPTH = 4
GRID_W = 64
ROPE_BASE = 10000.0
EPS = 1e-6
A_HEADS = 8
A_KV_HEADS = 2
A_GROUP = A_HEADS // A_KV_HEADS
HEAD_DIM = 128
A_SCALE = HEAD_DIM ** -0.5
B_HEADS = 8
B_NOPE = 128
B_ROPE = 64
B_V = 128
KV_RANK = 512
B_SCALE = (B_NOPE + B_ROPE) ** -0.5
B_QK_PAD = 256
A_Q_W = A_HEADS * HEAD_DIM
A_KV_W = A_KV_HEADS * HEAD_DIM
B_Q_W = B_HEADS * (B_NOPE + B_ROPE)
CHUNK = 128
C_GROUPS = 8
C_GROUP_W = D_MODEL // C_GROUPS
FFN_HIDDEN = 5632
LOG2E = 1.4426950408889634
LANES = 128

Z_QA = 0
Z_KA = Z_QA + A_Q_W
Z_VA = Z_KA + A_KV_W
Z_QB = Z_VA + A_KV_W
Z_CKV = Z_QB + B_HEADS * B_QK_PAD
Z_KR = Z_CKV + KV_RANK
Z_W = Z_KR + LANES

MOD_ROWS = 8
_NT_DIMS = (((1,), (1,)), ((), ()))

TM_FFN, TC_FFN = 1024, 512
TM_PROJ = 512
TM_ATTN_OUT = 512
TM_CMLP = 512
TN_ADA = 1024
TQ_ATTN = 256
UNROLL_GQA, UNROLL_MLA = 8, 4


def _cparams(sem, vmem_mib):
    return pltpu.CompilerParams(dimension_semantics=sem, vmem_limit_bytes=vmem_mib << 20)


def _layer_resident(shape, layer):
    n = len(shape)
    return pl.BlockSpec((None,) + tuple(shape), lambda *_: (layer,) + (0,) * n, pipeline_mode=pl.Buffered(1))


def _layer_row(width, layer):
    return pl.BlockSpec((None, 1, width), lambda *_: (layer, 0, 0))


def _mod_spec(layer, row0):
    return pl.BlockSpec((None, None, 6, D_MODEL), lambda b, *_: (layer, row0 + b, 0, 0))


def _rms(x):
    return x * lax.rsqrt(jnp.mean(x * x, axis=-1, keepdims=True) + EPS)


def _modulate(x, g, shift, scale):
    return (_rms(x) * g) * (1.0 + scale) + shift


def _swap_halves(v, n):
    lane = lax.broadcasted_iota(jnp.int32, v.shape, 1)
    fwd = pltpu.roll(v, LANES - n, 1)
    bwd = pltpu.roll(v, n, 1)
    return jnp.where((lane & (2 * n - 1)) < n, fwd, bwd)


def _ada_kernel(cond_ref, w_ref, b_ref, o_ref):
    c = cond_ref[...]
    a = (c * jax.nn.sigmoid(c)).astype(BF16)
    o_ref[...] = jnp.dot(a, w_ref[...].astype(BF16), preferred_element_type=F32) + b_ref[...]


def _ada_mod(cond, ada_w, ada_b):
    n6 = 6 * D_MODEL
    out = pl.pallas_call(
        _ada_kernel,
        grid=(DEPTH, n6 // TN_ADA),
        in_specs=[
            pl.BlockSpec((MOD_ROWS, D_MODEL), lambda l, n: (0, 0)),
            pl.BlockSpec((None, D_MODEL, TN_ADA), lambda l, n: (l, 0, n)),
            pl.BlockSpec((None, 1, TN_ADA), lambda l, n: (l, 0, n)),
        ],
        out_specs=pl.BlockSpec((None, MOD_ROWS, TN_ADA), lambda l, n: (l, 0, n)),
        out_shape=jax.ShapeDtypeStruct((DEPTH, MOD_ROWS, n6), F32),
        compiler_params=_cparams(("arbitrary", "arbitrary"), 40),
        name="ada_mod",
    )(cond, ada_w, ada_b.reshape(DEPTH, 1, n6))
    return out.reshape(DEPTH, MOD_ROWS, 6, D_MODEL)


def _ffn_kernel(*refs, n_chunks, final):
    if final:
        x_hbm, xb_ref, mod_ref, wg_ref, wu_ref, wd_ref, fn_ref, o_ref, x_buf, x_sem = refs
    else:
        x_hbm, xb_ref, mod_ref, wg_ref, wu_ref, wd_ref, o_ref, x_buf, x_sem = refs
    b, i, j = pl.program_id(0), pl.program_id(1), pl.program_id(2)
    tm = x_buf.shape[0]
    x_copy = pltpu.make_async_copy(x_hbm.at[b, pl.ds(pl.multiple_of(i * tm, tm), tm), :], x_buf, x_sem)

    @pl.when(j == 0)
    def _():
        x_copy.start()
        o_ref[...] = jnp.zeros_like(o_ref)

    xb = xb_ref[...]
    gate = jnp.dot(xb, wg_ref[...], preferred_element_type=F32)
    up = jnp.dot(xb, wu_ref[...], preferred_element_type=F32)
    h = (gate * jax.nn.sigmoid(gate) * up).astype(BF16)
    o_ref[...] += jnp.dot(h, wd_ref[...], preferred_element_type=F32)

    @pl.when(j == n_chunks - 1)
    def _():
        x_copy.wait()
        y = x_buf[...] + mod_ref[5:6, :] * o_ref[...]
        if final:
            y = _rms(y) * fn_ref[...]
        o_ref[...] = y


def _ffn(x, xb, mods, row0, layer, wg, wu, wd, final_g=None):
    nb, s, d = x.shape
    tm, tc = TM_FFN, TC_FFN
    n_chunks = FFN_HIDDEN // tc
    final = final_g is not None
    in_specs = [
        pl.BlockSpec(memory_space=pl.ANY),
        pl.BlockSpec((None, tm, d), lambda b, i, j: (b, i, 0)),
        _mod_spec(layer, row0),
        pl.BlockSpec((d, tc), lambda b, i, j: (0, j)),
        pl.BlockSpec((d, tc), lambda b, i, j: (0, j)),
        pl.BlockSpec((tc, d), lambda b, i, j: (j, 0)),
    ]
    args = [x, xb, mods, wg, wu, wd]
    if final:
        in_specs.append(pl.BlockSpec((1, d), lambda b, i, j: (0, 0)))
        args.append(final_g)
    return pl.pallas_call(
        functools.partial(_ffn_kernel, n_chunks=n_chunks, final=final),
        grid=(nb, s // tm, n_chunks),
        in_specs=in_specs,
        out_specs=pl.BlockSpec((None, tm, d), lambda b, i, j: (b, i, 0)),
        out_shape=jax.ShapeDtypeStruct((nb, s, d), F32),
        scratch_shapes=[pltpu.VMEM((tm, d), F32), pltpu.SemaphoreType.DMA(())],
        compiler_params=_cparams(("arbitrary", "arbitrary", "arbitrary"), 60),
        name="ffn_final" if final else "ffn",
    )(*args)


class _AttnWeights(NamedTuple):
    main: jax.Array
    va_t: jax.Array
    kv_k: jax.Array
    kv_vt: jax.Array


W_GROUP = 64


def _w_in_group_map():
    src = list(range(Z_QB // W_GROUP))
    per_head = (B_NOPE + B_ROPE) // W_GROUP
    for h in range(B_HEADS):
        first = (A_Q_W + 2 * A_KV_W) // W_GROUP + h * per_head
        src += list(range(first, first + per_head)) + [-1] * (B_QK_PAD // W_GROUP - per_head)
    first = (A_Q_W + 2 * A_KV_W + B_Q_W) // W_GROUP
    src += list(range(first, first + KV_RANK // W_GROUP))
    src += [first + KV_RANK // W_GROUP, -1]
    assert len(src) == Z_W // W_GROUP
    return src


W_GROUPS_PER_STEP = 6


def _w_in_layout_kernel(src, *refs):
    main_o = refs[-1]
    c = pl.program_id(1)
    parts = [jnp.where(src[W_GROUPS_PER_STEP * c + k] >= 0, refs[k][...], 0.0) for k in range(W_GROUPS_PER_STEP)]
    main_o[...] = jnp.concatenate(parts, axis=0).T.astype(BF16)


def _layout_attn_weights(w_in, w_kv_up):
    n_l, d, _ = w_in.shape
    w_t = jnp.swapaxes(w_in, 1, 2)
    src = jnp.asarray(_w_in_group_map(), jnp.int32)
    n_g = W_GROUPS_PER_STEP
    main = pl.pallas_call(
        _w_in_layout_kernel,
        grid_spec=pltpu.PrefetchScalarGridSpec(
            num_scalar_prefetch=1,
            grid=(n_l, Z_W // (n_g * W_GROUP)),
            in_specs=[pl.BlockSpec((None, W_GROUP, d),
                                   lambda l, c, s, k=k: (l, jnp.maximum(s[n_g * c + k], 0), 0))
                      for k in range(n_g)],
            out_specs=pl.BlockSpec((None, d, n_g * W_GROUP), lambda l, c, s: (l, 0, c)),
        ),
        out_shape=jax.ShapeDtypeStruct((n_l, d, Z_W), BF16),
        compiler_params=_cparams(("parallel", "parallel"), 32),
        name="w_in_layout",
    )(src, *([w_t] * n_g))
    va_t = w_t[:, Z_VA:Z_VA + A_KV_W, :].astype(BF16)
    kv4 = w_kv_up.reshape(n_l, KV_RANK, B_HEADS, B_NOPE + B_V)
    return _AttnWeights(
        main=main,
        va_t=va_t,
        kv_k=kv4[..., :B_NOPE].reshape(n_l, KV_RANK, -1).astype(BF16),
        kv_vt=jnp.swapaxes(kv4[..., B_NOPE:].reshape(n_l, KV_RANK, -1), 1, 2).astype(BF16))


def _proj_kernel(*refs, prompt):
    if prompt:
        (x_ref, mod_ref, g_ref, w_ref, wvat_ref, wkv_ref, wkvt_ref, qn_ref, kn_ref, kvn_ref, _, _, _, _,
         qa_o, ka_o, vat_o, qb_o, kb_o, vbt_o, sk_o, sv_o, sc_o, sr_o) = refs
    else:
        (x_ref, mod_ref, g_ref, w_ref, wvat_ref, wkv_ref, wkvt_ref, qn_ref, kn_ref, kvn_ref,
         c128_ref, s128_ref, c64_ref, s64_ref,
         qa_o, ka_o, vat_o, qb_o, kb_o, vbt_o) = refs

    half = x_ref.shape[0] // 2
    for r in range(2):
        rows = slice(r * half, (r + 1) * half)
        if not prompt:
            c128, s128 = c128_ref[rows, :], s128_ref[rows, :]
            c64, s64 = c64_ref[rows, :], s64_ref[rows, :]

        def rope128(v):
            return v if prompt else v * c128 + _swap_halves(v, 32) * s128

        def rope64(v):
            return v if prompt else v * c64 + _swap_halves(v, 16) * s64

        xb = _modulate(x_ref[rows, :], g_ref[...], mod_ref[0:1, :], mod_ref[1:2, :]).astype(BF16)
        za = jnp.dot(xb, w_ref[:, :Z_QB if prompt else Z_VA], preferred_element_type=F32)
        zb = jnp.dot(xb, w_ref[:, Z_QB:], preferred_element_type=F32)
        vat_o[:, rows] = lax.dot_general(wvat_ref[...], xb, _NT_DIMS, preferred_element_type=F32).astype(BF16)

        for h in range(A_HEADS):
            q = _rms(za[:, Z_QA + h * HEAD_DIM:Z_QA + (h + 1) * HEAD_DIM]) * qn_ref[...]
            qa_o[h, rows, :] = (rope128(q) * (A_SCALE * LOG2E)).astype(BF16)
        for h in range(A_KV_HEADS):
            sl = slice(h * HEAD_DIM, (h + 1) * HEAD_DIM)
            k = _rms(za[:, Z_KA + h * HEAD_DIM:Z_KA + (h + 1) * HEAD_DIM]) * kn_ref[...]
            if prompt:
                sk_o[r, :, sl] = k
            ka_o[rows, sl] = rope128(k).astype(BF16)
        if prompt:
            sv_o[r] = za[:, Z_VA:Z_VA + A_KV_W]
        for h in range(B_HEADS):
            base = h * B_QK_PAD
            qb_o[h, rows, :B_NOPE] = (zb[:, base:base + B_NOPE] * (B_SCALE * LOG2E)).astype(BF16)
            qb_o[h, rows, B_NOPE:] = (
                rope64(zb[:, base + B_NOPE:base + B_QK_PAD]) * (B_SCALE * LOG2E)).astype(BF16)

        ckv = _rms(zb[:, Z_CKV - Z_QB:Z_CKV - Z_QB + KV_RANK]) * kvn_ref[...]
        kr = zb[:, Z_KR - Z_QB:Z_KR - Z_QB + LANES]
        if prompt:
            sc_o[r] = ckv
            sr_o[r] = kr[:, :B_ROPE]
        krb = rope64(kr).astype(BF16)
        ckvb = ckv.astype(BF16)
        kn_all = jnp.dot(ckvb, wkv_ref[...], preferred_element_type=F32)
        for h in range(B_HEADS):
            kb_o[h, rows, :B_NOPE] = kn_all[:, h * B_NOPE:(h + 1) * B_NOPE].astype(BF16)
            kb_o[h, rows, B_NOPE:] = krb
        vbt_o[:, rows] = lax.dot_general(wkvt_ref[...], ckvb, _NT_DIMS, preferred_element_type=F32).astype(BF16)


def _attn_project(x, mods, row0, layer, attn_layer, g, w, qn, kn, kvn, rope=None, states=None):
    nb, s, d = x.shape
    tm = TM_PROJ
    prompt = rope is None

    def row(wd):
        return pl.BlockSpec((None, tm, wd), lambda b, i: (b, i, 0))

    def col(wd):
        return pl.BlockSpec((None, wd, tm), lambda b, i: (b, 0, i))

    def heads(n, wd):
        return pl.BlockSpec((None, n, tm, wd), lambda b, i: (b, 0, i, 0))

    in_specs = [
        row(d),
        _mod_spec(layer, row0),
        _layer_row(d, layer),
        _layer_resident((d, Z_W), attn_layer),
        _layer_resident((A_KV_W, d), attn_layer),
        _layer_resident((KV_RANK, B_HEADS * B_NOPE), attn_layer),
        _layer_resident((B_HEADS * B_V, KV_RANK), attn_layer),
        _layer_row(HEAD_DIM, attn_layer),
        _layer_row(HEAD_DIM, attn_layer),
        _layer_row(KV_RANK, attn_layer),
    ]
    args = [x, mods, g, w.main, w.va_t, w.kv_k, w.kv_vt, qn, kn, kvn]
    out_specs = [heads(A_HEADS, HEAD_DIM), row(A_KV_W), col(A_KV_W), heads(B_HEADS, B_QK_PAD),
                 heads(B_HEADS, B_QK_PAD), col(B_HEADS * B_V)]
    out_shape = [jax.ShapeDtypeStruct(shp, BF16) for shp in (
        (nb, A_HEADS, s, HEAD_DIM), (nb, s, A_KV_W), (nb, A_KV_W, s), (nb, B_HEADS, s, B_QK_PAD),
        (nb, B_HEADS, s, B_QK_PAD), (nb, B_HEADS * B_V, s))]
    aliases = {}
    if prompt:
        for st in states:
            n_seq, _, seq_len, wd = st.shape
            assert tm == 2 * seq_len and n_seq * seq_len == nb * s
            aliases[len(args)] = len(out_specs)
            in_specs.append(pl.BlockSpec(memory_space=pl.ANY))
            args.append(st)
            out_specs.append(pl.BlockSpec((2, None, seq_len, wd), lambda b, i: (i, attn_layer, 0, 0)))
            out_shape.append(jax.ShapeDtypeStruct(st.shape, F32))
    else:
        in_specs += [pl.BlockSpec((tm, LANES), lambda b, i: (i, 0))] * 4
        args += list(rope)
    return pl.pallas_call(
        functools.partial(_proj_kernel, prompt=prompt),
        grid=(nb, s // tm),
        in_specs=in_specs,
        out_specs=out_specs,
        out_shape=out_shape,
        input_output_aliases=aliases,
        compiler_params=_cparams(("parallel", "parallel"), 60),
        name="attn_proj_prompt" if prompt else "attn_proj_latent",
    )(*args)


def _ctx_kv_kernel(ckv_ref, kr_ref, wkv_ref, wkvt_ref, kb_o, vbt_o):
    ckvb = ckv_ref[...].astype(BF16)
    kn_all = jnp.dot(ckvb, wkv_ref[...], preferred_element_type=F32)
    krb = kr_ref[...].astype(BF16)
    zero = jnp.zeros((krb.shape[0], B_QK_PAD - B_NOPE - B_ROPE), BF16)
    for h in range(B_HEADS):
        kb_o[h, :, :B_NOPE] = kn_all[:, h * B_NOPE:(h + 1) * B_NOPE].astype(BF16)
        kb_o[h, :, B_NOPE:B_NOPE + B_ROPE] = krb
        kb_o[h, :, B_NOPE + B_ROPE:] = zero
    vbt_o[...] = lax.dot_general(wkvt_ref[...], ckvb, _NT_DIMS, preferred_element_type=F32).astype(BF16)


def _ctx_kv_expand(cache_ckv, cache_kr, attn_layer, w):
    nb, _, p, _ = cache_ckv.shape
    return pl.pallas_call(
        _ctx_kv_kernel,
        grid=(nb,),
        in_specs=[
            pl.BlockSpec((None, None, p, KV_RANK), lambda b: (b, attn_layer, 0, 0)),
            pl.BlockSpec((None, None, p, B_ROPE), lambda b: (b, attn_layer, 0, 0)),
            pl.BlockSpec((None, KV_RANK, B_HEADS * B_NOPE), lambda b: (attn_layer, 0, 0)),
            pl.BlockSpec((None, B_HEADS * B_V, KV_RANK), lambda b: (attn_layer, 0, 0)),
        ],
        out_specs=[
            pl.BlockSpec((None, B_HEADS, p, B_QK_PAD), lambda b: (b, 0, 0, 0)),
            pl.BlockSpec((None, B_HEADS * B_V, p), lambda b: (b, 0, 0)),
        ],
        out_shape=[
            jax.ShapeDtypeStruct((nb, B_HEADS, p, B_QK_PAD), BF16),
            jax.ShapeDtypeStruct((nb, B_HEADS * B_V, p), BF16),
        ],
        compiler_params=_cparams(("parallel",), 32),
        name="ctx_kv_expand",
    )(cache_ckv, cache_kr, w.kv_k, w.kv_vt)


KV_LANES = "lanes"
KV_HEADS = "heads"
KV_CACHE = "cache"


def _attn_kernel(*refs, seg_len, seg_layout, n_cast, unroll, hb, group, dq, dv, tq, rolled):
    nseg = len(seg_len)
    q_ref = refs[0]
    kv_refs = refs[1:1 + 2 * nseg]
    cast_in = refs[1 + 2 * nseg:1 + 2 * nseg + n_cast]
    o_ref = refs[1 + 2 * nseg + n_cast]
    cast_out = refs[2 + 2 * nseg + n_cast:2 + 2 * nseg + 2 * n_cast]
    s_ref, m_ref = refs[2 + 2 * nseg + 2 * n_cast:]
    for src, dst in zip(cast_in, cast_out):
        dst[...] = src[...].astype(BF16)
    n_sub = q_ref.shape[1] // tq
    n_items = hb * n_sub
    seg_off = [sum(seg_len[:i]) for i in range(nseg)]
    dynamic_kv = all(lay == KV_HEADS for lay in seg_layout)

    def split(i):
        if isinstance(i, int):
            return i // n_sub, (i % n_sub) * tq
        shift = n_sub.bit_length() - 1
        return lax.shift_right_logical(i, shift), pl.multiple_of((i & (n_sub - 1)) * tq, tq)

    def kv_head(h):
        if isinstance(h, int):
            return h // group
        return lax.shift_right_logical(h, group.bit_length() - 1) if dynamic_kv else 0

    def keys(s, kvh):
        ref = kv_refs[2 * s]
        k = ref[kvh] if seg_layout[s] == KV_HEADS else ref[:, kvh * dq:(kvh + 1) * dq]
        return k.astype(BF16)

    def values_t(s, kvh):
        ref = kv_refs[2 * s + 1]
        if seg_layout[s] == KV_HEADS:
            vt = ref[kvh]
        elif seg_layout[s] == KV_LANES:
            vt = ref[kvh * dv:(kvh + 1) * dv, :]
        else:
            vt = ref[:, kvh * dv:(kvh + 1) * dv].T
        return vt.astype(BF16)

    def score(i, slot):
        h, r = split(i)
        kvh = kv_head(h)
        q = q_ref[h, pl.ds(r, tq), :]
        m = None
        for s in range(nseg):
            st = lax.dot_general(keys(s, kvh), q, _NT_DIMS, preferred_element_type=F32)
            s_ref[slot, seg_off[s]:seg_off[s] + seg_len[s], :] = st
            ms = jnp.max(st, axis=0, keepdims=True)
            m = ms if m is None else jnp.maximum(m, ms)
        m_ref[slot] = m

    def finish(i, slot):
        h, r = split(i)
        kvh = kv_head(h)
        m = m_ref[slot]
        denom = None
        acc = None
        for s in range(nseg):
            p = jnp.exp2(s_ref[slot, seg_off[s]:seg_off[s] + seg_len[s], :] - m)
            ls = jnp.sum(p, axis=0, keepdims=True)
            os_ = jnp.dot(values_t(s, kvh), p.astype(BF16), preferred_element_type=F32)
            denom = ls if denom is None else denom + ls
            acc = os_ if acc is None else acc + os_
        o_ref[h, pl.ds(r, tq), :] = (acc / denom).T.astype(o_ref.dtype)

    if rolled:
        assert dynamic_kv or hb == group
        assert unroll % 2 == 0 and n_items % unroll == 0 and n_items >= 2 * unroll
        assert n_sub & (n_sub - 1) == 0 and group & (group - 1) == 0
        score(0, 0)

        def body(k, carry):
            for u in range(unroll):
                score(unroll * k + u + 1, (u + 1) % 2)
                finish(unroll * k + u, u % 2)
            return carry

        lax.fori_loop(0, n_items // unroll - 1, body, 0)
        for u in range(unroll - 1):
            score(n_items - unroll + u + 1, (u + 1) % 2)
            finish(n_items - unroll + u, u % 2)
        finish(n_items - 1, (unroll - 1) % 2)
    else:
        for i in range(n_items):
            score(i, i)
        for i in range(n_items):
            finish(i, i)


def _attention(q, segs, *, seq_len, hb, group, dq, tqb, tq, rolled, unroll=2, seg_layer=None, cast=()):
    nbq, n_heads, s_tot, _ = q.shape
    dv = HEAD_DIM
    per = s_tot // seq_len
    n_qb = seq_len // tqb
    nkv = hb // group
    in_specs = [pl.BlockSpec((None, hb, tqb, dq), lambda g, h, t: (g // per, h, (g % per) * n_qb + t, 0))]
    args = [q]
    seg_lens = []
    for k, v, layout in segs:
        if layout == KV_CACHE:
            sk = k.shape[2]
            in_specs.append(pl.BlockSpec((None, None, sk, nkv * dq), lambda g, h, t: (g, seg_layer, 0, h)))
            in_specs.append(pl.BlockSpec((None, None, sk, nkv * dv), lambda g, h, t: (g, seg_layer, 0, h)))
        elif layout == KV_LANES:
            sk = k.shape[1] // per
            in_specs.append(pl.BlockSpec((None, sk, nkv * dq), lambda g, h, t: (g // per, g % per, h)))
            in_specs.append(pl.BlockSpec((None, nkv * dv, sk), lambda g, h, t: (g // per, h, g % per)))
        else:
            sk = k.shape[2] // per
            in_specs.append(pl.BlockSpec((None, nkv, sk, dq), lambda g, h, t: (g // per, h, g % per, 0)))
            in_specs.append(pl.BlockSpec((None, nkv, dv, sk), lambda g, h, t: (g // per, h, 0, g % per)))
        seg_lens.append(sk)
        args += [k, v]
    grid = (nbq * per, n_heads // hb, n_qb)
    out_specs = [pl.BlockSpec((None, hb, tqb, dv), lambda g, h, t: (g // per, h, (g % per) * n_qb + t, 0))]
    out_shape = [jax.ShapeDtypeStruct((nbq, n_heads, s_tot, dv), BF16)]
    n_steps = grid[0] * grid[1] * grid[2]
    n_slots = 2 if rolled else hb * (tqb // tq)
    for a, first_row, n_rows in cast:
        slab = n_rows // n_steps
        assert n_rows % n_steps == 0 and first_row % slab == 0
        in_specs.append(pl.BlockSpec(
            (slab, a.shape[1]),
            lambda g, h, t, b0=first_row // slab: (b0 + (g * grid[1] + h) * grid[2] + t, 0)))
        out_specs.append(pl.BlockSpec((slab, a.shape[1]), lambda g, h, t: ((g * grid[1] + h) * grid[2] + t, 0)))
        out_shape.append(jax.ShapeDtypeStruct((n_rows, a.shape[1]), BF16))
        args.append(a)
    outs = pl.pallas_call(
        functools.partial(_attn_kernel, seg_len=tuple(seg_lens), seg_layout=tuple(s[2] for s in segs),
                          n_cast=len(cast), unroll=unroll, hb=hb, group=group, dq=dq, dv=dv, tq=tq,
                          rolled=rolled),
        grid=grid,
        in_specs=in_specs,
        out_specs=out_specs,
        out_shape=out_shape,
        scratch_shapes=[pltpu.VMEM((n_slots, sum(seg_lens), tq), F32), pltpu.VMEM((n_slots, 1, tq), F32)],
        compiler_params=_cparams(("parallel", "parallel", "arbitrary"), 56),
        name="attention",
    )(*args)
    return (outs[0], tuple(outs[1:])) if cast else outs[0]


def _close_mixer(x, y, mod_ref, gf_ref, o_ref, xb_o, rows=slice(None)):
    x_new = x + mod_ref[2:3, :] * y
    o_ref[rows, :] = x_new
    xb_o[rows, :] = _modulate(x_new, gf_ref[...], mod_ref[3:4, :], mod_ref[4:5, :]).astype(BF16)


def _attn_out_kernel(x_ref, mod_ref, gf_ref, oa_ref, ob_ref, w_ref, o_ref, xb_o):
    half = x_ref.shape[0] // 2
    for r in range(2):
        rows = slice(r * half, (r + 1) * half)
        cat = jnp.concatenate([oa_ref[h, rows, :] for h in range(A_HEADS)]
                              + [ob_ref[h, rows, :] for h in range(B_HEADS)], axis=-1)
        y = jnp.dot(cat, w_ref[...], preferred_element_type=F32)
        _close_mixer(x_ref[rows, :], y, mod_ref, gf_ref, o_ref, xb_o, rows)


def _attn_out(x, mods, row0, layer, attn_layer, g_ffn, oa, ob, w_out):
    nb, s, d = x.shape
    tm = TM_ATTN_OUT
    tile = pl.BlockSpec((None, tm, d), lambda b, i: (b, i, 0))
    return pl.pallas_call(
        _attn_out_kernel,
        grid=(nb, s // tm),
        in_specs=[
            tile,
            _mod_spec(layer, row0),
            _layer_row(d, layer),
            pl.BlockSpec((None, A_HEADS, tm, HEAD_DIM), lambda b, i: (b, 0, i, 0)),
            pl.BlockSpec((None, B_HEADS, tm, B_V), lambda b, i: (b, 0, i, 0)),
            _layer_resident((A_Q_W + B_HEADS * B_V, d), attn_layer),
        ],
        out_specs=[tile, tile],
        out_shape=[jax.ShapeDtypeStruct((nb, s, d), F32), jax.ShapeDtypeStruct((nb, s, d), BF16)],
        compiler_params=_cparams(("parallel", "parallel"), 44),
        name="attn_out",
    )(x, mods, g_ffn, oa, ob, w_out)


def _gelu_tanh(x):
    return 0.5 * x * (1.0 + jnp.tanh(0.7978845608028654 * (x + 0.044715 * (x * x * x))))


def _cmlp_kernel(x_ref, mod_ref, g_ref, gf_ref, win_ref, vn_ref, ws_ref, bias_ref, wout_ref, o_ref, xb_o, t_ref):
    half = x_ref.shape[0] // 2
    for r in range(2):
        x = x_ref[r * half:(r + 1) * half, :]
        xb = _modulate(x, g_ref[...], mod_ref[0:1, :], mod_ref[1:2, :]).astype(BF16)
        v = _gelu_tanh(jnp.dot(xb, win_ref[:, D_MODEL:], preferred_element_type=F32))
        vb = (_rms(v) * vn_ref[...]).astype(BF16)
        u = _gelu_tanh(jnp.dot(xb, win_ref[:, :D_MODEL], preferred_element_type=F32))
        for n in range(half // CHUNK):
            rows = slice(n * CHUNK, (n + 1) * CHUNK)
            trows = slice(r * half + n * CHUNK, r * half + (n + 1) * CHUNK)
            for g in range(C_GROUPS):
                cols = slice(g * C_GROUP_W, (g + 1) * C_GROUP_W)
                sv = jnp.dot(ws_ref[g], vb[rows, cols], preferred_element_type=F32) + bias_ref[:, cols]
                t_ref[trows, cols] = (u[rows, cols] * sv).astype(BF16)
        hrows = slice(r * half, (r + 1) * half)
        y = jnp.dot(t_ref[hrows, :], wout_ref[...], preferred_element_type=F32)
        _close_mixer(x, y, mod_ref, gf_ref, o_ref, xb_o, hrows)


def _chunk_mlp(x, mods, row0, layer, c_layer, g, g_ffn, w_in, v_norm, w_s, bias, w_out):
    nb, s, d = x.shape
    tm = TM_CMLP
    tile = pl.BlockSpec((None, tm, d), lambda b, i: (b, i, 0))
    return pl.pallas_call(
        _cmlp_kernel,
        grid=(nb, s // tm),
        in_specs=[
            tile,
            _mod_spec(layer, row0),
            _layer_row(d, layer),
            _layer_row(d, layer),
            _layer_resident((d, 2 * d), c_layer),
            _layer_row(d, c_layer),
            _layer_resident((C_GROUPS, CHUNK, CHUNK), c_layer),
            _layer_resident((CHUNK, d), c_layer),
            _layer_resident((d, d), c_layer),
        ],
        out_specs=[tile, tile],
        out_shape=[jax.ShapeDtypeStruct((nb, s, d), F32), jax.ShapeDtypeStruct((nb, s, d), BF16)],
        scratch_shapes=[pltpu.VMEM((tm, d), BF16)],
        compiler_params=_cparams(("parallel", "parallel"), 60),
        name="chunk_mlp",
    )(x, mods, g, g_ffn, w_in, v_norm, w_s, bias, w_out)


def _rope_tables(n_tokens):
    t = np.arange(n_tokens)
    row = (t // GRID_W).astype(np.float32)
    col = (t % GRID_W).astype(np.float32)

    def axis_tables(pos, half):
        inv = (1.0 / (ROPE_BASE ** (np.arange(0, half, 2, dtype=np.float32) / half))).astype(np.float32)
        ang = (pos[:, None] * inv).astype(np.float32).astype(np.float64)
        c, s = np.cos(ang), np.sin(ang)
        return np.concatenate([c, c], axis=1), np.concatenate([-s, s], axis=1)

    def tables(dim):
        cr, sr = axis_tables(row, dim // 2)
        cc, sc = axis_tables(col, dim // 2)
        return np.concatenate([cr, cc], axis=1), np.concatenate([sr, sc], axis=1)

    c128, s128 = tables(HEAD_DIM)
    c64, s64 = tables(B_ROPE)
    pad = np.zeros((n_tokens, LANES - B_ROPE))
    c64 = np.concatenate([c64, pad], axis=1)
    s64 = np.concatenate([s64, pad], axis=1)
    return tuple(jnp.asarray(a, F32) for a in (c128, s128, c64, s64))


def kernel(x_prompt, x_sample, c, cache_gqa_k, cache_gqa_v, cache_mla_ckv, cache_mla_krope, c_ctx, ada_w, ada_b, norm_mix, norm_ffn, ffn_gate, ffn_up, ffn_down, attn_w_in, attn_q_norm, attn_k_norm, attn_kv_norm, attn_w_kv_up, attn_w_out, cmlp_w_in, cmlp_v_norm, cmlp_w_s, cmlp_b_s, cmlp_w_out, final_norm):
    n_p, s_p, d = x_prompt.shape
    n_s, s_s, _ = x_sample.shape
    past = cache_gqa_k.shape[2]

    cond = jnp.concatenate([c_ctx[None, :], c, jnp.zeros((MOD_ROWS - 1 - n_s, d), F32)], axis=0)
    mods = _ada_mod(cond, ada_w, ada_b)

    def rows(a):
        return a.reshape(a.shape[0], 1, a.shape[1])

    def heads_t(vt):
        return vt.reshape(vt.shape[0], B_HEADS, B_V, vt.shape[2])

    aw = _layout_attn_weights(attn_w_in, attn_w_kv_up)
    g_mix, g_ffn = rows(norm_mix), rows(norm_ffn)
    qn, kn, kvn = rows(attn_q_norm), rows(attn_k_norm), rows(attn_kv_norm)
    cw_s = cmlp_w_s.astype(BF16)
    c_vn = rows(cmlp_v_norm)

    def flat(a):
        return a.reshape(-1, a.shape[-1])

    def ffn_cast(layer):
        return [(flat(w), layer * w.shape[1], w.shape[1]) for w in (ffn_gate, ffn_up, ffn_down)]

    ffn_w = {}
    n_al, n_cl = attn_w_out.shape[0], cmlp_w_in.shape[0]
    whole = [(flat(w), 0, w.shape[0] * w.shape[1]) for w in (attn_w_out, cmlp_w_in, cmlp_w_out)]
    c_bias = jnp.repeat(jnp.swapaxes(cmlp_b_s, 1, 2), C_GROUP_W, axis=2)
    final_g = final_norm.reshape(1, d)

    xp = x_prompt.reshape(1, n_p * s_p, d)
    xs = x_sample
    rope = _rope_tables(s_s)
    cache_k = cache_gqa_k.reshape(n_s, -1, past, A_KV_W)
    cache_v = cache_gqa_v.reshape(n_s, -1, past, A_KV_W)

    states = tuple(jnp.zeros((n_p, n_al, s_p, wd), F32) for wd in (A_KV_W, A_KV_W, KV_RANK, B_ROPE))
    for i in range(DEPTH):
        j = i // 2
        if i % 2 == 0:
            qa, ka, vat, qb, kb, vbt = _attn_project(xs, mods, 1, i, j, g_mix, aw, qn, kn, kvn, rope=rope)
            kb_ctx, vbt_ctx = _ctx_kv_expand(cache_mla_ckv, cache_mla_krope, j, aw)
            oa, ffn_w[i] = _attention(
                qa, [(ka, vat, KV_LANES), (cache_k, cache_v, KV_CACHE)], seq_len=s_s, hb=A_GROUP, group=A_GROUP,
                dq=HEAD_DIM, tqb=2048, tq=TQ_ATTN, rolled=True, unroll=UNROLL_GQA, seg_layer=j, cast=ffn_cast(i))
            ob, cast_b = _attention(
                qb, [(kb, heads_t(vbt), KV_HEADS), (kb_ctx, heads_t(vbt_ctx), KV_HEADS)], seq_len=s_s, hb=1,
                group=1, dq=B_QK_PAD, tqb=4096, tq=TQ_ATTN, rolled=True, unroll=UNROLL_MLA,
                cast=ffn_cast(i + 1) + (whole if i == 0 else []))
            ffn_w[i + 1] = cast_b[:3]
            if i == 0:
                a_out = cast_b[3].reshape(n_al, -1, d)
                cw_in = cast_b[4].reshape(n_cl, d, -1)
                cw_out = cast_b[5].reshape(n_cl, -1, d)
            xs, xsb = _attn_out(xs, mods, 1, i, j, g_ffn, oa, ob, a_out)

            qa, ka, vat, qb, kb, vbt, *states = _attn_project(
                xp, mods, 0, i, j, g_mix, aw, qn, kn, kvn, states=states)
            oa = _attention(qa, [(ka, vat, KV_LANES)], seq_len=s_p, hb=A_HEADS, group=A_GROUP, dq=HEAD_DIM,
                            tqb=s_p, tq=s_p, rolled=False)
            ob = _attention(qb, [(kb, heads_t(vbt), KV_HEADS)], seq_len=s_p, hb=B_HEADS, group=1, dq=B_QK_PAD,
                            tqb=s_p, tq=s_p, rolled=False)
            xp, xpb = _attn_out(xp, mods, 0, i, j, g_ffn, oa, ob, a_out)
        else:
            xp, xpb = _chunk_mlp(xp, mods, 0, i, j, g_mix, g_ffn, cw_in, c_vn, cw_s, c_bias, cw_out)
            xs, xsb = _chunk_mlp(xs, mods, 1, i, j, g_mix, g_ffn, cw_in, c_vn, cw_s, c_bias, cw_out)

        fin = final_g if i == DEPTH - 1 else None
        xp = _ffn(xp, xpb, mods, 0, i, *ffn_w[i], fin)
        xs = _ffn(xs, xsb, mods, 1, i, *ffn_w[i], fin)

    y_prompt = xp.reshape(n_p, s_p, d)
    st_k, st_v, st_c, st_r = states
    return (y_prompt, xs,
            st_k.reshape(n_p, -1, s_p, A_KV_HEADS, HEAD_DIM),
            st_v.reshape(n_p, -1, s_p, A_KV_HEADS, HEAD_DIM),
            st_c, st_r)
```
